```python
import jax, jax.numpy as jnp
from jax import lax
import numpy as np

D_MODEL = 1024
BATCH = 32
SEQ = 2048
DEPTH = 2

GRID_W = 64
CTX_LEN = 256
N_MOD = 6
EPS = 1e-6
ROPE_BASE = 10000.0
Q_BLOCK = 128
HEAD_DIM = 64
GQA_Q_HEADS = 6
GQA_KV_HEADS = 2
GQA_GROUP = GQA_Q_HEADS // GQA_KV_HEADS
GQA_WIDTH = GQA_Q_HEADS * HEAD_DIM
POOL_GROUPS = 4
POOL_WINDOWS = (2, 4, 8, 16)
POOL_WIDTH = D_MODEL // 4
POOL_GROUP_DIM = POOL_WIDTH // POOL_GROUPS
MLA_HEADS = 6
MLA_NOPE_DIM = 64
MLA_ROPE_DIM = 32
MLA_V_DIM = 64
MLA_Q_RANK = 384
MLA_KV_RANK = 256
MLA_WIDTH = MLA_HEADS * MLA_V_DIM
MIX_WIDTH = GQA_WIDTH + POOL_WIDTH + MLA_WIDTH
PROJ_SIZES = (GQA_WIDTH, GQA_KV_HEADS * HEAD_DIM, GQA_KV_HEADS * HEAD_DIM, POOL_WIDTH, MLA_Q_RANK, MLA_KV_RANK, MLA_ROPE_DIM)
IN_COLS = GQA_WIDTH + 4 * GQA_KV_HEADS * HEAD_DIM // 2 + POOL_WIDTH + MLA_Q_RANK + MLA_KV_RANK + MLA_ROPE_DIM
PEER_HEADS = 8
PEER_N_KEYS = 128
PEER_N_EXPERTS = PEER_N_KEYS * PEER_N_KEYS
PEER_TOPK = 16
PEER_QUERY_DIM = 256
PEER_HALF = PEER_QUERY_DIM // 2
PEER_CHUNK = 128

kernel_name = "hybrid_prefix_dit_gqa_pool_mla_peer"


def rms_norm(x, g):
    xf = x.astype(jnp.float32)
    y = xf * lax.rsqrt(jnp.mean(xf * xf, axis=-1, keepdims=True) + EPS)
    return (y * g.astype(jnp.float32)).astype(x.dtype)


def modulate(x, g, shift, scale):
    return rms_norm(x, g) * (1 + scale) + shift


def axial_rope(rows, rot_dim):
    row = jnp.repeat(jnp.arange(rows), GRID_W).astype(jnp.float32)
    col = jnp.tile(jnp.arange(GRID_W), rows).astype(jnp.float32)
    n = rot_dim // 4
    inv = ROPE_BASE ** (-jnp.arange(n, dtype=jnp.float32) / n)
    ang = jnp.concatenate([row[:, None] * inv, col[:, None] * inv], axis=-1)
    return jnp.cos(ang), jnp.sin(ang)


def apply_rope(x, cos, sin):
    L = x.shape[1]
    shape = (1, L) + (1,) * (x.ndim - 3) + (cos.shape[-1],)
    cos = cos.reshape(shape).astype(x.dtype)
    sin = sin.reshape(shape).astype(x.dtype)
    x1, x2 = jnp.split(x, 2, axis=-1)
    return jnp.concatenate([x1 * cos - x2 * sin, x1 * sin + x2 * cos], axis=-1)


def split_cols(p):
    outs = []
    o = 0
    for s in PROJ_SIZES:
        outs.append(p[..., o:o + s])
        o += s
    return outs


def blocked_attention(q, k, v, scale):
    B, Lq = q.shape[0], q.shape[1]
    nb = Lq // Q_BLOCK
    qb = q.reshape((B, nb, Q_BLOCK) + q.shape[2:]).swapaxes(0, 1)

    def one_block(qblk):
        s = jnp.einsum('bqhgd,bkhd->bhgqk', qblk, k, preferred_element_type=jnp.float32) * scale
        p = jax.nn.softmax(s, axis=-1).astype(v.dtype)
        return jnp.einsum('bhgqk,bkhd->bqhgd', p, v)

    o = lax.map(one_block, qb)
    return o.swapaxes(0, 1).reshape(B, Lq, -1)


def gqa_heads(q, k, v, qn_g, kn_g, rope):
    B, L, _ = q.shape
    q = rms_norm(q.reshape(B, L, GQA_KV_HEADS, GQA_GROUP, HEAD_DIM), qn_g)
    k = rms_norm(k.reshape(B, L, GQA_KV_HEADS, HEAD_DIM), kn_g)
    v = v.reshape(B, L, GQA_KV_HEADS, HEAD_DIM)
    if rope is not None:
        q = apply_rope(q, rope[0], rope[1])
        k = apply_rope(k, rope[0], rope[1])
    return q, k, v


def mla_heads(cq, ckv, kr, qn_g, kvn_g, w_uq, w_ukv, rope):
    B, L, _ = cq.shape
    q = (rms_norm(cq, qn_g) @ w_uq).reshape(B, L, MLA_HEADS, 1, MLA_NOPE_DIM + MLA_ROPE_DIM)
    kv = (rms_norm(ckv, kvn_g) @ w_ukv).reshape(B, L, MLA_HEADS, MLA_NOPE_DIM + MLA_V_DIM)
    q_nope, q_rope = q[..., :MLA_NOPE_DIM], q[..., MLA_NOPE_DIM:]
    k_nope, v = kv[..., :MLA_NOPE_DIM], kv[..., MLA_NOPE_DIM:]
    kr = kr[:, :, None, :]
    if rope is not None:
        q_rope = apply_rope(q_rope, rope[0], rope[1])
        kr = apply_rope(kr, rope[0], rope[1])
    q = jnp.concatenate([q_nope, q_rope], axis=-1)
    k = jnp.concatenate([k_nope, jnp.broadcast_to(kr, (B, L, MLA_HEADS, MLA_ROPE_DIM))], axis=-1)
    return q, k, v


def centred_mean_minus_self(x, w):
    B, L, C = x.shape
    xf = x.astype(jnp.float32)
    cs = jnp.concatenate([jnp.zeros((B, 1, C), jnp.float32), jnp.cumsum(xf, axis=1)], axis=1)
    t = jnp.arange(L)
    lo = jnp.clip(t - w // 2, 0, L)
    hi = jnp.clip(t - w // 2 + w, 0, L)
    cnt = (hi - lo).astype(jnp.float32)
    return ((cs[:, hi] - cs[:, lo]) / cnt[None, :, None] - xf).astype(x.dtype)


def pool_mixer(x, pool_w, pool_scale):
    B, L, _ = x.shape
    xg = x.reshape(B, L, POOL_GROUPS, POOL_GROUP_DIM)
    pooled = jnp.stack([centred_mean_minus_self(xg[:, :, i], POOL_WINDOWS[i]) for i in range(POOL_GROUPS)], axis=2)
    y = jnp.einsum('blgc,gcd->blgd', pooled, pool_w).reshape(B, L, POOL_WIDTH)
    return y * pool_scale


def peer_ffn(h, wq, subkeys, u_tab, v_tab):
    B, L, D = h.shape
    hc = h.reshape(B * L // PEER_CHUNK, PEER_CHUNK, D)

    def one_chunk(xc):
        C = xc.shape[0]
        q = (xc @ wq).reshape(C, PEER_HEADS, 2, PEER_HALF)
        s = jnp.einsum('chpd,hpnd->chpn', q, subkeys, preferred_element_type=jnp.float32)
        s1, i1 = lax.top_k(s[:, :, 0], PEER_TOPK)
        s2, i2 = lax.top_k(s[:, :, 1], PEER_TOPK)
        cand_s = (s1[..., :, None] + s2[..., None, :]).reshape(C, PEER_HEADS, PEER_TOPK * PEER_TOPK)
        cand_i = (i1[..., :, None] * PEER_N_KEYS + i2[..., None, :]).reshape(C, PEER_HEADS, PEER_TOPK * PEER_TOPK)
        top_s, pos = lax.top_k(cand_s, PEER_TOPK)
        eid = jnp.take_along_axis(cand_i, pos, axis=-1).reshape(C, PEER_HEADS * PEER_TOPK)
        gate = jax.nn.softmax(top_s, axis=-1).reshape(C, PEER_HEADS * PEER_TOPK)
        u = u_tab[eid]
        v = v_tab[eid]
        a = jnp.einsum('cd,ced->ce', xc, u, preferred_element_type=jnp.float32)
        wgt = (gate * jax.nn.gelu(a)).astype(xc.dtype)
        return jnp.einsum('ce,ced->cd', wgt, v)

    return lax.map(one_chunk, hc).reshape(B, L, D)


def mix_out(oa, ob, oc, w_out):
    return jnp.concatenate([oa, ob, oc], axis=-1) @ w_out


def trunk_layer(xl, xc, mod_l, mod_c, n1_g, n2_g, w_in, gqa_qn_g, gqa_kn_g, pool_w, pool_scale,
                mla_qn_g, mla_kvn_g, mla_w_uq, mla_w_ukv, w_out, peer_wq, peer_subkeys, peer_u, peer_v,
                rope_a, rope_c, update_ctx):
    shl, scl, gtl, shl2, scl2, gtl2 = [mod_l[:, i][:, None, :] for i in range(N_MOD)]
    shc, scc, gtc, shc2, scc2, gtc2 = [mod_c[i] for i in range(N_MOD)]
    hl = modulate(xl, n1_g, shl, scl)
    hc = modulate(xc, n1_g, shc, scc)
    aq_l, ak_l, av_l, b_l, cq_l, ckv_l, kr_l = split_cols(hl @ w_in)
    aq_c, ak_c, av_c, b_c, cq_c, ckv_c, kr_c = split_cols(hc @ w_in)
    qa_l, ka_l, va_l = gqa_heads(aq_l, ak_l, av_l, gqa_qn_g, gqa_kn_g, rope_a)
    qa_c, ka_c, va_c = gqa_heads(aq_c, ak_c, av_c, gqa_qn_g, gqa_kn_g, None)
    sa = HEAD_DIM ** -0.5
    oa_l = blocked_attention(qa_l, jnp.concatenate([ka_c, ka_l], axis=1), jnp.concatenate([va_c, va_l], axis=1), sa)
    qc_l, kc_l, vc_l = mla_heads(cq_l, ckv_l, kr_l, mla_qn_g, mla_kvn_g, mla_w_uq, mla_w_ukv, rope_c)
    qc_c, kc_c, vc_c = mla_heads(cq_c, ckv_c, kr_c, mla_qn_g, mla_kvn_g, mla_w_uq, mla_w_ukv, None)
    sc = (MLA_NOPE_DIM + MLA_ROPE_DIM) ** -0.5
    oc_l = blocked_attention(qc_l, jnp.concatenate([kc_c, kc_l], axis=1), jnp.concatenate([vc_c, vc_l], axis=1), sc)
    ob_l = pool_mixer(b_l, pool_w, pool_scale)
    xl_new = xl + gtl * mix_out(oa_l, ob_l, oc_l, w_out)
    xl_new = xl_new + gtl2 * peer_ffn(modulate(xl_new, n2_g, shl2, scl2), peer_wq, peer_subkeys, peer_u, peer_v)
    if update_ctx:
        oa_c = blocked_attention(qa_c, ka_c, va_c, sa)
        oc_c = blocked_attention(qc_c, kc_c, vc_c, sc)
        ob_c = pool_mixer(b_c, pool_w, pool_scale)
        xc_new = xc + gtc * mix_out(oa_c, ob_c, oc_c, w_out)
        xc_new = xc_new + gtc2 * peer_ffn(modulate(xc_new, n2_g, shc2, scc2), peer_wq, peer_subkeys, peer_u, peer_v)
    else:
        xc_new = xc
    return xl_new, xc_new


def setup_inputs(seed: int = 0) -> dict:
    key = jax.random.key(seed)
    ks = jax.random.split(key, 32)
    D = D_MODEL

    def nrm(k, shape, scale):
        return jax.random.normal(k, shape, jnp.float32) * scale

    def gain(k, shape):
        return 1.0 + 0.1 * jax.random.normal(k, shape, jnp.float32)

    return {
        "x": nrm(ks[0], (BATCH, SEQ, D), 1.0),
        "c": nrm(ks[1], (BATCH, D), 1.0),
        "ctx": nrm(ks[2], (BATCH, CTX_LEN, D), 1.0),
        "c_ctx": nrm(ks[3], (D,), 1.0),
        "ada_w": nrm(ks[4], (DEPTH, D, N_MOD * D), 0.5 * D ** -0.5),
        "ada_b": nrm(ks[5], (DEPTH, N_MOD * D), 0.02),
        "norm1_g": gain(ks[6], (DEPTH, D)),
        "norm2_g": gain(ks[7], (DEPTH, D)),
        "w_in": nrm(ks[8], (DEPTH, D, IN_COLS), D ** -0.5),
        "gqa_qn_g": gain(ks[9], (DEPTH, HEAD_DIM)),
        "gqa_kn_g": gain(ks[10], (DEPTH, HEAD_DIM)),
        "pool_w": nrm(ks[11], (DEPTH, POOL_GROUPS, POOL_GROUP_DIM, POOL_GROUP_DIM), POOL_GROUP_DIM ** -0.5),
        "pool_scale": gain(ks[12], (DEPTH, POOL_WIDTH)),
        "mla_qn_g": gain(ks[13], (DEPTH, MLA_Q_RANK)),
        "mla_kvn_g": gain(ks[14], (DEPTH, MLA_KV_RANK)),
        "mla_w_uq": nrm(ks[15], (DEPTH, MLA_Q_RANK, MLA_HEADS * (MLA_NOPE_DIM + MLA_ROPE_DIM)), MLA_Q_RANK ** -0.5),
        "mla_w_ukv": nrm(ks[16], (DEPTH, MLA_KV_RANK, MLA_HEADS * (MLA_NOPE_DIM + MLA_V_DIM)), MLA_KV_RANK ** -0.5),
        "w_out": nrm(ks[17], (DEPTH, MIX_WIDTH, D), MIX_WIDTH ** -0.5),
        "peer_wq": nrm(ks[18], (DEPTH, D, PEER_HEADS * PEER_QUERY_DIM), D ** -0.5),
        "peer_subkeys": nrm(ks[19], (DEPTH, PEER_HEADS, 2, PEER_N_KEYS, PEER_HALF), PEER_HALF ** -0.5),
        "peer_u": nrm(ks[20], (DEPTH, PEER_N_EXPERTS, D), D ** -0.5),
        "peer_v": nrm(ks[21], (DEPTH, PEER_N_EXPERTS, D), PEER_HEADS ** -0.5),
        "final_g": gain(ks[22], (D,)),
    }


def reference(x, c, ctx, c_ctx, ada_w, ada_b, norm1_g, norm2_g, w_in, gqa_qn_g, gqa_kn_g, pool_w, pool_scale,
              mla_qn_g, mla_kvn_g, mla_w_uq, mla_w_ukv, w_out, peer_wq, peer_subkeys, peer_u, peer_v, final_g):
    B, L, D = x.shape
    rows = L // GRID_W
    rope_a = axial_rope(rows, HEAD_DIM)
    rope_c = axial_rope(rows, MLA_ROPE_DIM)
    sc_lat = jax.nn.silu(c)
    sc_ctx = jax.nn.silu(c_ctx)
    xl, xc = x, ctx
    for layer in range(DEPTH):
        mod_l = (sc_lat @ ada_w[layer] + ada_b[layer]).reshape(B, N_MOD, D)
        mod_c = (sc_ctx @ ada_w[layer] + ada_b[layer]).reshape(N_MOD, D)
        xl, xc = trunk_layer(xl, xc, mod_l, mod_c, norm1_g[layer], norm2_g[layer], w_in[layer],
                             gqa_qn_g[layer], gqa_kn_g[layer], pool_w[layer], pool_scale[layer],
                             mla_qn_g[layer], mla_kvn_g[layer], mla_w_uq[layer], mla_w_ukv[layer], w_out[layer],
                             peer_wq[layer], peer_subkeys[layer], peer_u[layer], peer_v[layer],
                             rope_a, rope_c, layer < DEPTH - 1)
    return rms_norm(xl, final_g)
```

```python
import functools

import jax
import jax.numpy as jnp
from jax import lax
from jax.experimental import pallas as pl
from jax.experimental.pallas import tpu as pltpu

D_MODEL = 1024
GRID_W = 64
N_MOD = 6
EPS = 1e-6
ROPE_BASE = 10000.0
HEAD_DIM = 64
GQA_Q_HEADS = 6
GQA_KV_HEADS = 2
GQA_GROUP = GQA_Q_HEADS // GQA_KV_HEADS
POOL_GROUPS = 4
POOL_WINDOWS = (2, 4, 8, 16)
POOL_WIDTH = D_MODEL // 4
POOL_GROUP_DIM = POOL_WIDTH // POOL_GROUPS
MLA_HEADS = 6
MLA_NOPE_DIM = 64
MLA_ROPE_DIM = 32
MLA_V_DIM = 64
MLA_Q_RANK = 384
MLA_KV_RANK = 256
PEER_HEADS = 8
PEER_N_KEYS = 128
PEER_N_EXPERTS = PEER_N_KEYS * PEER_N_KEYS
PEER_TOPK = 16
PEER_QUERY_DIM = 256
PEER_HALF = PEER_QUERY_DIM // 2

LANES = 128
N_ATT_HEADS = GQA_Q_HEADS + MLA_HEADS
QK_WIDTH = N_ATT_HEADS * LANES
V_WIDTH = N_ATT_HEADS * HEAD_DIM
POOL_PAD = 16

_C_QA = 0
_C_KA = _C_QA + GQA_Q_HEADS * LANES
_C_VA = _C_KA + GQA_Q_HEADS * LANES
_C_B = _C_VA + GQA_Q_HEADS * HEAD_DIM
_C_CQ = _C_B + POOL_WIDTH
_C_CKV = _C_CQ + MLA_Q_RANK
_C_KR = _C_CKV + MLA_KV_RANK
IN_WIDE = _C_KR + MLA_HEADS * LANES

PEER_TOK = 512
PEER_EBLK = 1024
PEER_I1_PER_BLK = PEER_EBLK // PEER_N_KEYS
VMEM_LIMIT = 56 * 1024 * 1024

BF = jnp.bfloat16
F32 = jnp.float32
NEG_INF = float("-inf")
POS_INF = float("inf")


def _cparams(sem):
    return pltpu.CompilerParams(dimension_semantics=sem, vmem_limit_bytes=VMEM_LIMIT)


def _dot(a, b):
    return jnp.dot(a, b, preferred_element_type=F32)


def _dot_nt(a, b):
    return lax.dot_general(a, b, (((1,), (1,)), ((), ())), preferred_element_type=F32)


def _split_bf16(a):
    hi = a.astype(BF)
    lo = (a - hi.astype(F32)).astype(BF)
    return hi, lo


def _dot3(a, b):
    ah, al = _split_bf16(a)
    bh, bl = _split_bf16(b)
    return _dot(ah, bh) + _dot(ah, bl) + _dot(al, bh)


def _rms(x):
    return x * lax.rsqrt(jnp.mean(x * x, axis=-1, keepdims=True) + EPS)


def _adaln_kernel(c_ref, w_ref, b_ref, o_ref):
    c = c_ref[...]
    s = c * (1.0 / (1.0 + jnp.exp(-c)))
    o_ref[...] = _dot3(s, w_ref[...]) + b_ref[...]


def _adaln(cc, ada_w, ada_b):
    depth, d, nd = ada_w.shape
    r = cc.shape[0]
    nblk = nd // d
    return pl.pallas_call(
        _adaln_kernel,
        out_shape=jax.ShapeDtypeStruct((depth, r, nd), F32),
        grid=(depth, nblk),
        in_specs=[
            pl.BlockSpec((r, d), lambda l, j: (0, 0)),
            pl.BlockSpec((None, d, d), lambda l, j: (l, 0, j)),
            pl.BlockSpec((None, 1, d), lambda l, j: (l, 0, j)),
        ],
        out_specs=pl.BlockSpec((None, r, d), lambda l, j: (l, 0, j)),
        compiler_params=_cparams(("arbitrary", "arbitrary")),
        name="adaln",
    )(cc, ada_w, ada_b.reshape(depth, 1, nd))


def _rope_block(xb, cb, sb, half):
    lane = lax.broadcasted_iota(jnp.int32, xb.shape, 1)
    fwd = pltpu.roll(xb, LANES - half, 1)
    bwd = pltpu.roll(xb, half, 1)
    sw = jnp.where((lane & half) == 0, fwd, bwd)
    return xb * cb + sw * sb


def _inproj_kernel(*refs, rope):
    if rope:
        (x_ref, mod_ref, n1_ref, w_ref, qg_ref, kg_ref, cqg_ref, ckvg_ref, wuq_ref, wuk_ref, wuv_ref,
         ca_ref, sa_ref, cc_ref, sc_ref, q_ref, k_ref, v_ref, b_ref) = refs
    else:
        (x_ref, mod_ref, n1_ref, w_ref, qg_ref, kg_ref, cqg_ref, ckvg_ref, wuq_ref, wuk_ref, wuv_ref,
         q_ref, k_ref, v_ref, b_ref) = refs
    x = x_ref[...]
    shift = mod_ref[0:1, :]
    scale = mod_ref[1:2, :]
    h = _rms(x) * n1_ref[...] * (1.0 + scale) + shift
    p = _dot(h.astype(BF), w_ref[...])

    sa = HEAD_DIM ** -0.5
    sc = (MLA_NOPE_DIM + MLA_ROPE_DIM) ** -0.5

    for hh in range(GQA_Q_HEADS):
        for (c0, g_ref, o_ref, mul) in ((_C_QA, qg_ref, q_ref, sa), (_C_KA, kg_ref, k_ref, 1.0)):
            blk = p[:, c0 + hh * LANES:c0 + (hh + 1) * LANES]
            ms = jnp.sum(blk * blk, axis=-1, keepdims=True) * (1.0 / HEAD_DIM)
            y = blk * lax.rsqrt(ms + EPS) * g_ref[...]
            if rope:
                y = _rope_block(y, ca_ref[...], sa_ref[...], HEAD_DIM // 2)
            if mul != 1.0:
                y = y * mul
            o_ref[:, hh * LANES:(hh + 1) * LANES] = y.astype(BF)
    v_ref[:, 0:GQA_Q_HEADS * HEAD_DIM] = p[:, _C_VA:_C_VA + GQA_Q_HEADS * HEAD_DIM].astype(BF)
    b_ref[...] = p[:, _C_B:_C_B + POOL_WIDTH]

    cq = _rms(p[:, _C_CQ:_C_CQ + MLA_Q_RANK]) * cqg_ref[...]
    ckv = (_rms(p[:, _C_CKV:_C_CKV + MLA_KV_RANK]) * ckvg_ref[...]).astype(BF)
    qc = _dot(cq.astype(BF), wuq_ref[...])
    kc = _dot(ckv, wuk_ref[...]) + p[:, _C_KR:_C_KR + MLA_HEADS * LANES]
    vc = _dot(ckv, wuv_ref[...])
    base = GQA_Q_HEADS * LANES
    for hh in range(MLA_HEADS):
        qb = qc[:, hh * LANES:(hh + 1) * LANES]
        kb = kc[:, hh * LANES:(hh + 1) * LANES]
        if rope:
            qb = _rope_block(qb, cc_ref[...], sc_ref[...], MLA_ROPE_DIM // 2)
            kb = _rope_block(kb, cc_ref[...], sc_ref[...], MLA_ROPE_DIM // 2)
        q_ref[:, base + hh * LANES:base + (hh + 1) * LANES] = (qb * sc).astype(BF)
        k_ref[:, base + hh * LANES:base + (hh + 1) * LANES] = kb.astype(BF)
    v_ref[:, GQA_Q_HEADS * HEAD_DIM:V_WIDTH] = vc.astype(BF)


def _inproj(x, mod, mod_row0, lw, rope_tabs, tl):
    b, l, d = x.shape
    rope = rope_tabs is not None
    grid = (b, l // tl)

    def full(a):
        nd = a.ndim
        return pl.BlockSpec(a.shape, lambda i, j: (0,) * nd)

    if mod_row0 is None:
        mod_spec = pl.BlockSpec((None, N_MOD, d), lambda i, j: (i, 0, 0))
    else:
        mod_spec = pl.BlockSpec((None, N_MOD, d), lambda i, j: (mod_row0, 0, 0))
    weights = [lw["n1_g"], lw["w_in"], lw["gqa_qg"], lw["gqa_kg"], lw["mla_qg"], lw["mla_kvg"],
               lw["w_uq"], lw["w_uk"], lw["w_uv"]]
    in_specs = [pl.BlockSpec((None, tl, d), lambda i, j: (i, j, 0)), mod_spec] + [full(a) for a in weights]
    args = [x, mod] + weights
    if rope:
        in_specs += [pl.BlockSpec((tl, LANES), lambda i, j: (j, 0)) for _ in range(4)]
        args += list(rope_tabs)
    outs = [(QK_WIDTH, BF), (QK_WIDTH, BF), (V_WIDTH, BF), (POOL_WIDTH, F32)]
    return pl.pallas_call(
        functools.partial(_inproj_kernel, rope=rope),
        out_shape=[jax.ShapeDtypeStruct((b, l, w), dt) for w, dt in outs],
        grid=grid,
        in_specs=in_specs,
        out_specs=[pl.BlockSpec((None, tl, w), lambda i, j: (i, j, 0)) for w, _ in outs],
        compiler_params=_cparams(("parallel", "parallel")),
        name="inproj",
    )(*args)


def _attn_kernel(*refs, n_src):
    q_ref = refs[0]
    k_refs = refs[1:1 + n_src]
    v_refs = refs[1 + n_src:1 + 2 * n_src]
    o_ref = refs[1 + 2 * n_src]
    tq = q_ref.shape[0]
    lane = lax.broadcasted_iota(jnp.int32, (tq, LANES), 1)
    for jb in range(N_ATT_HEADS // 2):
        halves = []
        for n in (2 * jb, 2 * jb + 1):
            q = q_ref[:, n * LANES:(n + 1) * LANES]
            ss = [_dot_nt(q, k_ref[:, n * LANES:(n + 1) * LANES]) for k_ref in k_refs]
            m = ss[0].max(axis=-1, keepdims=True)
            for s in ss[1:]:
                m = jnp.maximum(m, s.max(axis=-1, keepdims=True))
            acc = None
            den = None
            for s, v_ref in zip(ss, v_refs):
                e = jnp.exp(s - m)
                dsum = e.sum(axis=-1, keepdims=True)
                pv = _dot(e.astype(BF), v_ref[:, jb * LANES:(jb + 1) * LANES])
                acc = pv if acc is None else acc + pv
                den = dsum if den is None else den + dsum
            halves.append(acc * (1.0 / den))
        o = jnp.where(lane < HEAD_DIM, halves[0], halves[1])
        o_ref[:, jb * LANES:(jb + 1) * LANES] = o.astype(BF)


def _attn(q, ks, vs, tq):
    b, l, _ = q.shape
    n_src = len(ks)
    in_specs = [pl.BlockSpec((None, tq, QK_WIDTH), lambda i, j: (i, j, 0))]
    in_specs += [pl.BlockSpec((None, k.shape[1], QK_WIDTH), lambda i, j: (i, 0, 0)) for k in ks]
    in_specs += [pl.BlockSpec((None, v.shape[1], V_WIDTH), lambda i, j: (i, 0, 0)) for v in vs]
    return pl.pallas_call(
        functools.partial(_attn_kernel, n_src=n_src),
        out_shape=jax.ShapeDtypeStruct((b, l, V_WIDTH), BF),
        grid=(b, l // tq),
        in_specs=in_specs,
        out_specs=pl.BlockSpec((None, tq, V_WIDTH), lambda i, j: (i, j, 0)),
        compiler_params=_cparams(("parallel", "arbitrary")),
        name="attn",
    )(q, *ks, *vs)


def _pool_kernel(b_ref, w_ref, s_ref, o_ref, xp_ref):
    l = b_ref.shape[0]
    x = b_ref[...]
    zeros = jnp.zeros((POOL_PAD, POOL_WIDTH), F32)
    xp_ref[0:POOL_PAD, :] = zeros
    xp_ref[POOL_PAD + l:POOL_PAD + l + POOL_PAD, :] = zeros
    xp_ref[POOL_PAD:POOL_PAD + l, :] = x
    t = lax.broadcasted_iota(jnp.int32, (l, LANES), 0)
    lane = lax.broadcasted_iota(jnp.int32, (l, LANES), 1)
    outs = []
    for half in range(POOL_WIDTH // LANES):
        w_small = POOL_WINDOWS[2 * half]
        w_big = POOL_WINDOWS[2 * half + 1]
        lo, hi = half * LANES, (half + 1) * LANES

        def win(j):
            return xp_ref[pl.ds(POOL_PAD + j, l), lo:hi]

        s_small = None
        for j in range(-(w_small // 2), w_small // 2):
            s_small = win(j) if s_small is None else s_small + win(j)
        s_big = s_small
        for j in range(-(w_big // 2), w_big // 2):
            if not (-(w_small // 2) <= j < w_small // 2):
                s_big = s_big + win(j)

        def cnt(w):
            lo_i = jnp.maximum(t - w // 2, 0)
            hi_i = jnp.minimum(t - w // 2 + w, l)
            return (hi_i - lo_i).astype(F32)

        left = lane < POOL_GROUP_DIM
        s = jnp.where(left, s_small, s_big)
        c = jnp.where(left, cnt(w_small), cnt(w_big))
        outs.append(s / c - x[:, lo:hi])
    pooled = jnp.concatenate(outs, axis=1).astype(BF)
    o_ref[...] = (_dot(pooled, w_ref[...]) * s_ref[...]).astype(BF)


def _pool(pb, w_bd, pscale):
    b, l, w = pb.shape
    return pl.pallas_call(
        _pool_kernel,
        out_shape=jax.ShapeDtypeStruct((b, l, w), BF),
        grid=(b,),
        in_specs=[
            pl.BlockSpec((None, l, w), lambda i: (i, 0, 0)),
            pl.BlockSpec((w, w), lambda i: (0, 0)),
            pl.BlockSpec((1, w), lambda i: (0, 0)),
        ],
        out_specs=pl.BlockSpec((None, l, w), lambda i: (i, 0, 0)),
        scratch_shapes=[pltpu.VMEM((l + 2 * POOL_PAD, w), F32)],
        compiler_params=_cparams(("parallel",)),
        name="pool",
    )(pb, w_bd, pscale)


def _outproj_kernel(x_ref, att_ref, ob_ref, mod_ref, n2_ref, w1_ref, w2_ref, xo_ref, h_ref):
    y = _dot(att_ref[...], w1_ref[...]) + _dot(ob_ref[...], w2_ref[...])
    xn = x_ref[...] + mod_ref[2:3, :] * y
    xo_ref[...] = xn
    h = _rms(xn) * n2_ref[...] * (1.0 + mod_ref[4:5, :]) + mod_ref[3:4, :]
    h_ref[...] = h.astype(BF)


def _outproj(x, att, ob, mod, mod_row0, lw, tl):
    b, l, d = x.shape
    if mod_row0 is None:
        mod_spec = pl.BlockSpec((None, N_MOD, d), lambda i, j: (i, 0, 0))
    else:
        mod_spec = pl.BlockSpec((None, N_MOD, d), lambda i, j: (mod_row0, 0, 0))

    def tile(w):
        return pl.BlockSpec((None, tl, w), lambda i, j: (i, j, 0))

    def full(a):
        return pl.BlockSpec(a.shape, lambda i, j: (0, 0))

    return pl.pallas_call(
        _outproj_kernel,
        out_shape=[jax.ShapeDtypeStruct((b, l, d), F32), jax.ShapeDtypeStruct((b, l, d), BF)],
        grid=(b, l // tl),
        in_specs=[tile(d), tile(V_WIDTH), tile(POOL_WIDTH), mod_spec, full(lw["n2_g"]), full(lw["w_o1"]),
                  full(lw["w_o2"])],
        out_specs=[tile(d), tile(d)],
        compiler_params=_cparams(("parallel", "parallel")),
        name="outproj",
    )(x, att, ob, mod, lw["n2_g"], lw["w_o1"], lw["w_o2"])


def _gelu_tanh(x):
    return 0.5 * x * (1.0 + jnp.tanh(0.7978845608028654 * (x + 0.044715 * (x * x * x))))


def _peer_prologue(hn_ref, wq_ref, sk_ref, thr_ref, e1_ref, s2_ref, e2_ref, s_ref, v_ref):
    t = hn_ref.shape[0]
    hn = hn_ref[...]
    sub16 = lax.broadcasted_iota(jnp.int32, (PEER_TOPK, t), 0).astype(F32)

    def head(h, carry):
        qh = _dot(hn, wq_ref[h]).astype(BF)
        for p in range(2):
            st = _dot_nt(sk_ref[h, p], qh[:, p * PEER_HALF:(p + 1) * PEER_HALF])
            s_ref[p] = st
            m = st.max(axis=0, keepdims=True)
            v_ref[p, 0:1, :] = m
            for k in range(1, PEER_TOPK):
                m = jnp.where(st < m, st, NEG_INF).max(axis=0, keepdims=True)
                v_ref[p, k:k + 1, :] = m
        v1 = v_ref[0]
        v2 = v_ref[1]

        def cand(a):
            return v1[a:a + 1, :] + v2

        def cand_max_below(m):
            best = None
            for a in range(PEER_TOPK):
                c = cand(a)
                if m is not None:
                    c = jnp.where(c < m, c, NEG_INF)
                c = c.max(axis=0, keepdims=True)
                best = c if best is None else jnp.maximum(best, c)
            return best

        c00 = cand_max_below(None)
        tau = lax.fori_loop(1, PEER_TOPK, lambda k, m: cand_max_below(m), c00)
        z = jnp.zeros((1, t), F32)
        s1 = s_ref[0]
        thr = jnp.full((PEER_N_KEYS, t), POS_INF, F32)
        for a in range(PEER_TOPK):
            c = cand(a)
            sel = c >= tau
            z = z + jnp.where(sel, jnp.exp(c - c00), 0.0).sum(axis=0, keepdims=True)
            cnt = jnp.where(sel, 1.0, 0.0).sum(axis=0, keepdims=True)
            last = jnp.where(sub16 == cnt - 1.0, v2, 0.0).sum(axis=0, keepdims=True)
            thr_a = jnp.where(cnt > 0.0, last, POS_INF)
            thr = jnp.where(s1 == v1[a:a + 1, :], thr_a, thr)
        thr_ref[h] = thr
        e1_ref[h] = jnp.exp(s1 - v1[0:1, :]) * (1.0 / z)
        s2 = s_ref[1]
        s2_ref[h] = s2
        e2_ref[h] = jnp.exp(s2 - v2[0:1, :])
        return carry

    lax.fori_loop(0, PEER_HEADS, head, 0)


def _peer_kernel(hn_ref, wq_ref, sk_ref, u_ref, vt_ref, o_ref,
                 acc_ref, at_ref, wt_ref, thr_ref, e1_ref, s2_ref, e2_ref, s_ref, v_ref):
    e = pl.program_id(1)
    t = hn_ref.shape[0]

    @pl.when(e == 0)
    def _():
        _peer_prologue(hn_ref, wq_ref, sk_ref, thr_ref, e1_ref, s2_ref, e2_ref, s_ref, v_ref)
        acc_ref[...] = jnp.zeros_like(acc_ref)

    at_ref[...] = _dot_nt(u_ref[...], hn_ref[...])

    i1_0 = pl.multiple_of(e * PEER_I1_PER_BLK, PEER_I1_PER_BLK)
    for lg in range(t // LANES):
        ls = slice(lg * LANES, (lg + 1) * LANES)
        thr_s = [thr_ref[h, pl.ds(i1_0, PEER_I1_PER_BLK), ls] for h in range(PEER_HEADS)]
        e1_s = [e1_ref[h, pl.ds(i1_0, PEER_I1_PER_BLK), ls] for h in range(PEER_HEADS)]
        for j in range(PEER_I1_PER_BLK):
            rs = slice(j * PEER_N_KEYS, (j + 1) * PEER_N_KEYS)
            g = jnp.zeros((PEER_N_KEYS, LANES), F32)
            for h in range(PEER_HEADS):
                thr = thr_s[h][j:j + 1, :]
                e1 = e1_s[h][j:j + 1, :]
                g = g + e1 * jnp.where(s2_ref[h, :, ls] >= thr, e2_ref[h, :, ls], 0.0)
            wt_ref[rs, ls] = (_gelu_tanh(at_ref[rs, ls]) * g).astype(BF)
    acc_ref[...] += _dot(vt_ref[...], wt_ref[...])

    @pl.when(e == pl.num_programs(1) - 1)
    def _():
        o_ref[...] = acc_ref[...].T


def _peer(hn, lw):
    ntok, d = hn.shape
    t = PEER_TOK
    n_e = PEER_N_EXPERTS // PEER_EBLK
    tab = pltpu.VMEM((PEER_HEADS, PEER_N_KEYS, t), F32)
    return pl.pallas_call(
        _peer_kernel,
        out_shape=jax.ShapeDtypeStruct((ntok, d), F32),
        grid=(ntok // t, n_e),
        in_specs=[
            pl.BlockSpec((t, d), lambda i, e: (i, 0)),
            pl.BlockSpec(lw["peer_wq"].shape, lambda i, e: (0, 0, 0)),
            pl.BlockSpec(lw["peer_sk"].shape, lambda i, e: (0, 0, 0, 0)),
            pl.BlockSpec((PEER_EBLK, d), lambda i, e: (e, 0)),
            pl.BlockSpec((d, PEER_EBLK), lambda i, e: (0, e)),
        ],
        out_specs=pl.BlockSpec((t, d), lambda i, e: (i, 0)),
        scratch_shapes=[
            pltpu.VMEM((d, t), F32),
            pltpu.VMEM((PEER_EBLK, t), F32),
            pltpu.VMEM((PEER_EBLK, t), BF),
            tab, tab, tab, tab,
            pltpu.VMEM((2, PEER_N_KEYS, t), F32),
            pltpu.VMEM((2, PEER_TOPK, t), F32),
        ],
        compiler_params=_cparams(("parallel", "arbitrary")),
        name="peer",
    )(hn, lw["peer_wq"], lw["peer_sk"], lw["peer_u"], lw["peer_vt"])


def _resid_kernel(x_ref, p_ref, mod_ref, g_ref, o_ref, *, final):
    y = x_ref[...] + mod_ref[5:6, :] * p_ref[...]
    if final:
        y = _rms(y) * g_ref[...]
    o_ref[...] = y


def _resid(x, peer_out, mod, mod_row0, final_g, final, tl):
    b, l, d = x.shape
    if mod_row0 is None:
        mod_spec = pl.BlockSpec((None, N_MOD, d), lambda i, j: (i, 0, 0))
    else:
        mod_spec = pl.BlockSpec((None, N_MOD, d), lambda i, j: (mod_row0, 0, 0))
    tile = pl.BlockSpec((None, tl, d), lambda i, j: (i, j, 0))
    return pl.pallas_call(
        functools.partial(_resid_kernel, final=final),
        out_shape=jax.ShapeDtypeStruct((b, l, d), F32),
        grid=(b, l // tl),
        in_specs=[tile, tile, mod_spec, pl.BlockSpec((1, d), lambda i, j: (0, 0))],
        out_specs=tile,
        compiler_params=_cparams(("parallel", "parallel")),
        name="resid",
    )(x, peer_out, mod, final_g)


def _rope_tables(l):
    rows = l // GRID_W
    row = jnp.repeat(jnp.arange(rows), GRID_W).astype(F32)
    col = jnp.tile(jnp.arange(GRID_W), rows).astype(F32)

    def cs(rot_dim):
        n = rot_dim // 4
        inv = ROPE_BASE ** (-jnp.arange(n, dtype=F32) / n)
        ang = jnp.concatenate([row[:, None] * inv, col[:, None] * inv], axis=-1)
        return jnp.cos(ang), jnp.sin(ang)

    ones = lambda w: jnp.ones((l, w), F32)
    zeros = lambda w: jnp.zeros((l, w), F32)
    ca, sa = cs(HEAD_DIM)
    cos_a = jnp.concatenate([ca, ca, ones(LANES - HEAD_DIM)], axis=1)
    sin_a = jnp.concatenate([-sa, sa, zeros(LANES - HEAD_DIM)], axis=1)
    cc, sc = cs(MLA_ROPE_DIM)
    tail = LANES - MLA_NOPE_DIM - MLA_ROPE_DIM
    cos_c = jnp.concatenate([ones(MLA_NOPE_DIM), cc, cc, ones(tail)], axis=1)
    sin_c = jnp.concatenate([zeros(MLA_NOPE_DIM), -sc, sc, zeros(tail)], axis=1)
    return cos_a, sin_a, cos_c, sin_c


def _layer_weights(layer, norm1_g, norm2_g, w_in, gqa_qn_g, gqa_kn_g, pool_w, pool_scale, mla_qn_g, mla_kvn_g,
                   mla_w_uq, mla_w_ukv, w_out, peer_wq, peer_subkeys, peer_u, peer_v):
    d = D_MODEL
    w = w_in[layer]
    o = 0
    aq = w[:, o:o + GQA_Q_HEADS * HEAD_DIM]; o += GQA_Q_HEADS * HEAD_DIM
    ak = w[:, o:o + GQA_KV_HEADS * HEAD_DIM]; o += GQA_KV_HEADS * HEAD_DIM
    av = w[:, o:o + GQA_KV_HEADS * HEAD_DIM]; o += GQA_KV_HEADS * HEAD_DIM
    wb = w[:, o:o + POOL_WIDTH]; o += POOL_WIDTH
    wcq = w[:, o:o + MLA_Q_RANK]; o += MLA_Q_RANK
    wckv = w[:, o:o + MLA_KV_RANK]; o += MLA_KV_RANK
    wkr = w[:, o:o + MLA_ROPE_DIM]
    z64 = jnp.zeros((d, LANES - HEAD_DIM), F32)
    cols = []
    for h in range(GQA_Q_HEADS):
        cols += [aq[:, h * HEAD_DIM:(h + 1) * HEAD_DIM], z64]
    for h in range(GQA_Q_HEADS):
        g = h // GQA_GROUP
        cols += [ak[:, g * HEAD_DIM:(g + 1) * HEAD_DIM], z64]
    for h in range(GQA_Q_HEADS):
        g = h // GQA_GROUP
        cols += [av[:, g * HEAD_DIM:(g + 1) * HEAD_DIM]]
    cols += [wb, wcq, wckv]
    for h in range(MLA_HEADS):
        cols += [jnp.zeros((d, MLA_NOPE_DIM), F32), wkr,
                 jnp.zeros((d, LANES - MLA_NOPE_DIM - MLA_ROPE_DIM), F32)]
    w_wide = jnp.concatenate(cols, axis=1).astype(BF)
    assert w_wide.shape == (d, IN_WIDE)

    def head_gain(g):
        return jnp.concatenate([g, jnp.zeros((LANES - HEAD_DIM,), F32)])[None, :]

    qd = MLA_NOPE_DIM + MLA_ROPE_DIM
    uq = mla_w_uq[layer]
    ukv = mla_w_ukv[layer]
    uq_cols, uk_cols, uv_cols = [], [], []
    for h in range(MLA_HEADS):
        uq_cols += [uq[:, h * qd:(h + 1) * qd], jnp.zeros((MLA_Q_RANK, LANES - qd), F32)]
        k0 = h * (MLA_NOPE_DIM + MLA_V_DIM)
        uk_cols += [ukv[:, k0:k0 + MLA_NOPE_DIM], jnp.zeros((MLA_KV_RANK, LANES - MLA_NOPE_DIM), F32)]
        uv_cols += [ukv[:, k0 + MLA_NOPE_DIM:k0 + MLA_NOPE_DIM + MLA_V_DIM]]

    pw = pool_w[layer]
    w_bd = jnp.zeros((POOL_WIDTH, POOL_WIDTH), F32)
    for g in range(POOL_GROUPS):
        s = slice(g * POOL_GROUP_DIM, (g + 1) * POOL_GROUP_DIM)
        w_bd = w_bd.at[s, s].set(pw[g])

    wo = w_out[layer]
    na = GQA_Q_HEADS * HEAD_DIM
    w_o1 = jnp.concatenate([wo[0:na], wo[na + POOL_WIDTH:]], axis=0).astype(BF)
    w_o2 = wo[na:na + POOL_WIDTH].astype(BF)

    return {
        "n1_g": norm1_g[layer][None, :],
        "n2_g": norm2_g[layer][None, :],
        "w_in": w_wide,
        "gqa_qg": head_gain(gqa_qn_g[layer]),
        "gqa_kg": head_gain(gqa_kn_g[layer]),
        "mla_qg": mla_qn_g[layer][None, :],
        "mla_kvg": mla_kvn_g[layer][None, :],
        "w_uq": jnp.concatenate(uq_cols, axis=1).astype(BF),
        "w_uk": jnp.concatenate(uk_cols, axis=1).astype(BF),
        "w_uv": jnp.concatenate(uv_cols, axis=1).astype(BF),
        "pool_w": w_bd.astype(BF),
        "pool_scale": pool_scale[layer][None, :],
        "w_o1": w_o1,
        "w_o2": w_o2,
        "peer_wq": peer_wq[layer].reshape(d, PEER_HEADS, PEER_QUERY_DIM).transpose(1, 0, 2).astype(BF),
        "peer_sk": peer_subkeys[layer].astype(BF),
        "peer_u": peer_u[layer].astype(BF),
        "peer_vt": peer_v[layer].T.astype(BF),
    }


def _tile(l, pref):
    return pref if l % pref == 0 else l


def kernel(x, c, ctx, c_ctx, ada_w, ada_b, norm1_g, norm2_g, w_in, gqa_qn_g, gqa_kn_g, pool_w, pool_scale,
           mla_qn_g, mla_kvn_g, mla_w_uq, mla_w_ukv, w_out, peer_wq, peer_subkeys, peer_u, peer_v, final_g):
    b, l, d = x.shape
    lc = ctx.shape[1]
    depth = ada_w.shape[0]
    assert d == D_MODEL and l % GRID_W == 0
    assert (b * l) % PEER_TOK == 0 and (b * lc) % PEER_TOK == 0

    n_rows = -(-(b + 1) // 8) * 8
    cc = jnp.concatenate([c, c_ctx[None, :], jnp.zeros((n_rows - b - 1, d), F32)], axis=0)
    mod_all = _adaln(cc, ada_w, ada_b).reshape(depth, n_rows, N_MOD, d)

    rope_tabs = _rope_tables(l)
    tl_l, tl_c = _tile(l, 512), _tile(lc, 512)
    tq_l, tq_c = _tile(l, 256), _tile(lc, 256)
    fg = final_g[None, :]

    xl, xc = x, ctx
    for layer in range(depth):
        lw = _layer_weights(layer, norm1_g, norm2_g, w_in, gqa_qn_g, gqa_kn_g, pool_w, pool_scale, mla_qn_g,
                            mla_kvn_g, mla_w_uq, mla_w_ukv, w_out, peer_wq, peer_subkeys, peer_u, peer_v)
        mod = mod_all[layer]
        last = layer == depth - 1
        q_l, k_l, v_l, pb_l = _inproj(xl, mod, None, lw, rope_tabs, tl_l)
        q_c, k_c, v_c, pb_c = _inproj(xc, mod, b, lw, None, tl_c)

        att_l = _attn(q_l, [k_c, k_l], [v_c, v_l], tq_l)
        ob_l = _pool(pb_l, lw["pool_w"], lw["pool_scale"])
        xl_mid, hn_l = _outproj(xl, att_l, ob_l, mod, None, lw, tl_l)
        peer_l = _peer(hn_l.reshape(b * l, d), lw).reshape(b, l, d)
        xl = _resid(xl_mid, peer_l, mod, None, fg, last, tl_l)

        if not last:
            att_c = _attn(q_c, [k_c], [v_c], tq_c)
            ob_c = _pool(pb_c, lw["pool_w"], lw["pool_scale"])
            xc_mid, hn_c = _outproj(xc, att_c, ob_c, mod, b, lw, tl_c)
            peer_c = _peer(hn_c.reshape(b * lc, d), lw).reshape(b, lc, d)
            xc = _resid(xc_mid, peer_c, mod, b, fg, False, tl_c)
    return xl
```

```python
import functools

import jax
import jax.numpy as jnp
from jax import lax
from jax.experimental import pallas as pl
from jax.experimental.pallas import tpu as pltpu

D_MODEL = 1024
GRID_W = 64
N_MOD = 6
EPS = 1e-6
ROPE_BASE = 10000.0
HEAD_DIM = 64
GQA_Q_HEADS = 6
GQA_KV_HEADS = 2
GQA_GROUP = GQA_Q_HEADS // GQA_KV_HEADS
POOL_GROUPS = 4
POOL_WINDOWS = (2, 4, 8, 16)
POOL_WIDTH = D_MODEL // 4
POOL_GROUP_DIM = POOL_WIDTH // POOL_GROUPS
MLA_HEADS = 6
MLA_NOPE_DIM = 64
MLA_ROPE_DIM = 32
MLA_V_DIM = 64
MLA_Q_RANK = 384
MLA_KV_RANK = 256
PEER_HEADS = 8
PEER_N_KEYS = 128
PEER_N_EXPERTS = PEER_N_KEYS * PEER_N_KEYS
PEER_TOPK = 16
PEER_QUERY_DIM = 256
PEER_HALF = PEER_QUERY_DIM // 2

LANES = 128
N_ATT_HEADS = GQA_Q_HEADS + MLA_HEADS
QK_WIDTH = N_ATT_HEADS * LANES
V_WIDTH = N_ATT_HEADS * HEAD_DIM
POOL_PAD = 16

_C_QA = 0
_C_KA = _C_QA + GQA_Q_HEADS * LANES
_C_VA = _C_KA + GQA_Q_HEADS * LANES
_C_B = _C_VA + GQA_Q_HEADS * HEAD_DIM
_C_CQ = _C_B + POOL_WIDTH
_C_CKV = _C_CQ + MLA_Q_RANK
_C_KR = _C_CKV + MLA_KV_RANK
IN_WIDE = _C_KR + MLA_HEADS * LANES

PEER_TOK = 512
PEER_EBLK = 1024
PEER_I1_PER_BLK = PEER_EBLK // PEER_N_KEYS
VMEM_LIMIT = 56 * 1024 * 1024

BF = jnp.bfloat16
F32 = jnp.float32
NEG_INF = float("-inf")
POS_INF = float("inf")


def _cparams(sem):
    return pltpu.CompilerParams(dimension_semantics=sem, vmem_limit_bytes=VMEM_LIMIT)


def _dot(a, b):
    return jnp.dot(a, b, preferred_element_type=F32)


def _dot_nt(a, b):
    return lax.dot_general(a, b, (((1,), (1,)), ((), ())), preferred_element_type=F32)


def _split_bf16(a):
    hi = a.astype(BF)
    lo = (a - hi.astype(F32)).astype(BF)
    return hi, lo


def _dot3(a, b):
    ah, al = _split_bf16(a)
    bh, bl = _split_bf16(b)
    return _dot(ah, bh) + _dot(ah, bl) + _dot(al, bh)


def _rms(x):
    return x * lax.rsqrt(jnp.mean(x * x, axis=-1, keepdims=True) + EPS)


def _adaln_kernel(c_ref, w_ref, b_ref, o_ref):
    c = c_ref[...]
    s = c * (1.0 / (1.0 + jnp.exp(-c)))
    o_ref[...] = _dot3(s, w_ref[...]) + b_ref[...]


def _adaln(cc, ada_w, ada_b):
    depth, d, nd = ada_w.shape
    r = cc.shape[0]
    nblk = nd // d
    return pl.pallas_call(
        _adaln_kernel,
        out_shape=jax.ShapeDtypeStruct((depth, r, nd), F32),
        grid=(depth, nblk),
        in_specs=[
            pl.BlockSpec((r, d), lambda l, j: (0, 0)),
            pl.BlockSpec((None, d, d), lambda l, j: (l, 0, j)),
            pl.BlockSpec((None, 1, d), lambda l, j: (l, 0, j)),
        ],
        out_specs=pl.BlockSpec((None, r, d), lambda l, j: (l, 0, j)),
        compiler_params=_cparams(("arbitrary", "arbitrary")),
        name="adaln",
    )(cc, ada_w, ada_b.reshape(depth, 1, nd))


def _rope_block(xb, cb, sb, half):
    lane = lax.broadcasted_iota(jnp.int32, xb.shape, 1)
    fwd = pltpu.roll(xb, LANES - half, 1)
    bwd = pltpu.roll(xb, half, 1)
    sw = jnp.where((lane & half) == 0, fwd, bwd)
    return xb * cb + sw * sb


def _inproj_kernel(*refs, rope):
    if rope:
        (x_ref, mod_ref, n1_ref, w_ref, qg_ref, kg_ref, cqg_ref, ckvg_ref, wuq_ref, wuk_ref, wuv_ref,
         ca_ref, sa_ref, cc_ref, sc_ref, q_ref, k_ref, v_ref, b_ref) = refs
    else:
        (x_ref, mod_ref, n1_ref, w_ref, qg_ref, kg_ref, cqg_ref, ckvg_ref, wuq_ref, wuk_ref, wuv_ref,
         q_ref, k_ref, v_ref, b_ref) = refs
    x = x_ref[...]
    shift = mod_ref[0:1, :]
    scale = mod_ref[1:2, :]
    h = _rms(x) * n1_ref[...] * (1.0 + scale) + shift
    p = _dot(h.astype(BF), w_ref[...])

    sa = HEAD_DIM ** -0.5
    sc = (MLA_NOPE_DIM + MLA_ROPE_DIM) ** -0.5

    for hh in range(GQA_Q_HEADS):
        for (c0, g_ref, o_ref, mul) in ((_C_QA, qg_ref, q_ref, sa), (_C_KA, kg_ref, k_ref, 1.0)):
            blk = p[:, c0 + hh * LANES:c0 + (hh + 1) * LANES]
            ms = jnp.sum(blk * blk, axis=-1, keepdims=True) * (1.0 / HEAD_DIM)
            y = blk * lax.rsqrt(ms + EPS) * g_ref[...]
            if rope:
                y = _rope_block(y, ca_ref[...], sa_ref[...], HEAD_DIM // 2)
            if mul != 1.0:
                y = y * mul
            o_ref[:, hh * LANES:(hh + 1) * LANES] = y.astype(BF)
    v_ref[:, 0:GQA_Q_HEADS * HEAD_DIM] = p[:, _C_VA:_C_VA + GQA_Q_HEADS * HEAD_DIM].astype(BF)
    b_ref[...] = p[:, _C_B:_C_B + POOL_WIDTH]

    cq = _rms(p[:, _C_CQ:_C_CQ + MLA_Q_RANK]) * cqg_ref[...]
    ckv = (_rms(p[:, _C_CKV:_C_CKV + MLA_KV_RANK]) * ckvg_ref[...]).astype(BF)
    qc = _dot(cq.astype(BF), wuq_ref[...])
    kc = _dot(ckv, wuk_ref[...]) + p[:, _C_KR:_C_KR + MLA_HEADS * LANES]
    vc = _dot(ckv, wuv_ref[...])
    base = GQA_Q_HEADS * LANES
    for hh in range(MLA_HEADS):
        qb = qc[:, hh * LANES:(hh + 1) * LANES]
        kb = kc[:, hh * LANES:(hh + 1) * LANES]
        if rope:
            qb = _rope_block(qb, cc_ref[...], sc_ref[...], MLA_ROPE_DIM // 2)
            kb = _rope_block(kb, cc_ref[...], sc_ref[...], MLA_ROPE_DIM // 2)
        q_ref[:, base + hh * LANES:base + (hh + 1) * LANES] = (qb * sc).astype(BF)
        k_ref[:, base + hh * LANES:base + (hh + 1) * LANES] = kb.astype(BF)
    v_ref[:, GQA_Q_HEADS * HEAD_DIM:V_WIDTH] = vc.astype(BF)


def _inproj(x, mod, mod_row0, lw, rope_tabs, tl):
    b, l, d = x.shape
    rope = rope_tabs is not None
    grid = (b, l // tl)

    def full(a):
        nd = a.ndim
        return pl.BlockSpec(a.shape, lambda i, j: (0,) * nd)

    if mod_row0 is None:
        mod_spec = pl.BlockSpec((None, N_MOD, d), lambda i, j: (i, 0, 0))
    else:
        mod_spec = pl.BlockSpec((None, N_MOD, d), lambda i, j: (mod_row0, 0, 0))
    weights = [lw["n1_g"], lw["w_in"], lw["gqa_qg"], lw["gqa_kg"], lw["mla_qg"], lw["mla_kvg"],
               lw["w_uq"], lw["w_uk"], lw["w_uv"]]
    in_specs = [pl.BlockSpec((None, tl, d), lambda i, j: (i, j, 0)), mod_spec] + [full(a) for a in weights]
    args = [x, mod] + weights
    if rope:
        in_specs += [pl.BlockSpec((tl, LANES), lambda i, j: (j, 0)) for _ in range(4)]
        args += list(rope_tabs)
    outs = [(QK_WIDTH, BF), (QK_WIDTH, BF), (V_WIDTH, BF), (POOL_WIDTH, F32)]
    return pl.pallas_call(
        functools.partial(_inproj_kernel, rope=rope),
        out_shape=[jax.ShapeDtypeStruct((b, l, w), dt) for w, dt in outs],
        grid=grid,
        in_specs=in_specs,
        out_specs=[pl.BlockSpec((None, tl, w), lambda i, j: (i, j, 0)) for w, _ in outs],
        compiler_params=_cparams(("parallel", "parallel")),
        name="inproj",
    )(*args)


def _attn_kernel(*refs, n_src):
    q_ref = refs[0]
    k_refs = refs[1:1 + n_src]
    v_refs = refs[1 + n_src:1 + 2 * n_src]
    o_ref = refs[1 + 2 * n_src]
    tq = q_ref.shape[0]
    lane = lax.broadcasted_iota(jnp.int32, (tq, LANES), 1)
    for jb in range(N_ATT_HEADS // 2):
        halves = []
        for n in (2 * jb, 2 * jb + 1):
            q = q_ref[:, n * LANES:(n + 1) * LANES]
            ss = [_dot_nt(q, k_ref[:, n * LANES:(n + 1) * LANES]) for k_ref in k_refs]
            m = ss[0].max(axis=-1, keepdims=True)
            for s in ss[1:]:
                m = jnp.maximum(m, s.max(axis=-1, keepdims=True))
            acc = None
            den = None
            for s, v_ref in zip(ss, v_refs):
                e = jnp.exp(s - m)
                dsum = e.sum(axis=-1, keepdims=True)
                pv = _dot(e.astype(BF), v_ref[:, jb * LANES:(jb + 1) * LANES])
                acc = pv if acc is None else acc + pv
                den = dsum if den is None else den + dsum
            halves.append(acc * (1.0 / den))
        o = jnp.where(lane < HEAD_DIM, halves[0], halves[1])
        o_ref[:, jb * LANES:(jb + 1) * LANES] = o.astype(BF)


def _attn(q, ks, vs, tq):
    b, l, _ = q.shape
    n_src = len(ks)
    in_specs = [pl.BlockSpec((None, tq, QK_WIDTH), lambda i, j: (i, j, 0))]
    in_specs += [pl.BlockSpec((None, k.shape[1], QK_WIDTH), lambda i, j: (i, 0, 0)) for k in ks]
    in_specs += [pl.BlockSpec((None, v.shape[1], V_WIDTH), lambda i, j: (i, 0, 0)) for v in vs]
    return pl.pallas_call(
        functools.partial(_attn_kernel, n_src=n_src),
        out_shape=jax.ShapeDtypeStruct((b, l, V_WIDTH), BF),
        grid=(b, l // tq),
        in_specs=in_specs,
        out_specs=pl.BlockSpec((None, tq, V_WIDTH), lambda i, j: (i, j, 0)),
        compiler_params=_cparams(("parallel", "arbitrary")),
        name="attn",
    )(q, *ks, *vs)


def _pool_kernel(b_ref, w_ref, s_ref, o_ref, xp_ref):
    l = b_ref.shape[0]
    x = b_ref[...]
    zeros = jnp.zeros((POOL_PAD, POOL_WIDTH), F32)
    xp_ref[0:POOL_PAD, :] = zeros
    xp_ref[POOL_PAD + l:POOL_PAD + l + POOL_PAD, :] = zeros
    xp_ref[POOL_PAD:POOL_PAD + l, :] = x
    t = lax.broadcasted_iota(jnp.int32, (l, LANES), 0)
    lane = lax.broadcasted_iota(jnp.int32, (l, LANES), 1)
    outs = []
    for half in range(POOL_WIDTH // LANES):
        w_small = POOL_WINDOWS[2 * half]
        w_big = POOL_WINDOWS[2 * half + 1]
        lo, hi = half * LANES, (half + 1) * LANES

        def win(j):
            return xp_ref[pl.ds(POOL_PAD + j, l), lo:hi]

        s_small = None
        for j in range(-(w_small // 2), w_small // 2):
            s_small = win(j) if s_small is None else s_small + win(j)
        s_big = s_small
        for j in range(-(w_big // 2), w_big // 2):
            if not (-(w_small // 2) <= j < w_small // 2):
                s_big = s_big + win(j)

        def cnt(w):
            lo_i = jnp.maximum(t - w // 2, 0)
            hi_i = jnp.minimum(t - w // 2 + w, l)
            return (hi_i - lo_i).astype(F32)

        left = lane < POOL_GROUP_DIM
        s = jnp.where(left, s_small, s_big)
        c = jnp.where(left, cnt(w_small), cnt(w_big))
        outs.append(s / c - x[:, lo:hi])
    pooled = jnp.concatenate(outs, axis=1).astype(BF)
    o_ref[...] = (_dot(pooled, w_ref[...]) * s_ref[...]).astype(BF)


def _pool(pb, w_bd, pscale):
    b, l, w = pb.shape
    return pl.pallas_call(
        _pool_kernel,
        out_shape=jax.ShapeDtypeStruct((b, l, w), BF),
        grid=(b,),
        in_specs=[
            pl.BlockSpec((None, l, w), lambda i: (i, 0, 0)),
            pl.BlockSpec((w, w), lambda i: (0, 0)),
            pl.BlockSpec((1, w), lambda i: (0, 0)),
        ],
        out_specs=pl.BlockSpec((None, l, w), lambda i: (i, 0, 0)),
        scratch_shapes=[pltpu.VMEM((l + 2 * POOL_PAD, w), F32)],
        compiler_params=_cparams(("parallel",)),
        name="pool",
    )(pb, w_bd, pscale)


def _outproj_kernel(x_ref, att_ref, ob_ref, mod_ref, n2_ref, w1_ref, w2_ref, xo_ref, h_ref):
    y = _dot(att_ref[...], w1_ref[...]) + _dot(ob_ref[...], w2_ref[...])
    xn = x_ref[...] + mod_ref[2:3, :] * y
    xo_ref[...] = xn
    h = _rms(xn) * n2_ref[...] * (1.0 + mod_ref[4:5, :]) + mod_ref[3:4, :]
    h_ref[...] = h.astype(BF)


def _outproj(x, att, ob, mod, mod_row0, lw, tl):
    b, l, d = x.shape
    if mod_row0 is None:
        mod_spec = pl.BlockSpec((None, N_MOD, d), lambda i, j: (i, 0, 0))
    else:
        mod_spec = pl.BlockSpec((None, N_MOD, d), lambda i, j: (mod_row0, 0, 0))

    def tile(w):
        return pl.BlockSpec((None, tl, w), lambda i, j: (i, j, 0))

    def full(a):
        return pl.BlockSpec(a.shape, lambda i, j: (0, 0))

    return pl.pallas_call(
        _outproj_kernel,
        out_shape=[jax.ShapeDtypeStruct((b, l, d), F32), jax.ShapeDtypeStruct((b, l, d), BF)],
        grid=(b, l // tl),
        in_specs=[tile(d), tile(V_WIDTH), tile(POOL_WIDTH), mod_spec, full(lw["n2_g"]), full(lw["w_o1"]),
                  full(lw["w_o2"])],
        out_specs=[tile(d), tile(d)],
        compiler_params=_cparams(("parallel", "parallel")),
        name="outproj",
    )(x, att, ob, mod, lw["n2_g"], lw["w_o1"], lw["w_o2"])


def _gelu_tanh(x):
    return 0.5 * x * (1.0 + jnp.tanh(0.7978845608028654 * (x + 0.044715 * (x * x * x))))


def _peer_prologue(hn_ref, wq_ref, sk_ref, c1_ref, e1_ref, r2_ref, e2_ref, s_ref, v_ref):
    t = hn_ref.shape[0]
    hn = hn_ref[...]

    def head(h, carry):
        qh = _dot(hn, wq_ref[h]).astype(BF)
        for p in range(2):
            st = _dot_nt(sk_ref[h, p], qh[:, p * PEER_HALF:(p + 1) * PEER_HALF])
            s_ref[p] = st
            m = st.max(axis=0, keepdims=True)
            v_ref[p, 0:1, :] = m
            for k in range(1, PEER_TOPK):
                m = jnp.where(st < m, st, NEG_INF).max(axis=0, keepdims=True)
                v_ref[p, k:k + 1, :] = m
        v1 = v_ref[0]
        v2 = v_ref[1]

        def cand(a):
            return v1[a:a + 1, :] + v2

        def cand_max_below(m):
            best = None
            for a in range(PEER_TOPK):
                c = cand(a)
                if m is not None:
                    c = jnp.where(c < m, c, NEG_INF)
                c = c.max(axis=0, keepdims=True)
                best = c if best is None else jnp.maximum(best, c)
            return best

        c00 = cand_max_below(None)
        tau = lax.fori_loop(1, PEER_TOPK, lambda k, m: cand_max_below(m), c00)
        z = jnp.zeros((1, t), F32)
        s1 = s_ref[0]
        s2 = s_ref[1]
        c1 = jnp.zeros((PEER_N_KEYS, t), F32)
        r2 = jnp.zeros((PEER_N_KEYS, t), F32)
        for a in range(PEER_TOPK):
            c = cand(a)
            sel = c >= tau
            z = z + jnp.where(sel, jnp.exp(c - c00), 0.0).sum(axis=0, keepdims=True)
            cnt = jnp.where(sel, 1.0, 0.0).sum(axis=0, keepdims=True)
            c1 = jnp.where(s1 == v1[a:a + 1, :], cnt, c1)
            r2 = r2 + jnp.where(s2 < v2[a:a + 1, :], 1.0, 0.0)
        c1_ref[h] = c1
        r2_ref[h] = r2.astype(BF)
        e1_ref[h] = jnp.exp(s1 - v1[0:1, :]) * (1.0 / z)
        e2_ref[h] = jnp.exp(s2 - v2[0:1, :]).astype(BF)
        return carry

    lax.fori_loop(0, PEER_HEADS, head, 0)


def _bcast_rows_bf16(row):
    tile = jnp.broadcast_to(row, (16, LANES)).astype(BF)
    return jnp.concatenate([tile] * (PEER_N_KEYS // 16), axis=0)


def _peer_dense(at_ref, wt_ref, blk, c1_ref, e1_ref, r2_ref, e2_ref):
    t = at_ref.shape[1]
    i1_0 = pl.multiple_of(blk * PEER_I1_PER_BLK, PEER_I1_PER_BLK)
    for lg in range(t // LANES):
        ls = slice(lg * LANES, (lg + 1) * LANES)
        c1_s = [c1_ref[h, pl.ds(i1_0, PEER_I1_PER_BLK), ls] for h in range(PEER_HEADS)]
        e1_s = [e1_ref[h, pl.ds(i1_0, PEER_I1_PER_BLK), ls] for h in range(PEER_HEADS)]
        for j in range(PEER_I1_PER_BLK):
            rs = slice(j * PEER_N_KEYS, (j + 1) * PEER_N_KEYS)
            g = jnp.zeros((PEER_N_KEYS, LANES), BF)
            for h in range(PEER_HEADS):
                c1 = _bcast_rows_bf16(c1_s[h][j:j + 1, :])
                e1 = _bcast_rows_bf16(e1_s[h][j:j + 1, :])
                g = g + e1 * jnp.where(r2_ref[h, :, ls] < c1, e2_ref[h, :, ls], jnp.zeros((), BF))
            wt_ref[rs, ls] = _gelu_tanh(at_ref[rs, ls]).astype(BF) * g


def _peer_kernel(hn_ref, wq_ref, sk_ref, u0_ref, ua_ref, ub_ref, vta_ref, vtb_ref, o_ref,
                 acc_ref, at_a, at_b, wt_a, wt_b, c1_ref, e1_ref, r2_ref, e2_ref, s_ref, v_ref, fence_sem):
    s = pl.program_id(1)
    last = pl.num_programs(1) - 1
    tabs = (c1_ref, e1_ref, r2_ref, e2_ref)

    @pl.when(s == 0)
    def _():
        _peer_prologue(hn_ref, wq_ref, sk_ref, c1_ref, e1_ref, r2_ref, e2_ref, s_ref, v_ref)
        acc_ref[...] = jnp.zeros_like(acc_ref)
        wt_b[...] = jnp.zeros_like(wt_b)
        at_a[...] = _dot_nt(u0_ref[...], hn_ref[...])

    @pl.when(s < last)
    def _():
        at_b[...] = _dot_nt(ua_ref[...], hn_ref[...])
        _peer_dense(at_a, wt_a, 2 * s, *tabs)
        acc_ref[...] += _dot(vta_ref[...], wt_b[...])
        pl.semaphore_signal(fence_sem, 1)
        pl.semaphore_wait(fence_sem, 1)

    @pl.when(s < last)
    def _():
        at_a[...] = _dot_nt(ub_ref[...], hn_ref[...])
        _peer_dense(at_b, wt_b, 2 * s + 1, *tabs)
        acc_ref[...] += _dot(vtb_ref[...], wt_a[...])

    @pl.when(s == last)
    def _():
        o_ref[...] = (acc_ref[...] + _dot(vta_ref[...], wt_b[...])).T


def _peer(hn, lw):
    ntok, d = hn.shape
    t = PEER_TOK
    n_blk = PEER_N_EXPERTS // PEER_EBLK
    n_steps = n_blk // 2 + 1
    hi = n_blk - 1
    tab_f = pltpu.VMEM((PEER_HEADS, PEER_N_KEYS, t), F32)
    tab_b = pltpu.VMEM((PEER_HEADS, PEER_N_KEYS, t), BF)

    def u_spec(f):
        return pl.BlockSpec((PEER_EBLK, d), lambda i, s: (f(s), 0))

    def vt_spec(f):
        return pl.BlockSpec((d, PEER_EBLK), lambda i, s: (0, f(s)))

    return pl.pallas_call(
        _peer_kernel,
        out_shape=jax.ShapeDtypeStruct((ntok, d), F32),
        grid=(ntok // t, n_steps),
        in_specs=[
            pl.BlockSpec((t, d), lambda i, s: (i, 0)),
            pl.BlockSpec(lw["peer_wq"].shape, lambda i, s: (0, 0, 0)),
            pl.BlockSpec(lw["peer_sk"].shape, lambda i, s: (0, 0, 0, 0)),
            u_spec(lambda s: 0),
            u_spec(lambda s: jnp.minimum(2 * s + 1, hi)),
            u_spec(lambda s: jnp.minimum(2 * s + 2, hi)),
            vt_spec(lambda s: jnp.maximum(2 * s - 1, 0)),
            vt_spec(lambda s: jnp.minimum(2 * s, hi)),
        ],
        out_specs=pl.BlockSpec((t, d), lambda i, s: (i, 0)),
        scratch_shapes=[
            pltpu.VMEM((d, t), F32),
            pltpu.VMEM((PEER_EBLK, t), F32),
            pltpu.VMEM((PEER_EBLK, t), F32),
            pltpu.VMEM((PEER_EBLK, t), BF),
            pltpu.VMEM((PEER_EBLK, t), BF),
            tab_f, tab_f, tab_b, tab_b,
            pltpu.VMEM((2, PEER_N_KEYS, t), F32),
            pltpu.VMEM((2, PEER_TOPK, t), F32),
            pltpu.SemaphoreType.REGULAR,
        ],
        compiler_params=_cparams(("parallel", "arbitrary")),
        name="peer",
    )(hn, lw["peer_wq"], lw["peer_sk"], lw["peer_u"], lw["peer_u"], lw["peer_u"], lw["peer_vt"], lw["peer_vt"])


def _resid_kernel(x_ref, p_ref, mod_ref, g_ref, o_ref, *, final):
    y = x_ref[...] + mod_ref[5:6, :] * p_ref[...]
    if final:
        y = _rms(y) * g_ref[...]
    o_ref[...] = y


def _resid(x, peer_out, mod, mod_row0, final_g, final, tl):
    b, l, d = x.shape
    if mod_row0 is None:
        mod_spec = pl.BlockSpec((None, N_MOD, d), lambda i, j: (i, 0, 0))
    else:
        mod_spec = pl.BlockSpec((None, N_MOD, d), lambda i, j: (mod_row0, 0, 0))
    tile = pl.BlockSpec((None, tl, d), lambda i, j: (i, j, 0))
    return pl.pallas_call(
        functools.partial(_resid_kernel, final=final),
        out_shape=jax.ShapeDtypeStruct((b, l, d), F32),
        grid=(b, l // tl),
        in_specs=[tile, tile, mod_spec, pl.BlockSpec((1, d), lambda i, j: (0, 0))],
        out_specs=tile,
        compiler_params=_cparams(("parallel", "parallel")),
        name="resid",
    )(x, peer_out, mod, final_g)


def _rope_tables(l):
    rows = l // GRID_W
    row = jnp.repeat(jnp.arange(rows), GRID_W).astype(F32)
    col = jnp.tile(jnp.arange(GRID_W), rows).astype(F32)

    def cs(rot_dim):
        n = rot_dim // 4
        inv = ROPE_BASE ** (-jnp.arange(n, dtype=F32) / n)
        ang = jnp.concatenate([row[:, None] * inv, col[:, None] * inv], axis=-1)
        return jnp.cos(ang), jnp.sin(ang)

    ones = lambda w: jnp.ones((l, w), F32)
    zeros = lambda w: jnp.zeros((l, w), F32)
    ca, sa = cs(HEAD_DIM)
    cos_a = jnp.concatenate([ca, ca, ones(LANES - HEAD_DIM)], axis=1)
    sin_a = jnp.concatenate([-sa, sa, zeros(LANES - HEAD_DIM)], axis=1)
    cc, sc = cs(MLA_ROPE_DIM)
    tail = LANES - MLA_NOPE_DIM - MLA_ROPE_DIM
    cos_c = jnp.concatenate([ones(MLA_NOPE_DIM), cc, cc, ones(tail)], axis=1)
    sin_c = jnp.concatenate([zeros(MLA_NOPE_DIM), -sc, sc, zeros(tail)], axis=1)
    return cos_a, sin_a, cos_c, sin_c


def _layer_weights(layer, norm1_g, norm2_g, w_in, gqa_qn_g, gqa_kn_g, pool_w, pool_scale, mla_qn_g, mla_kvn_g,
                   mla_w_uq, mla_w_ukv, w_out, peer_wq, peer_subkeys, peer_u, peer_v):
    d = D_MODEL
    w = w_in[layer]
    o = 0
    aq = w[:, o:o + GQA_Q_HEADS * HEAD_DIM]; o += GQA_Q_HEADS * HEAD_DIM
    ak = w[:, o:o + GQA_KV_HEADS * HEAD_DIM]; o += GQA_KV_HEADS * HEAD_DIM
    av = w[:, o:o + GQA_KV_HEADS * HEAD_DIM]; o += GQA_KV_HEADS * HEAD_DIM
    wb = w[:, o:o + POOL_WIDTH]; o += POOL_WIDTH
    wcq = w[:, o:o + MLA_Q_RANK]; o += MLA_Q_RANK
    wckv = w[:, o:o + MLA_KV_RANK]; o += MLA_KV_RANK
    wkr = w[:, o:o + MLA_ROPE_DIM]
    z64 = jnp.zeros((d, LANES - HEAD_DIM), F32)
    cols = []
    for h in range(GQA_Q_HEADS):
        cols += [aq[:, h * HEAD_DIM:(h + 1) * HEAD_DIM], z64]
    for h in range(GQA_Q_HEADS):
        g = h // GQA_GROUP
        cols += [ak[:, g * HEAD_DIM:(g + 1) * HEAD_DIM], z64]
    for h in range(GQA_Q_HEADS):
        g = h // GQA_GROUP
        cols += [av[:, g * HEAD_DIM:(g + 1) * HEAD_DIM]]
    cols += [wb, wcq, wckv]
    for h in range(MLA_HEADS):
        cols += [jnp.zeros((d, MLA_NOPE_DIM), F32), wkr,
                 jnp.zeros((d, LANES - MLA_NOPE_DIM - MLA_ROPE_DIM), F32)]
    w_wide = jnp.concatenate(cols, axis=1).astype(BF)
    assert w_wide.shape == (d, IN_WIDE)

    def head_gain(g):
        return jnp.concatenate([g, jnp.zeros((LANES - HEAD_DIM,), F32)])[None, :]

    qd = MLA_NOPE_DIM + MLA_ROPE_DIM
    uq = mla_w_uq[layer]
    ukv = mla_w_ukv[layer]
    uq_cols, uk_cols, uv_cols = [], [], []
    for h in range(MLA_HEADS):
        uq_cols += [uq[:, h * qd:(h + 1) * qd], jnp.zeros((MLA_Q_RANK, LANES - qd), F32)]
        k0 = h * (MLA_NOPE_DIM + MLA_V_DIM)
        uk_cols += [ukv[:, k0:k0 + MLA_NOPE_DIM], jnp.zeros((MLA_KV_RANK, LANES - MLA_NOPE_DIM), F32)]
        uv_cols += [ukv[:, k0 + MLA_NOPE_DIM:k0 + MLA_NOPE_DIM + MLA_V_DIM]]

    pw = pool_w[layer]
    w_bd = jnp.zeros((POOL_WIDTH, POOL_WIDTH), F32)
    for g in range(POOL_GROUPS):
        s = slice(g * POOL_GROUP_DIM, (g + 1) * POOL_GROUP_DIM)
        w_bd = w_bd.at[s, s].set(pw[g])

    wo = w_out[layer]
    na = GQA_Q_HEADS * HEAD_DIM
    w_o1 = jnp.concatenate([wo[0:na], wo[na + POOL_WIDTH:]], axis=0).astype(BF)
    w_o2 = wo[na:na + POOL_WIDTH].astype(BF)

    return {
        "n1_g": norm1_g[layer][None, :],
        "n2_g": norm2_g[layer][None, :],
        "w_in": w_wide,
        "gqa_qg": head_gain(gqa_qn_g[layer]),
        "gqa_kg": head_gain(gqa_kn_g[layer]),
        "mla_qg": mla_qn_g[layer][None, :],
        "mla_kvg": mla_kvn_g[layer][None, :],
        "w_uq": jnp.concatenate(uq_cols, axis=1).astype(BF),
        "w_uk": jnp.concatenate(uk_cols, axis=1).astype(BF),
        "w_uv": jnp.concatenate(uv_cols, axis=1).astype(BF),
        "pool_w": w_bd.astype(BF),
        "pool_scale": pool_scale[layer][None, :],
        "w_o1": w_o1,
        "w_o2": w_o2,
        "peer_wq": peer_wq[layer].reshape(d, PEER_HEADS, PEER_QUERY_DIM).transpose(1, 0, 2).astype(BF),
        "peer_sk": peer_subkeys[layer].astype(BF),
        "peer_u": peer_u[layer].astype(BF),
        "peer_vt": peer_v[layer].T.astype(BF),
    }


def _tile(l, pref):
    return pref if l % pref == 0 else l


def kernel(x, c, ctx, c_ctx, ada_w, ada_b, norm1_g, norm2_g, w_in, gqa_qn_g, gqa_kn_g, pool_w, pool_scale,
           mla_qn_g, mla_kvn_g, mla_w_uq, mla_w_ukv, w_out, peer_wq, peer_subkeys, peer_u, peer_v, final_g):
    b, l, d = x.shape
    lc = ctx.shape[1]
    depth = ada_w.shape[0]
    assert d == D_MODEL and l % GRID_W == 0
    assert (b * l) % PEER_TOK == 0 and (b * lc) % PEER_TOK == 0

    n_rows = -(-(b + 1) // 8) * 8
    cc = jnp.concatenate([c, c_ctx[None, :], jnp.zeros((n_rows - b - 1, d), F32)], axis=0)
    mod_all = _adaln(cc, ada_w, ada_b).reshape(depth, n_rows, N_MOD, d)

    rope_tabs = _rope_tables(l)
    tl_l, tl_c = _tile(l, 512), _tile(lc, 512)
    tq_l, tq_c = _tile(l, 256), _tile(lc, 256)
    fg = final_g[None, :]

    xl, xc = x, ctx
    for layer in range(depth):
        lw = _layer_weights(layer, norm1_g, norm2_g, w_in, gqa_qn_g, gqa_kn_g, pool_w, pool_scale, mla_qn_g,
                            mla_kvn_g, mla_w_uq, mla_w_ukv, w_out, peer_wq, peer_subkeys, peer_u, peer_v)
        mod = mod_all[layer]
        last = layer == depth - 1
        q_l, k_l, v_l, pb_l = _inproj(xl, mod, None, lw, rope_tabs, tl_l)
        q_c, k_c, v_c, pb_c = _inproj(xc, mod, b, lw, None, tl_c)

        att_l = _attn(q_l, [k_c, k_l], [v_c, v_l], tq_l)
        ob_l = _pool(pb_l, lw["pool_w"], lw["pool_scale"])
        xl_mid, hn_l = _outproj(xl, att_l, ob_l, mod, None, lw, tl_l)
        peer_l = _peer(hn_l.reshape(b * l, d), lw).reshape(b, l, d)
        xl = _resid(xl_mid, peer_l, mod, None, fg, last, tl_l)

        if not last:
            att_c = _attn(q_c, [k_c], [v_c], tq_c)
            ob_c = _pool(pb_c, lw["pool_w"], lw["pool_scale"])
            xc_mid, hn_c = _outproj(xc, att_c, ob_c, mod, b, lw, tl_c)
            peer_c = _peer(hn_c.reshape(b * lc, d), lw).reshape(b, lc, d)
            xc = _resid(xc_mid, peer_c, mod, b, fg, False, tl_c)
    return xl
```

```python
import functools

import jax
import jax.numpy as jnp
from jax import lax
from jax.experimental import pallas as pl
from jax.experimental.pallas import tpu as pltpu

D_MODEL = 1024
GRID_W = 64
N_MOD = 6
EPS = 1e-6
ROPE_BASE = 10000.0
HEAD_DIM = 64
GQA_Q_HEADS = 6
GQA_KV_HEADS = 2
GQA_GROUP = GQA_Q_HEADS // GQA_KV_HEADS
POOL_GROUPS = 4
POOL_WINDOWS = (2, 4, 8, 16)
POOL_WIDTH = D_MODEL // 4
POOL_GROUP_DIM = POOL_WIDTH // POOL_GROUPS
MLA_HEADS = 6
MLA_NOPE_DIM = 64
MLA_ROPE_DIM = 32
MLA_V_DIM = 64
MLA_Q_RANK = 384
MLA_KV_RANK = 256
PEER_HEADS = 8
PEER_N_KEYS = 128
PEER_N_EXPERTS = PEER_N_KEYS * PEER_N_KEYS
PEER_TOPK = 16
PEER_QUERY_DIM = 256
PEER_HALF = PEER_QUERY_DIM // 2

LANES = 128
N_ATT_HEADS = GQA_Q_HEADS + MLA_HEADS
QK_WIDTH = N_ATT_HEADS * LANES
V_WIDTH = N_ATT_HEADS * HEAD_DIM
POOL_PAD = 16

_C_QA = 0
_C_KA = _C_QA + GQA_Q_HEADS * LANES
_C_VA = _C_KA + GQA_Q_HEADS * LANES
_C_B = _C_VA + GQA_Q_HEADS * HEAD_DIM
_C_CQ = _C_B + POOL_WIDTH
_C_CKV = _C_CQ + MLA_Q_RANK
_C_KR = _C_CKV + MLA_KV_RANK
IN_WIDE = _C_KR + MLA_HEADS * LANES

PEER_TOK = 512
PEER_EBLK = 2048
PEER_I1_PER_BLK = PEER_EBLK // PEER_N_KEYS
PEER_CAND_ROWS = 2 * PEER_TOPK + 5 * 8
VMEM_LIMIT = 56 * 1024 * 1024

BF = jnp.bfloat16
F32 = jnp.float32
NEG_INF = float("-inf")
POS_INF = float("inf")


def _cparams(sem):
    return pltpu.CompilerParams(dimension_semantics=sem, vmem_limit_bytes=VMEM_LIMIT)


def _dot(a, b):
    return jnp.dot(a, b, preferred_element_type=F32)


def _dot_nt(a, b):
    return lax.dot_general(a, b, (((1,), (1,)), ((), ())), preferred_element_type=F32)


def _split_bf16(a):
    hi = a.astype(BF)
    lo = (a - hi.astype(F32)).astype(BF)
    return hi, lo


def _dot3(a, b):
    ah, al = _split_bf16(a)
    bh, bl = _split_bf16(b)
    return _dot(ah, bh) + _dot(ah, bl) + _dot(al, bh)


def _rms(x):
    return x * lax.rsqrt(jnp.mean(x * x, axis=-1, keepdims=True) + EPS)


def _adaln_kernel(c_ref, w_ref, b_ref, o_ref):
    c = c_ref[...]
    s = c * (1.0 / (1.0 + jnp.exp(-c)))
    o_ref[...] = _dot3(s, w_ref[...]) + b_ref[...]


def _adaln(cc, ada_w, ada_b):
    depth, d, nd = ada_w.shape
    r = cc.shape[0]
    nblk = nd // d
    return pl.pallas_call(
        _adaln_kernel,
        out_shape=jax.ShapeDtypeStruct((depth, r, nd), F32),
        grid=(depth, nblk),
        in_specs=[
            pl.BlockSpec((r, d), lambda l, j: (0, 0)),
            pl.BlockSpec((None, d, d), lambda l, j: (l, 0, j)),
            pl.BlockSpec((None, 1, d), lambda l, j: (l, 0, j)),
        ],
        out_specs=pl.BlockSpec((None, r, d), lambda l, j: (l, 0, j)),
        compiler_params=_cparams(("arbitrary", "arbitrary")),
        name="adaln",
    )(cc, ada_w, ada_b.reshape(depth, 1, nd))


def _rope_block(xb, cb, sb, half):
    lane = lax.broadcasted_iota(jnp.int32, xb.shape, 1)
    fwd = pltpu.roll(xb, LANES - half, 1)
    bwd = pltpu.roll(xb, half, 1)
    sw = jnp.where((lane & half) == 0, fwd, bwd)
    return xb * cb + sw * sb


def _inproj_kernel(*refs, rope):
    if rope:
        (x_ref, mod_ref, n1_ref, w_ref, qg_ref, kg_ref, cqg_ref, ckvg_ref, wuq_ref, wuk_ref, wuv_ref,
         ca_ref, sa_ref, cc_ref, sc_ref, q_ref, k_ref, v_ref, b_ref) = refs
    else:
        (x_ref, mod_ref, n1_ref, w_ref, qg_ref, kg_ref, cqg_ref, ckvg_ref, wuq_ref, wuk_ref, wuv_ref,
         q_ref, k_ref, v_ref, b_ref) = refs
    x = x_ref[...]
    shift = mod_ref[0:1, :]
    scale = mod_ref[1:2, :]
    h = _rms(x) * n1_ref[...] * (1.0 + scale) + shift
    p = _dot(h.astype(BF), w_ref[...])

    sa = HEAD_DIM ** -0.5
    sc = (MLA_NOPE_DIM + MLA_ROPE_DIM) ** -0.5

    for hh in range(GQA_Q_HEADS):
        for (c0, g_ref, o_ref, mul) in ((_C_QA, qg_ref, q_ref, sa), (_C_KA, kg_ref, k_ref, 1.0)):
            blk = p[:, c0 + hh * LANES:c0 + (hh + 1) * LANES]
            ms = jnp.sum(blk * blk, axis=-1, keepdims=True) * (1.0 / HEAD_DIM)
            y = blk * lax.rsqrt(ms + EPS) * g_ref[...]
            if rope:
                y = _rope_block(y, ca_ref[...], sa_ref[...], HEAD_DIM // 2)
            if mul != 1.0:
                y = y * mul
            o_ref[:, hh * LANES:(hh + 1) * LANES] = y.astype(BF)
    v_ref[:, 0:GQA_Q_HEADS * HEAD_DIM] = p[:, _C_VA:_C_VA + GQA_Q_HEADS * HEAD_DIM].astype(BF)
    b_ref[...] = p[:, _C_B:_C_B + POOL_WIDTH]

    cq = _rms(p[:, _C_CQ:_C_CQ + MLA_Q_RANK]) * cqg_ref[...]
    ckv = (_rms(p[:, _C_CKV:_C_CKV + MLA_KV_RANK]) * ckvg_ref[...]).astype(BF)
    qc = _dot(cq.astype(BF), wuq_ref[...])
    kc = _dot(ckv, wuk_ref[...]) + p[:, _C_KR:_C_KR + MLA_HEADS * LANES]
    vc = _dot(ckv, wuv_ref[...])
    base = GQA_Q_HEADS * LANES
    for hh in range(MLA_HEADS):
        qb = qc[:, hh * LANES:(hh + 1) * LANES]
        kb = kc[:, hh * LANES:(hh + 1) * LANES]
        if rope:
            qb = _rope_block(qb, cc_ref[...], sc_ref[...], MLA_ROPE_DIM // 2)
            kb = _rope_block(kb, cc_ref[...], sc_ref[...], MLA_ROPE_DIM // 2)
        q_ref[:, base + hh * LANES:base + (hh + 1) * LANES] = (qb * sc).astype(BF)
        k_ref[:, base + hh * LANES:base + (hh + 1) * LANES] = kb.astype(BF)
    v_ref[:, GQA_Q_HEADS * HEAD_DIM:V_WIDTH] = vc.astype(BF)


def _inproj(x, mod, mod_row0, lw, rope_tabs, tl):
    b, l, d = x.shape
    rope = rope_tabs is not None
    grid = (b, l // tl)

    def full(a):
        nd = a.ndim
        return pl.BlockSpec(a.shape, lambda i, j: (0,) * nd)

    if mod_row0 is None:
        mod_spec = pl.BlockSpec((None, N_MOD, d), lambda i, j: (i, 0, 0))
    else:
        mod_spec = pl.BlockSpec((None, N_MOD, d), lambda i, j: (mod_row0, 0, 0))
    weights = [lw["n1_g"], lw["w_in"], lw["gqa_qg"], lw["gqa_kg"], lw["mla_qg"], lw["mla_kvg"],
               lw["w_uq"], lw["w_uk"], lw["w_uv"]]
    in_specs = [pl.BlockSpec((None, tl, d), lambda i, j: (i, j, 0)), mod_spec] + [full(a) for a in weights]
    args = [x, mod] + weights
    if rope:
        in_specs += [pl.BlockSpec((tl, LANES), lambda i, j: (j, 0)) for _ in range(4)]
        args += list(rope_tabs)
    outs = [(QK_WIDTH, BF), (QK_WIDTH, BF), (V_WIDTH, BF), (POOL_WIDTH, F32)]
    return pl.pallas_call(
        functools.partial(_inproj_kernel, rope=rope),
        out_shape=[jax.ShapeDtypeStruct((b, l, w), dt) for w, dt in outs],
        grid=grid,
        in_specs=in_specs,
        out_specs=[pl.BlockSpec((None, tl, w), lambda i, j: (i, j, 0)) for w, _ in outs],
        compiler_params=_cparams(("parallel", "parallel")),
        name="inproj",
    )(*args)


def _attn_kernel(*refs, n_src):
    q_ref = refs[0]
    k_refs = refs[1:1 + n_src]
    v_refs = refs[1 + n_src:1 + 2 * n_src]
    o_ref = refs[1 + 2 * n_src]
    tq = q_ref.shape[0]
    lane = lax.broadcasted_iota(jnp.int32, (tq, LANES), 1)
    for jb in range(N_ATT_HEADS // 2):
        halves = []
        for n in (2 * jb, 2 * jb + 1):
            q = q_ref[:, n * LANES:(n + 1) * LANES]
            ss = [_dot_nt(q, k_ref[:, n * LANES:(n + 1) * LANES]) for k_ref in k_refs]
            m = ss[0].max(axis=-1, keepdims=True)
            for s in ss[1:]:
                m = jnp.maximum(m, s.max(axis=-1, keepdims=True))
            acc = None
            den = None
            for s, v_ref in zip(ss, v_refs):
                e = jnp.exp(s - m)
                dsum = e.sum(axis=-1, keepdims=True)
                pv = _dot(e.astype(BF), v_ref[:, jb * LANES:(jb + 1) * LANES])
                acc = pv if acc is None else acc + pv
                den = dsum if den is None else den + dsum
            halves.append(acc * (1.0 / den))
        o = jnp.where(lane < HEAD_DIM, halves[0], halves[1])
        o_ref[:, jb * LANES:(jb + 1) * LANES] = o.astype(BF)


def _attn(q, ks, vs, tq):
    b, l, _ = q.shape
    n_src = len(ks)
    in_specs = [pl.BlockSpec((None, tq, QK_WIDTH), lambda i, j: (i, j, 0))]
    in_specs += [pl.BlockSpec((None, k.shape[1], QK_WIDTH), lambda i, j: (i, 0, 0)) for k in ks]
    in_specs += [pl.BlockSpec((None, v.shape[1], V_WIDTH), lambda i, j: (i, 0, 0)) for v in vs]
    return pl.pallas_call(
        functools.partial(_attn_kernel, n_src=n_src),
        out_shape=jax.ShapeDtypeStruct((b, l, V_WIDTH), BF),
        grid=(b, l // tq),
        in_specs=in_specs,
        out_specs=pl.BlockSpec((None, tq, V_WIDTH), lambda i, j: (i, j, 0)),
        compiler_params=_cparams(("parallel", "arbitrary")),
        name="attn",
    )(q, *ks, *vs)


def _pool_kernel(b_ref, w_ref, s_ref, o_ref, xp_ref):
    l = b_ref.shape[0]
    x = b_ref[...]
    zeros = jnp.zeros((POOL_PAD, POOL_WIDTH), F32)
    xp_ref[0:POOL_PAD, :] = zeros
    xp_ref[POOL_PAD + l:POOL_PAD + l + POOL_PAD, :] = zeros
    xp_ref[POOL_PAD:POOL_PAD + l, :] = x
    t = lax.broadcasted_iota(jnp.int32, (l, LANES), 0)
    lane = lax.broadcasted_iota(jnp.int32, (l, LANES), 1)
    outs = []
    for half in range(POOL_WIDTH // LANES):
        w_small = POOL_WINDOWS[2 * half]
        w_big = POOL_WINDOWS[2 * half + 1]
        lo, hi = half * LANES, (half + 1) * LANES

        def win(j):
            return xp_ref[pl.ds(POOL_PAD + j, l), lo:hi]

        s_small = None
        for j in range(-(w_small // 2), w_small // 2):
            s_small = win(j) if s_small is None else s_small + win(j)
        s_big = s_small
        for j in range(-(w_big // 2), w_big // 2):
            if not (-(w_small // 2) <= j < w_small // 2):
                s_big = s_big + win(j)

        def cnt(w):
            lo_i = jnp.maximum(t - w // 2, 0)
            hi_i = jnp.minimum(t - w // 2 + w, l)
            return (hi_i - lo_i).astype(F32)

        left = lane < POOL_GROUP_DIM
        s = jnp.where(left, s_small, s_big)
        c = jnp.where(left, cnt(w_small), cnt(w_big))
        outs.append(s / c - x[:, lo:hi])
    pooled = jnp.concatenate(outs, axis=1).astype(BF)
    o_ref[...] = (_dot(pooled, w_ref[...]) * s_ref[...]).astype(BF)


def _pool(pb, w_bd, pscale):
    b, l, w = pb.shape
    return pl.pallas_call(
        _pool_kernel,
        out_shape=jax.ShapeDtypeStruct((b, l, w), BF),
        grid=(b,),
        in_specs=[
            pl.BlockSpec((None, l, w), lambda i: (i, 0, 0)),
            pl.BlockSpec((w, w), lambda i: (0, 0)),
            pl.BlockSpec((1, w), lambda i: (0, 0)),
        ],
        out_specs=pl.BlockSpec((None, l, w), lambda i: (i, 0, 0)),
        scratch_shapes=[pltpu.VMEM((l + 2 * POOL_PAD, w), F32)],
        compiler_params=_cparams(("parallel",)),
        name="pool",
    )(pb, w_bd, pscale)


def _outproj_kernel(x_ref, att_ref, ob_ref, mod_ref, n2_ref, w1_ref, w2_ref, xo_ref, h_ref):
    y = _dot(att_ref[...], w1_ref[...]) + _dot(ob_ref[...], w2_ref[...])
    xn = x_ref[...] + mod_ref[2:3, :] * y
    xo_ref[...] = xn
    h = _rms(xn) * n2_ref[...] * (1.0 + mod_ref[4:5, :]) + mod_ref[3:4, :]
    h_ref[...] = h.astype(BF)


def _outproj(x, att, ob, mod, mod_row0, lw, tl):
    b, l, d = x.shape
    if mod_row0 is None:
        mod_spec = pl.BlockSpec((None, N_MOD, d), lambda i, j: (i, 0, 0))
    else:
        mod_spec = pl.BlockSpec((None, N_MOD, d), lambda i, j: (mod_row0, 0, 0))

    def tile(w):
        return pl.BlockSpec((None, tl, w), lambda i, j: (i, j, 0))

    def full(a):
        return pl.BlockSpec(a.shape, lambda i, j: (0, 0))

    return pl.pallas_call(
        _outproj_kernel,
        out_shape=[jax.ShapeDtypeStruct((b, l, d), F32), jax.ShapeDtypeStruct((b, l, d), BF)],
        grid=(b, l // tl),
        in_specs=[tile(d), tile(V_WIDTH), tile(POOL_WIDTH), mod_spec, full(lw["n2_g"]), full(lw["w_o1"]),
                  full(lw["w_o2"])],
        out_specs=[tile(d), tile(d)],
        compiler_params=_cparams(("parallel", "parallel")),
        name="outproj",
    )(x, att, ob, mod, lw["n2_g"], lw["w_o1"], lw["w_o2"])


def _gelu_tanh(x):
    return 0.5 * x * (1.0 + jnp.tanh(0.7978845608028654 * (x + 0.044715 * (x * x * x))))


def _peer_prologue(hn_ref, wq_ref, sk_ref, c1_ref, e1_ref, r2_ref, e2_ref, s_ref, v_ref, cand_ref):
    t = hn_ref.shape[0]
    hn = hn_ref[...]

    sub8 = lax.broadcasted_iota(jnp.int32, (8, t), 0)
    sub16 = lax.broadcasted_iota(jnp.int32, (PEER_TOPK, t), 0)
    zeros8 = jnp.zeros((8, t), F32)

    def head(h, carry):
        qh = _dot(hn, wq_ref[h]).astype(BF)
        r2 = None
        for p in range(2):
            st = _dot_nt(sk_ref[h, p], qh[:, p * PEER_HALF:(p + 1) * PEER_HALF])
            s_ref[p] = st
            m = st.max(axis=0, keepdims=True)
            v_ref[p, 0:1, :] = m
            rank = jnp.zeros((PEER_N_KEYS, t), F32)
            for k in range(1, PEER_TOPK):
                below = st < m
                if p == 1:
                    rank = rank + jnp.where(below, 1.0, 0.0)
                m = jnp.where(below, st, NEG_INF).max(axis=0, keepdims=True)
                v_ref[p, k:k + 1, :] = m
            if p == 1:
                r2 = rank + jnp.where(st < m, 1.0, 0.0)
        v1 = v_ref[0]
        v2 = v_ref[1]

        groups = [
            v1[0:1, :] + v2,
            jnp.where(sub16 >= 1, v1 + v2[0:1, :], NEG_INF),
            jnp.where(sub8 >= 1, v1[1:2, :] + v2[0:8, :], NEG_INF),
            jnp.where(sub8 >= 2, v1[0:8, :] + v2[1:2, :], NEG_INF),
            jnp.where((sub8 >= 2) & (sub8 <= 4), v1[2:3, :] + v2[0:8, :], NEG_INF),
            jnp.where((sub8 >= 3) & (sub8 <= 4), v1[0:8, :] + v2[2:3, :], NEG_INF),
            jnp.where(sub8 == 3, v1[3:4, :] + v2[0:8, :], NEG_INF),
        ]
        cand_ref[...] = jnp.concatenate(groups, axis=0)
        c00 = v1[0:1, :] + v2[0:1, :]

        def below_max(k, m):
            c = cand_ref[...]
            return jnp.where(c < m, c, NEG_INF).max(axis=0, keepdims=True)

        tau = lax.fori_loop(1, PEER_TOPK, below_max, c00)
        c = cand_ref[...]
        sel = c >= tau
        z = jnp.where(sel, jnp.exp(c - c00), 0.0).sum(axis=0, keepdims=True)
        ind = jnp.where(sel, 1.0, 0.0)

        def row_total(r0, rows, a):
            return jnp.where(sub16 == a, ind[r0:r0 + rows, :].sum(axis=0, keepdims=True), 0.0)

        cnt = (ind[16:32, :] + jnp.concatenate([ind[40:48, :] + ind[56:64, :], zeros8], axis=0)
               + row_total(0, 16, 0) + row_total(32, 8, 1) + row_total(48, 8, 2) + row_total(64, 8, 3))
        s1 = s_ref[0]
        s2 = s_ref[1]
        c1 = jnp.zeros((PEER_N_KEYS, t), F32)
        for a in range(PEER_TOPK):
            c1 = jnp.where(s1 == v1[a:a + 1, :], cnt[a:a + 1, :], c1)
        c1_ref[h] = c1
        r2_ref[h] = r2.astype(BF)
        e1_ref[h] = jnp.exp(s1 - v1[0:1, :]) * (1.0 / z)
        e2_ref[h] = jnp.exp(s2 - v2[0:1, :]).astype(BF)
        return carry

    lax.fori_loop(0, PEER_HEADS, head, 0)


def _bcast_rows_bf16(row):
    tile = jnp.broadcast_to(row, (16, LANES)).astype(BF)
    return jnp.concatenate([tile] * (PEER_N_KEYS // 16), axis=0)


def _peer_dense(at_ref, wt_ref, blk, c1_ref, e1_ref, r2_ref, e2_ref):
    t = at_ref.shape[1]
    i1_0 = pl.multiple_of(blk * PEER_I1_PER_BLK, PEER_I1_PER_BLK)
    for lg in range(t // LANES):
        ls = slice(lg * LANES, (lg + 1) * LANES)
        c1_s = [c1_ref[h, pl.ds(i1_0, PEER_I1_PER_BLK), ls] for h in range(PEER_HEADS)]
        e1_s = [e1_ref[h, pl.ds(i1_0, PEER_I1_PER_BLK), ls] for h in range(PEER_HEADS)]
        for j in range(PEER_I1_PER_BLK):
            rs = slice(j * PEER_N_KEYS, (j + 1) * PEER_N_KEYS)
            g = jnp.zeros((PEER_N_KEYS, LANES), BF)
            for h in range(PEER_HEADS):
                c1 = _bcast_rows_bf16(c1_s[h][j:j + 1, :])
                e1 = _bcast_rows_bf16(e1_s[h][j:j + 1, :])
                g = g + e1 * jnp.where(r2_ref[h, :, ls] < c1, e2_ref[h, :, ls], jnp.zeros((), BF))
            wt_ref[rs, ls] = _gelu_tanh(at_ref[rs, ls]).astype(BF) * g


def _peer_kernel(hn_ref, wq_ref, sk_ref, u_ref, vt_ref, o_ref,
                 acc_ref, at_ref, wt_ref, c1_ref, e1_ref, r2_ref, e2_ref, s_ref, v_ref, cand_ref):
    e = pl.program_id(1)

    @pl.when(e == 0)
    def _():
        _peer_prologue(hn_ref, wq_ref, sk_ref, c1_ref, e1_ref, r2_ref, e2_ref, s_ref, v_ref, cand_ref)
        acc_ref[...] = jnp.zeros_like(acc_ref)

    at_ref[...] = _dot_nt(u_ref[...], hn_ref[...])
    _peer_dense(at_ref, wt_ref, e, c1_ref, e1_ref, r2_ref, e2_ref)
    acc_ref[...] += _dot(vt_ref[...], wt_ref[...])

    @pl.when(e == pl.num_programs(1) - 1)
    def _():
        o_ref[...] = acc_ref[...].T


def _peer(hn, lw):
    ntok, d = hn.shape
    t = PEER_TOK
    n_blk = PEER_N_EXPERTS // PEER_EBLK
    tab_f = pltpu.VMEM((PEER_HEADS, PEER_N_KEYS, t), F32)
    tab_b = pltpu.VMEM((PEER_HEADS, PEER_N_KEYS, t), BF)
    return pl.pallas_call(
        _peer_kernel,
        out_shape=jax.ShapeDtypeStruct((ntok, d), F32),
        grid=(ntok // t, n_blk),
        in_specs=[
            pl.BlockSpec((t, d), lambda i, e: (i, 0)),
            pl.BlockSpec(lw["peer_wq"].shape, lambda i, e: (0, 0, 0)),
            pl.BlockSpec(lw["peer_sk"].shape, lambda i, e: (0, 0, 0, 0)),
            pl.BlockSpec((PEER_EBLK, d), lambda i, e: (e, 0)),
            pl.BlockSpec((None, d, PEER_EBLK), lambda i, e: (e, 0, 0)),
        ],
        out_specs=pl.BlockSpec((t, d), lambda i, e: (i, 0)),
        scratch_shapes=[
            pltpu.VMEM((d, t), F32),
            pltpu.VMEM((PEER_EBLK, t), F32),
            pltpu.VMEM((PEER_EBLK, t), BF),
            tab_f, tab_f, tab_b, tab_b,
            pltpu.VMEM((2, PEER_N_KEYS, t), F32),
            pltpu.VMEM((2, PEER_TOPK, t), F32),
            pltpu.VMEM((PEER_CAND_ROWS, t), F32),
        ],
        compiler_params=_cparams(("parallel", "arbitrary")),
        name="peer",
    )(hn, lw["peer_wq"], lw["peer_sk"], lw["peer_u"], lw["peer_vt"])


def _resid_kernel(x_ref, p_ref, mod_ref, g_ref, o_ref, *, final):
    y = x_ref[...] + mod_ref[5:6, :] * p_ref[...]
    if final:
        y = _rms(y) * g_ref[...]
    o_ref[...] = y


def _resid(x, peer_out, mod, mod_row0, final_g, final, tl):
    b, l, d = x.shape
    if mod_row0 is None:
        mod_spec = pl.BlockSpec((None, N_MOD, d), lambda i, j: (i, 0, 0))
    else:
        mod_spec = pl.BlockSpec((None, N_MOD, d), lambda i, j: (mod_row0, 0, 0))
    tile = pl.BlockSpec((None, tl, d), lambda i, j: (i, j, 0))
    return pl.pallas_call(
        functools.partial(_resid_kernel, final=final),
        out_shape=jax.ShapeDtypeStruct((b, l, d), F32),
        grid=(b, l // tl),
        in_specs=[tile, tile, mod_spec, pl.BlockSpec((1, d), lambda i, j: (0, 0))],
        out_specs=tile,
        compiler_params=_cparams(("parallel", "parallel")),
        name="resid",
    )(x, peer_out, mod, final_g)


def _rope_tables(l):
    rows = l // GRID_W
    row = jnp.repeat(jnp.arange(rows), GRID_W).astype(F32)
    col = jnp.tile(jnp.arange(GRID_W), rows).astype(F32)

    def cs(rot_dim):
        n = rot_dim // 4
        inv = ROPE_BASE ** (-jnp.arange(n, dtype=F32) / n)
        ang = jnp.concatenate([row[:, None] * inv, col[:, None] * inv], axis=-1)
        return jnp.cos(ang), jnp.sin(ang)

    ones = lambda w: jnp.ones((l, w), F32)
    zeros = lambda w: jnp.zeros((l, w), F32)
    ca, sa = cs(HEAD_DIM)
    cos_a = jnp.concatenate([ca, ca, ones(LANES - HEAD_DIM)], axis=1)
    sin_a = jnp.concatenate([-sa, sa, zeros(LANES - HEAD_DIM)], axis=1)
    cc, sc = cs(MLA_ROPE_DIM)
    tail = LANES - MLA_NOPE_DIM - MLA_ROPE_DIM
    cos_c = jnp.concatenate([ones(MLA_NOPE_DIM), cc, cc, ones(tail)], axis=1)
    sin_c = jnp.concatenate([zeros(MLA_NOPE_DIM), -sc, sc, zeros(tail)], axis=1)
    return cos_a, sin_a, cos_c, sin_c


def _peer_vt_blocks(v_tab):
    n, d = v_tab.shape
    return v_tab.astype(BF).reshape(n // PEER_EBLK, PEER_EBLK, d).transpose(0, 2, 1)


def _layer_weights(layer,norm1_g, norm2_g, w_in, gqa_qn_g, gqa_kn_g, pool_w, pool_scale, mla_qn_g, mla_kvn_g,
                   mla_w_uq, mla_w_ukv, w_out, peer_wq, peer_subkeys, peer_u, peer_v):
    d = D_MODEL
    w = w_in[layer]
    o = 0
    aq = w[:, o:o + GQA_Q_HEADS * HEAD_DIM]; o += GQA_Q_HEADS * HEAD_DIM
    ak = w[:, o:o + GQA_KV_HEADS * HEAD_DIM]; o += GQA_KV_HEADS * HEAD_DIM
    av = w[:, o:o + GQA_KV_HEADS * HEAD_DIM]; o += GQA_KV_HEADS * HEAD_DIM
    wb = w[:, o:o + POOL_WIDTH]; o += POOL_WIDTH
    wcq = w[:, o:o + MLA_Q_RANK]; o += MLA_Q_RANK
    wckv = w[:, o:o + MLA_KV_RANK]; o += MLA_KV_RANK
    wkr = w[:, o:o + MLA_ROPE_DIM]
    z64 = jnp.zeros((d, LANES - HEAD_DIM), F32)
    cols = []
    for h in range(GQA_Q_HEADS):
        cols += [aq[:, h * HEAD_DIM:(h + 1) * HEAD_DIM], z64]
    for h in range(GQA_Q_HEADS):
        g = h // GQA_GROUP
        cols += [ak[:, g * HEAD_DIM:(g + 1) * HEAD_DIM], z64]
    for h in range(GQA_Q_HEADS):
        g = h // GQA_GROUP
        cols += [av[:, g * HEAD_DIM:(g + 1) * HEAD_DIM]]
    cols += [wb, wcq, wckv]
    for h in range(MLA_HEADS):
        cols += [jnp.zeros((d, MLA_NOPE_DIM), F32), wkr,
                 jnp.zeros((d, LANES - MLA_NOPE_DIM - MLA_ROPE_DIM), F32)]
    w_wide = jnp.concatenate(cols, axis=1).astype(BF)
    assert w_wide.shape == (d, IN_WIDE)

    def head_gain(g):
        return jnp.concatenate([g, jnp.zeros((LANES - HEAD_DIM,), F32)])[None, :]

    qd = MLA_NOPE_DIM + MLA_ROPE_DIM
    uq = mla_w_uq[layer]
    ukv = mla_w_ukv[layer]
    uq_cols, uk_cols, uv_cols = [], [], []
    for h in range(MLA_HEADS):
        uq_cols += [uq[:, h * qd:(h + 1) * qd], jnp.zeros((MLA_Q_RANK, LANES - qd), F32)]
        k0 = h * (MLA_NOPE_DIM + MLA_V_DIM)
        uk_cols += [ukv[:, k0:k0 + MLA_NOPE_DIM], jnp.zeros((MLA_KV_RANK, LANES - MLA_NOPE_DIM), F32)]
        uv_cols += [ukv[:, k0 + MLA_NOPE_DIM:k0 + MLA_NOPE_DIM + MLA_V_DIM]]

    pw = pool_w[layer]
    w_bd = jnp.zeros((POOL_WIDTH, POOL_WIDTH), F32)
    for g in range(POOL_GROUPS):
        s = slice(g * POOL_GROUP_DIM, (g + 1) * POOL_GROUP_DIM)
        w_bd = w_bd.at[s, s].set(pw[g])

    wo = w_out[layer]
    na = GQA_Q_HEADS * HEAD_DIM
    w_o1 = jnp.concatenate([wo[0:na], wo[na + POOL_WIDTH:]], axis=0).astype(BF)
    w_o2 = wo[na:na + POOL_WIDTH].astype(BF)

    return {
        "n1_g": norm1_g[layer][None, :],
        "n2_g": norm2_g[layer][None, :],
        "w_in": w_wide,
        "gqa_qg": head_gain(gqa_qn_g[layer]),
        "gqa_kg": head_gain(gqa_kn_g[layer]),
        "mla_qg": mla_qn_g[layer][None, :],
        "mla_kvg": mla_kvn_g[layer][None, :],
        "w_uq": jnp.concatenate(uq_cols, axis=1).astype(BF),
        "w_uk": jnp.concatenate(uk_cols, axis=1).astype(BF),
        "w_uv": jnp.concatenate(uv_cols, axis=1).astype(BF),
        "pool_w": w_bd.astype(BF),
        "pool_scale": pool_scale[layer][None, :],
        "w_o1": w_o1,
        "w_o2": w_o2,
        "peer_wq": peer_wq[layer].reshape(d, PEER_HEADS, PEER_QUERY_DIM).transpose(1, 0, 2).astype(BF),
        "peer_sk": peer_subkeys[layer].astype(BF),
        "peer_u": peer_u[layer].astype(BF),
        "peer_vt": _peer_vt_blocks(peer_v[layer]),
    }


def _tile(l, pref):
    return pref if l % pref == 0 else l


def kernel(x, c, ctx, c_ctx, ada_w, ada_b, norm1_g, norm2_g, w_in, gqa_qn_g, gqa_kn_g, pool_w, pool_scale,
           mla_qn_g, mla_kvn_g, mla_w_uq, mla_w_ukv, w_out, peer_wq, peer_subkeys, peer_u, peer_v, final_g):
    b, l, d = x.shape
    lc = ctx.shape[1]
    depth = ada_w.shape[0]
    assert d == D_MODEL and l % GRID_W == 0
    assert (b * l) % PEER_TOK == 0 and (b * lc) % PEER_TOK == 0

    n_rows = -(-(b + 1) // 8) * 8
    cc = jnp.concatenate([c, c_ctx[None, :], jnp.zeros((n_rows - b - 1, d), F32)], axis=0)
    mod_all = _adaln(cc, ada_w, ada_b).reshape(depth, n_rows, N_MOD, d)

    rope_tabs = _rope_tables(l)
    tl_l, tl_c = _tile(l, 512), _tile(lc, 512)
    tq_l, tq_c = _tile(l, 256), _tile(lc, 256)
    fg = final_g[None, :]

    xl, xc = x, ctx
    for layer in range(depth):
        lw = _layer_weights(layer, norm1_g, norm2_g, w_in, gqa_qn_g, gqa_kn_g, pool_w, pool_scale, mla_qn_g,
                            mla_kvn_g, mla_w_uq, mla_w_ukv, w_out, peer_wq, peer_subkeys, peer_u, peer_v)
        mod = mod_all[layer]
        last = layer == depth - 1
        q_l, k_l, v_l, pb_l = _inproj(xl, mod, None, lw, rope_tabs, tl_l)
        q_c, k_c, v_c, pb_c = _inproj(xc, mod, b, lw, None, tl_c)

        att_l = _attn(q_l, [k_c, k_l], [v_c, v_l], tq_l)
        ob_l = _pool(pb_l, lw["pool_w"], lw["pool_scale"])
        xl_mid, hn_l = _outproj(xl, att_l, ob_l, mod, None, lw, tl_l)
        peer_l = _peer(hn_l.reshape(b * l, d), lw).reshape(b, l, d)
        xl = _resid(xl_mid, peer_l, mod, None, fg, last, tl_l)

        if not last:
            att_c = _attn(q_c, [k_c], [v_c], tq_c)
            ob_c = _pool(pb_c, lw["pool_w"], lw["pool_scale"])
            xc_mid, hn_c = _outproj(xc, att_c, ob_c, mod, b, lw, tl_c)
            peer_c = _peer(hn_c.reshape(b * lc, d), lw).reshape(b, lc, d)
            xc = _resid(xc_mid, peer_c, mod, b, fg, False, tl_c)
    return xl
```

```python
import functools

import jax
import jax.numpy as jnp
from jax import lax
from jax.experimental import pallas as pl
from jax.experimental.pallas import tpu as pltpu

D_MODEL = 1024
GRID_W = 64
N_MOD = 6
EPS = 1e-6
ROPE_BASE = 10000.0
HEAD_DIM = 64
GQA_Q_HEADS = 6
GQA_KV_HEADS = 2
GQA_GROUP = GQA_Q_HEADS // GQA_KV_HEADS
POOL_GROUPS = 4
POOL_WINDOWS = (2, 4, 8, 16)
POOL_WIDTH = D_MODEL // 4
POOL_GROUP_DIM = POOL_WIDTH // POOL_GROUPS
MLA_HEADS = 6
MLA_NOPE_DIM = 64
MLA_ROPE_DIM = 32
MLA_V_DIM = 64
MLA_Q_RANK = 384
MLA_KV_RANK = 256
PEER_HEADS = 8
PEER_N_KEYS = 128
PEER_N_EXPERTS = PEER_N_KEYS * PEER_N_KEYS
PEER_TOPK = 16
PEER_QUERY_DIM = 256
PEER_HALF = PEER_QUERY_DIM // 2

LANES = 128
N_ATT_HEADS = GQA_Q_HEADS + MLA_HEADS
QK_WIDTH = N_ATT_HEADS * LANES
V_WIDTH = N_ATT_HEADS * HEAD_DIM
POOL_PAD = 16

_C_QA = 0
_C_KA = _C_QA + GQA_Q_HEADS * LANES
_C_VA = _C_KA + GQA_Q_HEADS * LANES
_C_B = _C_VA + GQA_Q_HEADS * HEAD_DIM
_C_CQ = _C_B + POOL_WIDTH
_C_CKV = _C_CQ + MLA_Q_RANK
_C_KR = _C_CKV + MLA_KV_RANK
IN_WIDE = _C_KR + MLA_HEADS * LANES

PEER_TOK = 512
PEER_EBLK = 2048
PEER_I1_PER_BLK = PEER_EBLK // PEER_N_KEYS
PEER_CAND_ROWS = 2 * PEER_TOPK + 5 * 8
VMEM_LIMIT = 56 * 1024 * 1024

BF = jnp.bfloat16
F32 = jnp.float32
LOG2_E = 1.4426950408889634
NEG_INF = float("-inf")
POS_INF = float("inf")


def _cparams(sem):
    return pltpu.CompilerParams(dimension_semantics=sem, vmem_limit_bytes=VMEM_LIMIT)


def _dot(a, b):
    return jnp.dot(a, b, preferred_element_type=F32)


def _dot_nt(a, b):
    return lax.dot_general(a, b, (((1,), (1,)), ((), ())), preferred_element_type=F32)


def _split_bf16(a):
    hi = a.astype(BF)
    lo = (a - hi.astype(F32)).astype(BF)
    return hi, lo


def _dot3(a, b):
    ah, al = _split_bf16(a)
    bh, bl = _split_bf16(b)
    return _dot(ah, bh) + _dot(ah, bl) + _dot(al, bh)


def _rms(x):
    return x * lax.rsqrt(jnp.mean(x * x, axis=-1, keepdims=True) + EPS)


def _adaln_kernel(c_ref, w_ref, b_ref, o_ref):
    c = c_ref[...]
    s = c * (1.0 / (1.0 + jnp.exp(-c)))
    o_ref[...] = _dot3(s, w_ref[...]) + b_ref[...]


def _adaln(cc, ada_w, ada_b):
    depth, d, nd = ada_w.shape
    r = cc.shape[0]
    nblk = nd // d
    return pl.pallas_call(
        _adaln_kernel,
        out_shape=jax.ShapeDtypeStruct((depth, r, nd), F32),
        grid=(depth, nblk),
        in_specs=[
            pl.BlockSpec((r, d), lambda l, j: (0, 0)),
            pl.BlockSpec((None, d, d), lambda l, j: (l, 0, j)),
            pl.BlockSpec((None, 1, d), lambda l, j: (l, 0, j)),
        ],
        out_specs=pl.BlockSpec((None, r, d), lambda l, j: (l, 0, j)),
        compiler_params=_cparams(("arbitrary", "arbitrary")),
        name="adaln",
    )(cc, ada_w, ada_b.reshape(depth, 1, nd))


def _rope_block(xb, cb, sb, half):
    lane = lax.broadcasted_iota(jnp.int32, xb.shape, 1)
    fwd = pltpu.roll(xb, LANES - half, 1)
    bwd = pltpu.roll(xb, half, 1)
    sw = jnp.where((lane & half) == 0, fwd, bwd)
    return xb * cb + sw * sb


def _inproj_kernel(*refs, rope):
    if rope:
        (x_ref, mod_ref, n1_ref, w_ref, qg_ref, kg_ref, cqg_ref, ckvg_ref, wuq_ref, wuk_ref, wuv_ref,
         ca_ref, sa_ref, cc_ref, sc_ref, q_ref, k_ref, v_ref, b_ref) = refs
    else:
        (x_ref, mod_ref, n1_ref, w_ref, qg_ref, kg_ref, cqg_ref, ckvg_ref, wuq_ref, wuk_ref, wuv_ref,
         q_ref, k_ref, v_ref, b_ref) = refs
    x = x_ref[...]
    shift = mod_ref[0:1, :]
    scale = mod_ref[1:2, :]
    h = _rms(x) * n1_ref[...] * (1.0 + scale) + shift
    p = _dot(h.astype(BF), w_ref[...])

    sa = HEAD_DIM ** -0.5 * LOG2_E
    sc = (MLA_NOPE_DIM + MLA_ROPE_DIM) ** -0.5 * LOG2_E

    for hh in range(GQA_Q_HEADS):
        for (c0, g_ref, o_ref, mul) in ((_C_QA, qg_ref, q_ref, sa), (_C_KA, kg_ref, k_ref, 1.0)):
            blk = p[:, c0 + hh * LANES:c0 + (hh + 1) * LANES]
            ms = jnp.sum(blk * blk, axis=-1, keepdims=True) * (1.0 / HEAD_DIM)
            y = blk * lax.rsqrt(ms + EPS) * g_ref[...]
            if rope:
                y = _rope_block(y, ca_ref[...], sa_ref[...], HEAD_DIM // 2)
            if mul != 1.0:
                y = y * mul
            o_ref[:, hh * LANES:(hh + 1) * LANES] = y.astype(BF)
    nva = GQA_Q_HEADS * HEAD_DIM
    v_ref[:, 0:nva] = p[:, _C_VA:_C_VA + nva].astype(BF)
    b_ref[...] = p[:, _C_B:_C_B + POOL_WIDTH]

    cq = _rms(p[:, _C_CQ:_C_CQ + MLA_Q_RANK]) * cqg_ref[...]
    ckv = (_rms(p[:, _C_CKV:_C_CKV + MLA_KV_RANK]) * ckvg_ref[...]).astype(BF)
    qc = _dot(cq.astype(BF), wuq_ref[...])
    kc = _dot(ckv, wuk_ref[...]) + p[:, _C_KR:_C_KR + MLA_HEADS * LANES]
    vc = _dot(ckv, wuv_ref[...])
    base = GQA_Q_HEADS * LANES
    for hh in range(MLA_HEADS):
        qb = qc[:, hh * LANES:(hh + 1) * LANES]
        kb = kc[:, hh * LANES:(hh + 1) * LANES]
        if rope:
            qb = _rope_block(qb, cc_ref[...], sc_ref[...], MLA_ROPE_DIM // 2)
            kb = _rope_block(kb, cc_ref[...], sc_ref[...], MLA_ROPE_DIM // 2)
        q_ref[:, base + hh * LANES:base + (hh + 1) * LANES] = (qb * sc).astype(BF)
        k_ref[:, base + hh * LANES:base + (hh + 1) * LANES] = kb.astype(BF)
    v_ref[:, nva:V_WIDTH] = vc.astype(BF)


def _inproj(x, mod, mod_row0, lw, rope_tabs, tl):
    b, l, d = x.shape
    rope = rope_tabs is not None
    grid = (b, l // tl)

    def full(a):
        nd = a.ndim
        return pl.BlockSpec(a.shape, lambda i, j: (0,) * nd)

    if mod_row0 is None:
        mod_spec = pl.BlockSpec((None, N_MOD, d), lambda i, j: (i, 0, 0))
    else:
        mod_spec = pl.BlockSpec((None, N_MOD, d), lambda i, j: (mod_row0, 0, 0))
    weights = [lw["n1_g"], lw["w_in"], lw["gqa_qg"], lw["gqa_kg"], lw["mla_qg"], lw["mla_kvg"],
               lw["w_uq"], lw["w_uk"], lw["w_uv"]]
    in_specs = [pl.BlockSpec((None, tl, d), lambda i, j: (i, j, 0)), mod_spec] + [full(a) for a in weights]
    args = [x, mod] + weights
    if rope:
        in_specs += [pl.BlockSpec((tl, LANES), lambda i, j: (j, 0)) for _ in range(4)]
        args += list(rope_tabs)
    def rows(w, dt):
        return jax.ShapeDtypeStruct((b, l, w), dt), pl.BlockSpec((None, tl, w), lambda i, j: (i, j, 0))

    outs = [rows(QK_WIDTH, BF), rows(QK_WIDTH, BF), rows(V_WIDTH, BF), rows(POOL_WIDTH, F32)]
    return pl.pallas_call(
        functools.partial(_inproj_kernel, rope=rope),
        out_shape=[o[0] for o in outs],
        grid=grid,
        in_specs=in_specs,
        out_specs=[o[1] for o in outs],
        compiler_params=_cparams(("parallel", "parallel")),
        name="inproj",
    )(*args)


def _attn_kernel(*refs, n_src):
    q_ref = refs[0]
    k_refs = refs[1:1 + n_src]
    v_refs = refs[1 + n_src:1 + 2 * n_src]
    o_ref = refs[1 + 2 * n_src]
    tq = q_ref.shape[0]
    lane = lax.broadcasted_iota(jnp.int32, (tq, LANES), 1)
    for jb in range(N_ATT_HEADS // 2):
        halves = []
        for n in (2 * jb, 2 * jb + 1):
            q = q_ref[:, n * LANES:(n + 1) * LANES]
            ss = [_dot_nt(q, k_ref[:, n * LANES:(n + 1) * LANES]) for k_ref in k_refs]
            m = ss[0].max(axis=-1, keepdims=True)
            for s in ss[1:]:
                m = jnp.maximum(m, s.max(axis=-1, keepdims=True))
            acc = None
            den = None
            for s, v_ref in zip(ss, v_refs):
                e = jnp.exp2(s - m)
                dsum = e.sum(axis=-1, keepdims=True)
                pv = _dot(e.astype(BF), v_ref[:, jb * LANES:(jb + 1) * LANES])
                acc = pv if acc is None else acc + pv
                den = dsum if den is None else den + dsum
            halves.append(acc * (1.0 / den))
        o = jnp.where(lane < HEAD_DIM, halves[0], halves[1])
        o_ref[:, jb * LANES:(jb + 1) * LANES] = o.astype(BF)


def _attn(q, ks, vs, tq):
    b, l, _ = q.shape
    n_src = len(ks)
    in_specs = [pl.BlockSpec((None, tq, QK_WIDTH), lambda i, j: (i, j, 0))]
    in_specs += [pl.BlockSpec((None, k.shape[1], QK_WIDTH), lambda i, j: (i, 0, 0)) for k in ks]
    in_specs += [pl.BlockSpec((None, v.shape[1], V_WIDTH), lambda i, j: (i, 0, 0)) for v in vs]
    return pl.pallas_call(
        functools.partial(_attn_kernel, n_src=n_src),
        out_shape=jax.ShapeDtypeStruct((b, l, V_WIDTH), BF),
        grid=(b, l // tq),
        in_specs=in_specs,
        out_specs=pl.BlockSpec((None, tq, V_WIDTH), lambda i, j: (i, j, 0)),
        compiler_params=_cparams(("parallel", "arbitrary")),
        name="attn",
    )(q, *ks, *vs)


def _pool_kernel(b_ref, w_ref, s_ref, o_ref, xp_ref):
    l = b_ref.shape[0]
    x = b_ref[...]
    zeros = jnp.zeros((POOL_PAD, POOL_WIDTH), F32)
    xp_ref[0:POOL_PAD, :] = zeros
    xp_ref[POOL_PAD + l:POOL_PAD + l + POOL_PAD, :] = zeros
    xp_ref[POOL_PAD:POOL_PAD + l, :] = x
    t = lax.broadcasted_iota(jnp.int32, (l, LANES), 0)
    lane = lax.broadcasted_iota(jnp.int32, (l, LANES), 1)
    outs = []
    for half in range(POOL_WIDTH // LANES):
        w_small = POOL_WINDOWS[2 * half]
        w_big = POOL_WINDOWS[2 * half + 1]
        lo, hi = half * LANES, (half + 1) * LANES

        def win(j):
            return xp_ref[pl.ds(POOL_PAD + j, l), lo:hi]

        s_small = None
        for j in range(-(w_small // 2), w_small // 2):
            s_small = win(j) if s_small is None else s_small + win(j)
        s_big = s_small
        for j in range(-(w_big // 2), w_big // 2):
            if not (-(w_small // 2) <= j < w_small // 2):
                s_big = s_big + win(j)

        def cnt(w):
            lo_i = jnp.maximum(t - w // 2, 0)
            hi_i = jnp.minimum(t - w // 2 + w, l)
            return (hi_i - lo_i).astype(F32)

        left = lane < POOL_GROUP_DIM
        s = jnp.where(left, s_small, s_big)
        c = jnp.where(left, cnt(w_small), cnt(w_big))
        outs.append(s / c - x[:, lo:hi])
    pooled = jnp.concatenate(outs, axis=1).astype(BF)
    o_ref[...] = (_dot(pooled, w_ref[...]) * s_ref[...]).astype(BF)


def _pool(pb, w_bd, pscale):
    b, l, w = pb.shape
    return pl.pallas_call(
        _pool_kernel,
        out_shape=jax.ShapeDtypeStruct((b, l, w), BF),
        grid=(b,),
        in_specs=[
            pl.BlockSpec((None, l, w), lambda i: (i, 0, 0)),
            pl.BlockSpec((w, w), lambda i: (0, 0)),
            pl.BlockSpec((1, w), lambda i: (0, 0)),
        ],
        out_specs=pl.BlockSpec((None, l, w), lambda i: (i, 0, 0)),
        scratch_shapes=[pltpu.VMEM((l + 2 * POOL_PAD, w), F32)],
        compiler_params=_cparams(("parallel",)),
        name="pool",
    )(pb, w_bd, pscale)


def _outproj_kernel(x_ref, att_ref, ob_ref, mod_ref, n2_ref, w1_ref, w2_ref, xo_ref, h_ref):
    y = _dot(att_ref[...], w1_ref[...]) + _dot(ob_ref[...], w2_ref[...])
    xn = x_ref[...] + mod_ref[2:3, :] * y
    xo_ref[...] = xn
    h = _rms(xn) * n2_ref[...] * (1.0 + mod_ref[4:5, :]) + mod_ref[3:4, :]
    h_ref[...] = h.astype(BF)


def _outproj(x, att, ob, mod, mod_row0, lw, tl):
    b, l, d = x.shape
    if mod_row0 is None:
        mod_spec = pl.BlockSpec((None, N_MOD, d), lambda i, j: (i, 0, 0))
    else:
        mod_spec = pl.BlockSpec((None, N_MOD, d), lambda i, j: (mod_row0, 0, 0))

    def tile(w):
        return pl.BlockSpec((None, tl, w), lambda i, j: (i, j, 0))

    def full(a):
        return pl.BlockSpec(a.shape, lambda i, j: (0, 0))

    return pl.pallas_call(
        _outproj_kernel,
        out_shape=[jax.ShapeDtypeStruct((b, l, d), F32), jax.ShapeDtypeStruct((b, l, d), BF)],
        grid=(b, l // tl),
        in_specs=[tile(d), tile(V_WIDTH), tile(POOL_WIDTH), mod_spec, full(lw["n2_g"]), full(lw["w_o1"]),
                  full(lw["w_o2"])],
        out_specs=[tile(d), tile(d)],
        compiler_params=_cparams(("parallel", "parallel")),
        name="outproj",
    )(x, att, ob, mod, lw["n2_g"], lw["w_o1"], lw["w_o2"])


def _gelu_tanh(x):
    return 0.5 * x * (1.0 + jnp.tanh(0.7978845608028654 * (x + 0.044715 * (x * x * x))))


def _peer_prologue(hn_ref, wq_ref, sk_ref, c1_ref, e1_ref, r2_ref, e2_ref, s_ref, v_ref, cand_ref):
    t = hn_ref.shape[0]
    hn = hn_ref[...]

    sub8 = lax.broadcasted_iota(jnp.int32, (8, t), 0)
    sub16 = lax.broadcasted_iota(jnp.int32, (PEER_TOPK, t), 0)
    zeros8 = jnp.zeros((8, t), F32)

    def head(h, carry):
        qh = _dot(hn, wq_ref[h]).astype(BF)
        r2 = None
        for p in range(2):
            st = _dot_nt(sk_ref[h, p], qh[:, p * PEER_HALF:(p + 1) * PEER_HALF])
            s_ref[p] = st
            m = st.max(axis=0, keepdims=True)
            v_ref[p, 0:1, :] = m
            rank = jnp.zeros((PEER_N_KEYS, t), F32)
            for k in range(1, PEER_TOPK):
                below = st < m
                if p == 1:
                    rank = rank + jnp.where(below, 1.0, 0.0)
                m = jnp.where(below, st, NEG_INF).max(axis=0, keepdims=True)
                v_ref[p, k:k + 1, :] = m
            if p == 1:
                r2 = rank + jnp.where(st < m, 1.0, 0.0)
        v1 = v_ref[0]
        v2 = v_ref[1]

        groups = [
            v1[0:1, :] + v2,
            jnp.where(sub16 >= 1, v1 + v2[0:1, :], NEG_INF),
            jnp.where(sub8 >= 1, v1[1:2, :] + v2[0:8, :], NEG_INF),
            jnp.where(sub8 >= 2, v1[0:8, :] + v2[1:2, :], NEG_INF),
            jnp.where((sub8 >= 2) & (sub8 <= 4), v1[2:3, :] + v2[0:8, :], NEG_INF),
            jnp.where((sub8 >= 3) & (sub8 <= 4), v1[0:8, :] + v2[2:3, :], NEG_INF),
            jnp.where(sub8 == 3, v1[3:4, :] + v2[0:8, :], NEG_INF),
        ]
        cand_ref[...] = jnp.concatenate(groups, axis=0)
        c00 = v1[0:1, :] + v2[0:1, :]

        def below_max(k, m):
            c = cand_ref[...]
            return jnp.where(c < m, c, NEG_INF).max(axis=0, keepdims=True)

        tau = lax.fori_loop(1, PEER_TOPK, below_max, c00)
        c = cand_ref[...]
        sel = c >= tau
        z = jnp.where(sel, jnp.exp(c - c00), 0.0).sum(axis=0, keepdims=True)
        ind = jnp.where(sel, 1.0, 0.0)

        def row_total(r0, rows, a):
            return jnp.where(sub16 == a, ind[r0:r0 + rows, :].sum(axis=0, keepdims=True), 0.0)

        cnt = (ind[16:32, :] + jnp.concatenate([ind[40:48, :] + ind[56:64, :], zeros8], axis=0)
               + row_total(0, 16, 0) + row_total(32, 8, 1) + row_total(48, 8, 2) + row_total(64, 8, 3))
        s1 = s_ref[0]
        s2 = s_ref[1]
        c1 = jnp.zeros((PEER_N_KEYS, t), F32)
        for a in range(PEER_TOPK):
            c1 = jnp.where(s1 == v1[a:a + 1, :], cnt[a:a + 1, :], c1)
        c1_ref[h] = c1
        r2_ref[h] = r2.astype(BF)
        e1_ref[h] = jnp.exp(s1 - v1[0:1, :]) * (1.0 / z)
        e2_ref[h] = jnp.exp(s2 - v2[0:1, :]).astype(BF)
        return carry

    lax.fori_loop(0, PEER_HEADS, head, 0)


def _bcast_rows_bf16(row):
    tile = jnp.broadcast_to(row, (16, LANES)).astype(BF)
    return jnp.concatenate([tile] * (PEER_N_KEYS // 16), axis=0)


def _peer_dense(at_ref, wt_ref, blk, c1_ref, e1_ref, r2_ref, e2_ref):
    t = at_ref.shape[1]
    i1_0 = pl.multiple_of(blk * PEER_I1_PER_BLK, PEER_I1_PER_BLK)
    for lg in range(t // LANES):
        ls = slice(lg * LANES, (lg + 1) * LANES)
        c1_s = [c1_ref[h, pl.ds(i1_0, PEER_I1_PER_BLK), ls] for h in range(PEER_HEADS)]
        e1_s = [e1_ref[h, pl.ds(i1_0, PEER_I1_PER_BLK), ls] for h in range(PEER_HEADS)]
        for j in range(PEER_I1_PER_BLK):
            rs = slice(j * PEER_N_KEYS, (j + 1) * PEER_N_KEYS)
            g = jnp.zeros((PEER_N_KEYS, LANES), BF)
            for h in range(PEER_HEADS):
                c1 = _bcast_rows_bf16(c1_s[h][j:j + 1, :])
                e1 = _bcast_rows_bf16(e1_s[h][j:j + 1, :])
                g = g + e1 * jnp.where(r2_ref[h, :, ls] < c1, e2_ref[h, :, ls], jnp.zeros((), BF))
            wt_ref[rs, ls] = _gelu_tanh(at_ref[rs, ls]).astype(BF) * g


def _peer_kernel(hn_ref, x_ref, mod_ref, fg_ref, wq_ref, sk_ref, u_ref, vt_ref, o_ref,
                 acc_ref, at_ref, wt_ref, c1_ref, e1_ref, r2_ref, e2_ref, s_ref, v_ref, cand_ref, *, final):
    e = pl.program_id(1)

    @pl.when(e == 0)
    def _():
        _peer_prologue(hn_ref, wq_ref, sk_ref, c1_ref, e1_ref, r2_ref, e2_ref, s_ref, v_ref, cand_ref)
        acc_ref[...] = jnp.zeros_like(acc_ref)

    at_ref[...] = _dot_nt(u_ref[...], hn_ref[...])
    _peer_dense(at_ref, wt_ref, e, c1_ref, e1_ref, r2_ref, e2_ref)
    acc_ref[...] += _dot(vt_ref[...], wt_ref[...])

    @pl.when(e == pl.num_programs(1) - 1)
    def _():
        y = x_ref[...] + mod_ref[5:6, :] * acc_ref[...].T
        if final:
            y = _rms(y) * fg_ref[...]
        o_ref[...] = y


def _peer(hn, x_mid, mod, mod_row0, tok_per_row, final_g, final, lw):
    ntok, d = hn.shape
    t = PEER_TOK
    n_blk = PEER_N_EXPERTS // PEER_EBLK
    if mod_row0 is None:
        assert tok_per_row % t == 0
        tiles_per_row = tok_per_row // t
        mod_spec = pl.BlockSpec((None, N_MOD, d), lambda i, e: (i // tiles_per_row, 0, 0))
    else:
        mod_spec = pl.BlockSpec((None, N_MOD, d), lambda i, e: (mod_row0, 0, 0))
    tab_f = pltpu.VMEM((PEER_HEADS, PEER_N_KEYS, t), F32)
    tab_b = pltpu.VMEM((PEER_HEADS, PEER_N_KEYS, t), BF)
    return pl.pallas_call(
        functools.partial(_peer_kernel, final=final),
        out_shape=jax.ShapeDtypeStruct((ntok, d), F32),
        grid=(ntok // t, n_blk),
        in_specs=[
            pl.BlockSpec((t, d), lambda i, e: (i, 0)),
            pl.BlockSpec((t, d), lambda i, e: (i, 0)),
            mod_spec,
            pl.BlockSpec((1, d), lambda i, e: (0, 0)),
            pl.BlockSpec(lw["peer_wq"].shape, lambda i, e: (0, 0, 0)),
            pl.BlockSpec(lw["peer_sk"].shape, lambda i, e: (0, 0, 0, 0)),
            pl.BlockSpec((PEER_EBLK, d), lambda i, e: (e, 0)),
            pl.BlockSpec((None, d, PEER_EBLK), lambda i, e: (e, 0, 0)),
        ],
        out_specs=pl.BlockSpec((t, d), lambda i, e: (i, 0)),
        scratch_shapes=[
            pltpu.VMEM((d, t), F32),
            pltpu.VMEM((PEER_EBLK, t), F32),
            pltpu.VMEM((PEER_EBLK, t), BF),
            tab_f, tab_f, tab_b, tab_b,
            pltpu.VMEM((2, PEER_N_KEYS, t), F32),
            pltpu.VMEM((2, PEER_TOPK, t), F32),
            pltpu.VMEM((PEER_CAND_ROWS, t), F32),
        ],
        compiler_params=_cparams(("parallel", "arbitrary")),
        name="peer",
    )(hn, x_mid, mod, final_g, lw["peer_wq"], lw["peer_sk"], lw["peer_u"], lw["peer_vt"])


def _rope_tables(l):
    rows = l // GRID_W
    row = jnp.repeat(jnp.arange(rows), GRID_W).astype(F32)
    col = jnp.tile(jnp.arange(GRID_W), rows).astype(F32)

    def cs(rot_dim):
        n = rot_dim // 4
        inv = ROPE_BASE ** (-jnp.arange(n, dtype=F32) / n)
        ang = jnp.concatenate([row[:, None] * inv, col[:, None] * inv], axis=-1)
        return jnp.cos(ang), jnp.sin(ang)

    ones = lambda w: jnp.ones((l, w), F32)
    zeros = lambda w: jnp.zeros((l, w), F32)
    ca, sa = cs(HEAD_DIM)
    cos_a = jnp.concatenate([ca, ca, ones(LANES - HEAD_DIM)], axis=1)
    sin_a = jnp.concatenate([-sa, sa, zeros(LANES - HEAD_DIM)], axis=1)
    cc, sc = cs(MLA_ROPE_DIM)
    tail = LANES - MLA_NOPE_DIM - MLA_ROPE_DIM
    cos_c = jnp.concatenate([ones(MLA_NOPE_DIM), cc, cc, ones(tail)], axis=1)
    sin_c = jnp.concatenate([zeros(MLA_NOPE_DIM), -sc, sc, zeros(tail)], axis=1)
    return cos_a, sin_a, cos_c, sin_c


def _peer_vt_blocks(v_tab):
    n, d = v_tab.shape
    return v_tab.astype(BF).reshape(n // PEER_EBLK, PEER_EBLK, d).transpose(0, 2, 1)


def _layer_weights(layer,norm1_g, norm2_g, w_in, gqa_qn_g, gqa_kn_g, pool_w, pool_scale, mla_qn_g, mla_kvn_g,
                   mla_w_uq, mla_w_ukv, w_out, peer_wq, peer_subkeys, peer_u, peer_v):
    d = D_MODEL
    w = w_in[layer]
    o = 0
    aq = w[:, o:o + GQA_Q_HEADS * HEAD_DIM]; o += GQA_Q_HEADS * HEAD_DIM
    ak = w[:, o:o + GQA_KV_HEADS * HEAD_DIM]; o += GQA_KV_HEADS * HEAD_DIM
    av = w[:, o:o + GQA_KV_HEADS * HEAD_DIM]; o += GQA_KV_HEADS * HEAD_DIM
    wb = w[:, o:o + POOL_WIDTH]; o += POOL_WIDTH
    wcq = w[:, o:o + MLA_Q_RANK]; o += MLA_Q_RANK
    wckv = w[:, o:o + MLA_KV_RANK]; o += MLA_KV_RANK
    wkr = w[:, o:o + MLA_ROPE_DIM]
    z64 = jnp.zeros((d, LANES - HEAD_DIM), F32)
    cols = []
    for h in range(GQA_Q_HEADS):
        cols += [aq[:, h * HEAD_DIM:(h + 1) * HEAD_DIM], z64]
    for h in range(GQA_Q_HEADS):
        g = h // GQA_GROUP
        cols += [ak[:, g * HEAD_DIM:(g + 1) * HEAD_DIM], z64]
    for h in range(GQA_Q_HEADS):
        g = h // GQA_GROUP
        cols += [av[:, g * HEAD_DIM:(g + 1) * HEAD_DIM]]
    cols += [wb, wcq, wckv]
    for h in range(MLA_HEADS):
        cols += [jnp.zeros((d, MLA_NOPE_DIM), F32), wkr,
                 jnp.zeros((d, LANES - MLA_NOPE_DIM - MLA_ROPE_DIM), F32)]
    w_wide = jnp.concatenate(cols, axis=1).astype(BF)
    assert w_wide.shape == (d, IN_WIDE)

    def head_gain(g):
        return jnp.concatenate([g, jnp.zeros((LANES - HEAD_DIM,), F32)])[None, :]

    qd = MLA_NOPE_DIM + MLA_ROPE_DIM
    uq = mla_w_uq[layer]
    ukv = mla_w_ukv[layer]
    uq_cols, uk_cols, uv_cols = [], [], []
    for h in range(MLA_HEADS):
        uq_cols += [uq[:, h * qd:(h + 1) * qd], jnp.zeros((MLA_Q_RANK, LANES - qd), F32)]
        k0 = h * (MLA_NOPE_DIM + MLA_V_DIM)
        uk_cols += [ukv[:, k0:k0 + MLA_NOPE_DIM], jnp.zeros((MLA_KV_RANK, LANES - MLA_NOPE_DIM), F32)]
        uv_cols += [ukv[:, k0 + MLA_NOPE_DIM:k0 + MLA_NOPE_DIM + MLA_V_DIM]]

    pw = pool_w[layer]
    w_bd = jnp.zeros((POOL_WIDTH, POOL_WIDTH), F32)
    for g in range(POOL_GROUPS):
        s = slice(g * POOL_GROUP_DIM, (g + 1) * POOL_GROUP_DIM)
        w_bd = w_bd.at[s, s].set(pw[g])

    wo = w_out[layer]
    na = GQA_Q_HEADS * HEAD_DIM
    w_o1 = jnp.concatenate([wo[0:na], wo[na + POOL_WIDTH:]], axis=0).astype(BF)
    w_o2 = wo[na:na + POOL_WIDTH].astype(BF)

    return {
        "n1_g": norm1_g[layer][None, :],
        "n2_g": norm2_g[layer][None, :],
        "w_in": w_wide,
        "gqa_qg": head_gain(gqa_qn_g[layer]),
        "gqa_kg": head_gain(gqa_kn_g[layer]),
        "mla_qg": mla_qn_g[layer][None, :],
        "mla_kvg": mla_kvn_g[layer][None, :],
        "w_uq": jnp.concatenate(uq_cols, axis=1).astype(BF),
        "w_uk": jnp.concatenate(uk_cols, axis=1).astype(BF),
        "w_uv": jnp.concatenate(uv_cols, axis=1).astype(BF),
        "pool_w": w_bd.astype(BF),
        "pool_scale": pool_scale[layer][None, :],
        "w_o1": w_o1,
        "w_o2": w_o2,
        "peer_wq": peer_wq[layer].reshape(d, PEER_HEADS, PEER_QUERY_DIM).transpose(1, 0, 2).astype(BF),
        "peer_sk": peer_subkeys[layer].astype(BF),
        "peer_u": peer_u[layer].astype(BF),
        "peer_vt": _peer_vt_blocks(peer_v[layer]),
    }


def _tile(l, pref):
    return pref if l % pref == 0 else l


def kernel(x, c, ctx, c_ctx, ada_w, ada_b, norm1_g, norm2_g, w_in, gqa_qn_g, gqa_kn_g, pool_w, pool_scale,
           mla_qn_g, mla_kvn_g, mla_w_uq, mla_w_ukv, w_out, peer_wq, peer_subkeys, peer_u, peer_v, final_g):
    b, l, d = x.shape
    lc = ctx.shape[1]
    depth = ada_w.shape[0]
    assert d == D_MODEL and l % GRID_W == 0
    assert (b * l) % PEER_TOK == 0 and (b * lc) % PEER_TOK == 0

    n_rows = -(-(b + 1) // 8) * 8
    cc = jnp.concatenate([c, c_ctx[None, :], jnp.zeros((n_rows - b - 1, d), F32)], axis=0)
    mod_all = _adaln(cc, ada_w, ada_b).reshape(depth, n_rows, N_MOD, d)

    rope_tabs = _rope_tables(l)
    tl_l, tl_c = _tile(l, 512), _tile(lc, 512)
    tq_l, tq_c = _tile(l, 256), _tile(lc, 256)
    fg = final_g[None, :]

    xl, xc = x, ctx
    for layer in range(depth):
        lw = _layer_weights(layer, norm1_g, norm2_g, w_in, gqa_qn_g, gqa_kn_g, pool_w, pool_scale, mla_qn_g,
                            mla_kvn_g, mla_w_uq, mla_w_ukv, w_out, peer_wq, peer_subkeys, peer_u, peer_v)
        mod = mod_all[layer]
        last = layer == depth - 1
        q_l, k_l, v_l, pb_l = _inproj(xl, mod, None, lw, rope_tabs, tl_l)
        q_c, k_c, v_c, pb_c = _inproj(xc, mod, b, lw, None, tl_c)

        att_l = _attn(q_l, [k_c, k_l], [v_c, v_l], tq_l)
        ob_l = _pool(pb_l, lw["pool_w"], lw["pool_scale"])
        xl_mid, hn_l = _outproj(xl, att_l, ob_l, mod, None, lw, tl_l)
        xl = _peer(hn_l.reshape(b * l, d), xl_mid.reshape(b * l, d), mod, None, l, fg, last, lw).reshape(b, l, d)

        if not last:
            att_c = _attn(q_c, [k_c], [v_c], tq_c)
            ob_c = _pool(pb_c, lw["pool_w"], lw["pool_scale"])
            xc_mid, hn_c = _outproj(xc, att_c, ob_c, mod, b, lw, tl_c)
            xc = _peer(hn_c.reshape(b * lc, d), xc_mid.reshape(b * lc, d), mod, b, lc, fg, False,
                       lw).reshape(b, lc, d)
    return xl
```

```python
import functools

import jax
import jax.numpy as jnp
from jax import lax
from jax.experimental import pallas as pl
from jax.experimental.pallas import tpu as pltpu

D_MODEL = 1024
GRID_W = 64
N_MOD = 6
EPS = 1e-6
ROPE_BASE = 10000.0
HEAD_DIM = 64
GQA_Q_HEADS = 6
GQA_KV_HEADS = 2
GQA_GROUP = GQA_Q_HEADS // GQA_KV_HEADS
POOL_GROUPS = 4
POOL_WINDOWS = (2, 4, 8, 16)
POOL_WIDTH = D_MODEL // 4
POOL_GROUP_DIM = POOL_WIDTH // POOL_GROUPS
MLA_HEADS = 6
MLA_NOPE_DIM = 64
MLA_ROPE_DIM = 32
MLA_V_DIM = 64
MLA_Q_RANK = 384
MLA_KV_RANK = 256
PEER_HEADS = 8
PEER_N_KEYS = 128
PEER_N_EXPERTS = PEER_N_KEYS * PEER_N_KEYS
PEER_TOPK = 16
PEER_QUERY_DIM = 256
PEER_HALF = PEER_QUERY_DIM // 2

LANES = 128
N_ATT_HEADS = GQA_Q_HEADS + MLA_HEADS
QK_WIDTH = N_ATT_HEADS * LANES
V_WIDTH = N_ATT_HEADS * HEAD_DIM
POOL_PAD = 16

_C_QA = 0
_C_KA = _C_QA + GQA_Q_HEADS * LANES
_C_VA = _C_KA + GQA_KV_HEADS * LANES
_C_B = _C_VA + GQA_Q_HEADS * HEAD_DIM
_C_CQ = _C_B + POOL_WIDTH
_C_CKV = _C_CQ + MLA_Q_RANK
_C_KR = _C_CKV + MLA_KV_RANK
IN_WIDE = _C_KR + LANES
ROLL_HALF = LANES // 2

PEER_TOK = 512
PEER_EBLK = 2048
PEER_I1_PER_BLK = PEER_EBLK // PEER_N_KEYS
PEER_CAND_ROWS = 2 * PEER_TOPK + 5 * 8
VMEM_LIMIT = 56 * 1024 * 1024

BF = jnp.bfloat16
F32 = jnp.float32
LOG2_E = 1.4426950408889634
NEG_INF = float("-inf")
POS_INF = float("inf")


def _cparams(sem):
    return pltpu.CompilerParams(dimension_semantics=sem, vmem_limit_bytes=VMEM_LIMIT)


def _dot(a, b):
    return jnp.dot(a, b, preferred_element_type=F32)


def _dot_nt(a, b):
    return lax.dot_general(a, b, (((1,), (1,)), ((), ())), preferred_element_type=F32)


def _split_bf16(a):
    hi = a.astype(BF)
    lo = (a - hi.astype(F32)).astype(BF)
    return hi, lo


def _dot3(a, b):
    ah, al = _split_bf16(a)
    bh, bl = _split_bf16(b)
    return _dot(ah, bh) + _dot(ah, bl) + _dot(al, bh)


def _rms(x):
    return x * lax.rsqrt(jnp.mean(x * x, axis=-1, keepdims=True) + EPS)


def _adaln_kernel(c_ref, w_ref, b_ref, o_ref):
    c = c_ref[...]
    s = c * (1.0 / (1.0 + jnp.exp(-c)))
    o_ref[...] = _dot3(s, w_ref[...]) + b_ref[...]


def _adaln(cc, ada_w, ada_b):
    depth, d, nd = ada_w.shape
    r = cc.shape[0]
    nblk = nd // d
    return pl.pallas_call(
        _adaln_kernel,
        out_shape=jax.ShapeDtypeStruct((depth, r, nd), F32),
        grid=(depth, nblk),
        in_specs=[
            pl.BlockSpec((r, d), lambda l, j: (0, 0)),
            pl.BlockSpec((None, d, d), lambda l, j: (l, 0, j)),
            pl.BlockSpec((None, 1, d), lambda l, j: (l, 0, j)),
        ],
        out_specs=pl.BlockSpec((None, r, d), lambda l, j: (l, 0, j)),
        compiler_params=_cparams(("arbitrary", "arbitrary")),
        name="adaln",
    )(cc, ada_w, ada_b.reshape(depth, 1, nd))


def _rope_block(xb, cb, sb):
    return xb * cb + pltpu.roll(xb, ROLL_HALF, 1) * sb


def _inproj_kernel(*refs, rope):
    if rope:
        (x_ref, mod_ref, n1_ref, w_ref, qg_ref, kg_ref, cqg_ref, ckvg_ref, wuq_ref, wuk_ref, wuv_ref,
         ca_ref, sa_ref, cc_ref, sc_ref, q_ref, k_ref, v_ref, b_ref) = refs
    else:
        (x_ref, mod_ref, n1_ref, w_ref, qg_ref, kg_ref, cqg_ref, ckvg_ref, wuq_ref, wuk_ref, wuv_ref,
         q_ref, k_ref, v_ref, b_ref) = refs
    x = x_ref[...]
    shift = mod_ref[0:1, :]
    scale = mod_ref[1:2, :]
    h = _rms(x) * n1_ref[...] * (1.0 + scale) + shift
    p = _dot(h.astype(BF), w_ref[...])

    sa = HEAD_DIM ** -0.5 * LOG2_E
    sc = (MLA_NOPE_DIM + MLA_ROPE_DIM) ** -0.5 * LOG2_E

    def gqa_head(c0, g_ref):
        blk = p[:, c0:c0 + LANES]
        ms = jnp.sum(blk * blk, axis=-1, keepdims=True) * (1.0 / HEAD_DIM)
        y = blk * lax.rsqrt(ms + EPS) * g_ref[...]
        if rope:
            y = _rope_block(y, ca_ref[...], sa_ref[...])
        return y

    for hh in range(GQA_Q_HEADS):
        q_ref[:, hh * LANES:(hh + 1) * LANES] = (gqa_head(_C_QA + hh * LANES, qg_ref) * sa).astype(BF)
    for g in range(GQA_KV_HEADS):
        kb = gqa_head(_C_KA + g * LANES, kg_ref).astype(BF)
        for hh in range(g * GQA_GROUP, (g + 1) * GQA_GROUP):
            k_ref[:, hh * LANES:(hh + 1) * LANES] = kb
    nva = GQA_Q_HEADS * HEAD_DIM
    v_ref[:, 0:nva] = p[:, _C_VA:_C_VA + nva].astype(BF)
    b_ref[...] = p[:, _C_B:_C_B + POOL_WIDTH]

    cq = _rms(p[:, _C_CQ:_C_CQ + MLA_Q_RANK]) * cqg_ref[...]
    ckv = (_rms(p[:, _C_CKV:_C_CKV + MLA_KV_RANK]) * ckvg_ref[...]).astype(BF)
    qc = _dot(cq.astype(BF), wuq_ref[...])
    kc = _dot(ckv, wuk_ref[...])
    vc = _dot(ckv, wuv_ref[...])
    kr = p[:, _C_KR:_C_KR + LANES]
    if rope:
        kr = _rope_block(kr, cc_ref[...], sc_ref[...])
    base = GQA_Q_HEADS * LANES
    for hh in range(MLA_HEADS):
        qb = qc[:, hh * LANES:(hh + 1) * LANES]
        if rope:
            qb = _rope_block(qb, cc_ref[...], sc_ref[...])
        q_ref[:, base + hh * LANES:base + (hh + 1) * LANES] = (qb * sc).astype(BF)
        k_ref[:, base + hh * LANES:base + (hh + 1) * LANES] = (kc[:, hh * LANES:(hh + 1) * LANES] + kr).astype(BF)
    v_ref[:, nva:V_WIDTH] = vc.astype(BF)


def _inproj(x, mod, mod_row0, lw, rope_tabs, tl):
    b, l, d = x.shape
    rope = rope_tabs is not None
    grid = (b, l // tl)

    def full(a):
        nd = a.ndim
        return pl.BlockSpec(a.shape, lambda i, j: (0,) * nd)

    if mod_row0 is None:
        mod_spec = pl.BlockSpec((None, N_MOD, d), lambda i, j: (i, 0, 0))
    else:
        mod_spec = pl.BlockSpec((None, N_MOD, d), lambda i, j: (mod_row0, 0, 0))
    weights = [lw["n1_g"], lw["w_in"], lw["gqa_qg"], lw["gqa_kg"], lw["mla_qg"], lw["mla_kvg"],
               lw["w_uq"], lw["w_uk"], lw["w_uv"]]
    in_specs = [pl.BlockSpec((None, tl, d), lambda i, j: (i, j, 0)), mod_spec] + [full(a) for a in weights]
    args = [x, mod] + weights
    if rope:
        in_specs += [pl.BlockSpec((tl, LANES), lambda i, j: (j, 0)) for _ in range(4)]
        args += list(rope_tabs)
    def rows(w, dt):
        return jax.ShapeDtypeStruct((b, l, w), dt), pl.BlockSpec((None, tl, w), lambda i, j: (i, j, 0))

    outs = [rows(QK_WIDTH, BF), rows(QK_WIDTH, BF), rows(V_WIDTH, BF), rows(POOL_WIDTH, F32)]
    return pl.pallas_call(
        functools.partial(_inproj_kernel, rope=rope),
        out_shape=[o[0] for o in outs],
        grid=grid,
        in_specs=in_specs,
        out_specs=[o[1] for o in outs],
        compiler_params=_cparams(("parallel", "parallel")),
        name="inproj",
    )(*args)


def _attn_kernel(*refs, n_src):
    q_ref = refs[0]
    k_refs = refs[1:1 + n_src]
    v_refs = refs[1 + n_src:1 + 2 * n_src]
    o_ref = refs[1 + 2 * n_src]
    tq = q_ref.shape[0]
    lane = lax.broadcasted_iota(jnp.int32, (tq, LANES), 1)
    for jb in range(N_ATT_HEADS // 2):
        halves = []
        for n in (2 * jb, 2 * jb + 1):
            q = q_ref[:, n * LANES:(n + 1) * LANES]
            ss = [_dot_nt(q, k_ref[:, n * LANES:(n + 1) * LANES]) for k_ref in k_refs]
            m = ss[0].max(axis=-1, keepdims=True)
            for s in ss[1:]:
                m = jnp.maximum(m, s.max(axis=-1, keepdims=True))
            acc = None
            den = None
            for s, v_ref in zip(ss, v_refs):
                e = jnp.exp2(s - m)
                dsum = e.sum(axis=-1, keepdims=True)
                pv = _dot(e.astype(BF), v_ref[:, jb * LANES:(jb + 1) * LANES])
                acc = pv if acc is None else acc + pv
                den = dsum if den is None else den + dsum
            halves.append(acc * (1.0 / den))
        o = jnp.where(lane < HEAD_DIM, halves[0], halves[1])
        o_ref[:, jb * LANES:(jb + 1) * LANES] = o.astype(BF)


def _attn(q, ks, vs, tq):
    b, l, _ = q.shape
    n_src = len(ks)
    in_specs = [pl.BlockSpec((None, tq, QK_WIDTH), lambda i, j: (i, j, 0))]
    in_specs += [pl.BlockSpec((None, k.shape[1], QK_WIDTH), lambda i, j: (i, 0, 0)) for k in ks]
    in_specs += [pl.BlockSpec((None, v.shape[1], V_WIDTH), lambda i, j: (i, 0, 0)) for v in vs]
    return pl.pallas_call(
        functools.partial(_attn_kernel, n_src=n_src),
        out_shape=jax.ShapeDtypeStruct((b, l, V_WIDTH), BF),
        grid=(b, l // tq),
        in_specs=in_specs,
        out_specs=pl.BlockSpec((None, tq, V_WIDTH), lambda i, j: (i, j, 0)),
        compiler_params=_cparams(("parallel", "arbitrary")),
        name="attn",
    )(q, *ks, *vs)


def _pool_kernel(b_ref, w_ref, s_ref, o_ref, xp_ref):
    l = b_ref.shape[0]
    x = b_ref[...]
    zeros = jnp.zeros((POOL_PAD, POOL_WIDTH), F32)
    xp_ref[0:POOL_PAD, :] = zeros
    xp_ref[POOL_PAD + l:POOL_PAD + l + POOL_PAD, :] = zeros
    xp_ref[POOL_PAD:POOL_PAD + l, :] = x
    t = lax.broadcasted_iota(jnp.int32, (l, LANES), 0)
    lane = lax.broadcasted_iota(jnp.int32, (l, LANES), 1)
    outs = []
    for half in range(POOL_WIDTH // LANES):
        w_small = POOL_WINDOWS[2 * half]
        w_big = POOL_WINDOWS[2 * half + 1]
        lo, hi = half * LANES, (half + 1) * LANES

        def win(j):
            return xp_ref[pl.ds(POOL_PAD + j, l), lo:hi]

        s_small = None
        for j in range(-(w_small // 2), w_small // 2):
            s_small = win(j) if s_small is None else s_small + win(j)
        s_big = s_small
        for j in range(-(w_big // 2), w_big // 2):
            if not (-(w_small // 2) <= j < w_small // 2):
                s_big = s_big + win(j)

        def cnt(w):
            lo_i = jnp.maximum(t - w // 2, 0)
            hi_i = jnp.minimum(t - w // 2 + w, l)
            return (hi_i - lo_i).astype(F32)

        left = lane < POOL_GROUP_DIM
        s = jnp.where(left, s_small, s_big)
        c = jnp.where(left, cnt(w_small), cnt(w_big))
        outs.append(s / c - x[:, lo:hi])
    pooled = jnp.concatenate(outs, axis=1).astype(BF)
    o_ref[...] = (_dot(pooled, w_ref[...]) * s_ref[...]).astype(BF)


def _pool(pb, w_bd, pscale):
    b, l, w = pb.shape
    return pl.pallas_call(
        _pool_kernel,
        out_shape=jax.ShapeDtypeStruct((b, l, w), BF),
        grid=(b,),
        in_specs=[
            pl.BlockSpec((None, l, w), lambda i: (i, 0, 0)),
            pl.BlockSpec((w, w), lambda i: (0, 0)),
            pl.BlockSpec((1, w), lambda i: (0, 0)),
        ],
        out_specs=pl.BlockSpec((None, l, w), lambda i: (i, 0, 0)),
        scratch_shapes=[pltpu.VMEM((l + 2 * POOL_PAD, w), F32)],
        compiler_params=_cparams(("parallel",)),
        name="pool",
    )(pb, w_bd, pscale)


def _outproj_kernel(x_ref, att_ref, ob_ref, mod_ref, n2_ref, w1_ref, w2_ref, xo_ref, h_ref):
    y = _dot(att_ref[...], w1_ref[...]) + _dot(ob_ref[...], w2_ref[...])
    xn = x_ref[...] + mod_ref[2:3, :] * y
    xo_ref[...] = xn
    h = _rms(xn) * n2_ref[...] * (1.0 + mod_ref[4:5, :]) + mod_ref[3:4, :]
    h_ref[...] = h.astype(BF)


def _outproj(x, att, ob, mod, mod_row0, lw, tl):
    b, l, d = x.shape
    if mod_row0 is None:
        mod_spec = pl.BlockSpec((None, N_MOD, d), lambda i, j: (i, 0, 0))
    else:
        mod_spec = pl.BlockSpec((None, N_MOD, d), lambda i, j: (mod_row0, 0, 0))

    def tile(w):
        return pl.BlockSpec((None, tl, w), lambda i, j: (i, j, 0))

    def full(a):
        return pl.BlockSpec(a.shape, lambda i, j: (0, 0))

    return pl.pallas_call(
        _outproj_kernel,
        out_shape=[jax.ShapeDtypeStruct((b, l, d), F32), jax.ShapeDtypeStruct((b, l, d), BF)],
        grid=(b, l // tl),
        in_specs=[tile(d), tile(V_WIDTH), tile(POOL_WIDTH), mod_spec, full(lw["n2_g"]), full(lw["w_o1"]),
                  full(lw["w_o2"])],
        out_specs=[tile(d), tile(d)],
        compiler_params=_cparams(("parallel", "parallel")),
        name="outproj",
    )(x, att, ob, mod, lw["n2_g"], lw["w_o1"], lw["w_o2"])


def _gelu_tanh(x):
    return 0.5 * x * (1.0 + jnp.tanh(0.7978845608028654 * (x + 0.044715 * (x * x * x))))


def _peer_prologue(hn_ref, wq_ref, sk_ref, c1_ref, e1_ref, r2_ref, e2_ref, s_ref, v_ref, cand_ref):
    t = hn_ref.shape[0]
    hn = hn_ref[...]

    sub8 = lax.broadcasted_iota(jnp.int32, (8, t), 0)
    sub16 = lax.broadcasted_iota(jnp.int32, (PEER_TOPK, t), 0)
    zeros8 = jnp.zeros((8, t), F32)

    def head(h, carry):
        qh = _dot(hn, wq_ref[h]).astype(BF)
        r2 = None
        for p in range(2):
            st = _dot_nt(sk_ref[h, p], qh[:, p * PEER_HALF:(p + 1) * PEER_HALF])
            s_ref[p] = st
            m = st.max(axis=0, keepdims=True)
            v_ref[p, 0:1, :] = m
            rank = jnp.zeros((PEER_N_KEYS, t), F32)
            for k in range(1, PEER_TOPK):
                below = st < m
                if p == 1:
                    rank = rank + jnp.where(below, 1.0, 0.0)
                m = jnp.where(below, st, NEG_INF).max(axis=0, keepdims=True)
                v_ref[p, k:k + 1, :] = m
            if p == 1:
                r2 = rank + jnp.where(st < m, 1.0, 0.0)
        v1 = v_ref[0]
        v2 = v_ref[1]

        groups = [
            v1[0:1, :] + v2,
            jnp.where(sub16 >= 1, v1 + v2[0:1, :], NEG_INF),
            jnp.where(sub8 >= 1, v1[1:2, :] + v2[0:8, :], NEG_INF),
            jnp.where(sub8 >= 2, v1[0:8, :] + v2[1:2, :], NEG_INF),
            jnp.where((sub8 >= 2) & (sub8 <= 4), v1[2:3, :] + v2[0:8, :], NEG_INF),
            jnp.where((sub8 >= 3) & (sub8 <= 4), v1[0:8, :] + v2[2:3, :], NEG_INF),
            jnp.where(sub8 == 3, v1[3:4, :] + v2[0:8, :], NEG_INF),
        ]
        cand_ref[...] = jnp.concatenate(groups, axis=0)
        c00 = v1[0:1, :] + v2[0:1, :]

        def below_max(k, m):
            c = cand_ref[...]
            return jnp.where(c < m, c, NEG_INF).max(axis=0, keepdims=True)

        tau = lax.fori_loop(1, PEER_TOPK, below_max, c00)
        c = cand_ref[...]
        sel = c >= tau
        z = jnp.where(sel, jnp.exp(c - c00), 0.0).sum(axis=0, keepdims=True)
        ind = jnp.where(sel, 1.0, 0.0)

        def row_total(r0, rows, a):
            return jnp.where(sub16 == a, ind[r0:r0 + rows, :].sum(axis=0, keepdims=True), 0.0)

        cnt = (ind[16:32, :] + jnp.concatenate([ind[40:48, :] + ind[56:64, :], zeros8], axis=0)
               + row_total(0, 16, 0) + row_total(32, 8, 1) + row_total(48, 8, 2) + row_total(64, 8, 3))
        s1 = s_ref[0]
        s2 = s_ref[1]
        c1 = jnp.zeros((PEER_N_KEYS, t), F32)
        for a in range(PEER_TOPK):
            c1 = jnp.where(s1 == v1[a:a + 1, :], cnt[a:a + 1, :], c1)
        c1_ref[h] = c1
        r2_ref[h] = r2.astype(BF)
        e1_ref[h] = jnp.exp(s1 - v1[0:1, :]) * (1.0 / z)
        e2_ref[h] = jnp.exp(s2 - v2[0:1, :]).astype(BF)
        return carry

    lax.fori_loop(0, PEER_HEADS, head, 0)


def _bcast_rows_bf16(row):
    tile = jnp.broadcast_to(row, (16, LANES)).astype(BF)
    return jnp.concatenate([tile] * (PEER_N_KEYS // 16), axis=0)


def _peer_dense(at_ref, wt_ref, blk, c1_ref, e1_ref, r2_ref, e2_ref):
    t = at_ref.shape[1]
    i1_0 = pl.multiple_of(blk * PEER_I1_PER_BLK, PEER_I1_PER_BLK)
    for lg in range(t // LANES):
        ls = slice(lg * LANES, (lg + 1) * LANES)
        c1_s = [c1_ref[h, pl.ds(i1_0, PEER_I1_PER_BLK), ls] for h in range(PEER_HEADS)]
        e1_s = [e1_ref[h, pl.ds(i1_0, PEER_I1_PER_BLK), ls] for h in range(PEER_HEADS)]
        for j in range(PEER_I1_PER_BLK):
            rs = slice(j * PEER_N_KEYS, (j + 1) * PEER_N_KEYS)
            g = jnp.zeros((PEER_N_KEYS, LANES), BF)
            for h in range(PEER_HEADS):
                c1 = _bcast_rows_bf16(c1_s[h][j:j + 1, :])
                e1 = _bcast_rows_bf16(e1_s[h][j:j + 1, :])
                g = g + e1 * jnp.where(r2_ref[h, :, ls] < c1, e2_ref[h, :, ls], jnp.zeros((), BF))
            wt_ref[rs, ls] = _gelu_tanh(at_ref[rs, ls]).astype(BF) * g


def _peer_kernel(hn_ref, x_ref, mod_ref, fg_ref, wq_ref, sk_ref, u_ref, vt_ref, o_ref,
                 acc_ref, at_ref, wt_ref, c1_ref, e1_ref, r2_ref, e2_ref, s_ref, v_ref, cand_ref, *, final):
    e = pl.program_id(1)

    @pl.when(e == 0)
    def _():
        _peer_prologue(hn_ref, wq_ref, sk_ref, c1_ref, e1_ref, r2_ref, e2_ref, s_ref, v_ref, cand_ref)
        acc_ref[...] = jnp.zeros_like(acc_ref)

    at_ref[...] = _dot_nt(u_ref[...], hn_ref[...])
    _peer_dense(at_ref, wt_ref, e, c1_ref, e1_ref, r2_ref, e2_ref)
    acc_ref[...] += _dot(vt_ref[...], wt_ref[...])

    @pl.when(e == pl.num_programs(1) - 1)
    def _():
        y = x_ref[...] + mod_ref[5:6, :] * acc_ref[...].T
        if final:
            y = _rms(y) * fg_ref[...]
        o_ref[...] = y


def _peer(hn, x_mid, mod, mod_row0, tok_per_row, final_g, final, lw):
    ntok, d = hn.shape
    t = PEER_TOK
    n_blk = PEER_N_EXPERTS // PEER_EBLK
    if mod_row0 is None:
        assert tok_per_row % t == 0
        tiles_per_row = tok_per_row // t
        mod_spec = pl.BlockSpec((None, N_MOD, d), lambda i, e: (i // tiles_per_row, 0, 0))
    else:
        mod_spec = pl.BlockSpec((None, N_MOD, d), lambda i, e: (mod_row0, 0, 0))
    tab_f = pltpu.VMEM((PEER_HEADS, PEER_N_KEYS, t), F32)
    tab_b = pltpu.VMEM((PEER_HEADS, PEER_N_KEYS, t), BF)
    return pl.pallas_call(
        functools.partial(_peer_kernel, final=final),
        out_shape=jax.ShapeDtypeStruct((ntok, d), F32),
        grid=(ntok // t, n_blk),
        in_specs=[
            pl.BlockSpec((t, d), lambda i, e: (i, 0)),
            pl.BlockSpec((t, d), lambda i, e: (i, 0)),
            mod_spec,
            pl.BlockSpec((1, d), lambda i, e: (0, 0)),
            pl.BlockSpec(lw["peer_wq"].shape, lambda i, e: (0, 0, 0)),
            pl.BlockSpec(lw["peer_sk"].shape, lambda i, e: (0, 0, 0, 0)),
            pl.BlockSpec((PEER_EBLK, d), lambda i, e: (e, 0)),
            pl.BlockSpec((None, d, PEER_EBLK), lambda i, e: (e, 0, 0)),
        ],
        out_specs=pl.BlockSpec((t, d), lambda i, e: (i, 0)),
        scratch_shapes=[
            pltpu.VMEM((d, t), F32),
            pltpu.VMEM((PEER_EBLK, t), F32),
            pltpu.VMEM((PEER_EBLK, t), BF),
            tab_f, tab_f, tab_b, tab_b,
            pltpu.VMEM((2, PEER_N_KEYS, t), F32),
            pltpu.VMEM((2, PEER_TOPK, t), F32),
            pltpu.VMEM((PEER_CAND_ROWS, t), F32),
        ],
        compiler_params=_cparams(("parallel", "arbitrary")),
        name="peer",
    )(hn, x_mid, mod, final_g, lw["peer_wq"], lw["peer_sk"], lw["peer_u"], lw["peer_vt"])


def _rope_tables(l):
    rows = l // GRID_W
    row = jnp.repeat(jnp.arange(rows), GRID_W).astype(F32)
    col = jnp.tile(jnp.arange(GRID_W), rows).astype(F32)

    def cs(rot_dim):
        n = rot_dim // 4
        inv = ROPE_BASE ** (-jnp.arange(n, dtype=F32) / n)
        ang = jnp.concatenate([row[:, None] * inv, col[:, None] * inv], axis=-1)
        return jnp.cos(ang), jnp.sin(ang)

    ones = lambda w: jnp.ones((l, w), F32)
    zeros = lambda w: jnp.zeros((l, w), F32)
    def block_tables(c, s):
        pad = ROLL_HALF - c.shape[1]
        return (jnp.concatenate([c, ones(pad), c, ones(pad)], axis=1),
                jnp.concatenate([-s, zeros(pad), s, zeros(pad)], axis=1))

    cos_a, sin_a = block_tables(*cs(HEAD_DIM))
    cos_c, sin_c = block_tables(*cs(MLA_ROPE_DIM))
    return cos_a, sin_a, cos_c, sin_c


def _peer_vt_blocks(v_tab):
    n, d = v_tab.shape
    return v_tab.astype(BF).reshape(n // PEER_EBLK, PEER_EBLK, d).transpose(0, 2, 1)


def _head_block(rot, nope):
    rows = (rot if rot is not None else nope).shape[0]
    n = 0 if rot is None else rot.shape[1] // 2
    m = 0 if nope is None else nope.shape[1]
    low = min(m, ROLL_HALF - n)
    parts = []
    if n:
        parts.append(rot[:, :n])
    parts.append(nope[:, :low] if m else jnp.zeros((rows, 0), F32))
    parts.append(jnp.zeros((rows, ROLL_HALF - n - low), F32))
    if n:
        parts.append(rot[:, n:])
    parts.append(nope[:, low:] if m else jnp.zeros((rows, 0), F32))
    parts.append(jnp.zeros((rows, ROLL_HALF - n - (m - low)), F32))
    blk = jnp.concatenate(parts, axis=1)
    assert blk.shape == (rows, LANES)
    return blk


def _layer_weights(layer, norm1_g, norm2_g, w_in, gqa_qn_g, gqa_kn_g, pool_w, pool_scale, mla_qn_g, mla_kvn_g,
                   mla_w_uq, mla_w_ukv, w_out, peer_wq, peer_subkeys, peer_u, peer_v):
    d = D_MODEL
    w = w_in[layer]
    o = 0
    aq = w[:, o:o + GQA_Q_HEADS * HEAD_DIM]; o += GQA_Q_HEADS * HEAD_DIM
    ak = w[:, o:o + GQA_KV_HEADS * HEAD_DIM]; o += GQA_KV_HEADS * HEAD_DIM
    av = w[:, o:o + GQA_KV_HEADS * HEAD_DIM]; o += GQA_KV_HEADS * HEAD_DIM
    wb = w[:, o:o + POOL_WIDTH]; o += POOL_WIDTH
    wcq = w[:, o:o + MLA_Q_RANK]; o += MLA_Q_RANK
    wckv = w[:, o:o + MLA_KV_RANK]; o += MLA_KV_RANK
    wkr = w[:, o:o + MLA_ROPE_DIM]
    cols = []
    for h in range(GQA_Q_HEADS):
        cols.append(_head_block(aq[:, h * HEAD_DIM:(h + 1) * HEAD_DIM], None))
    for g in range(GQA_KV_HEADS):
        cols.append(_head_block(ak[:, g * HEAD_DIM:(g + 1) * HEAD_DIM], None))
    for h in range(GQA_Q_HEADS):
        g = h // GQA_GROUP
        cols.append(av[:, g * HEAD_DIM:(g + 1) * HEAD_DIM])
    cols += [wb, wcq, wckv]
    kr_blk = _head_block(wkr, jnp.zeros((d, MLA_NOPE_DIM), F32))
    cols.append(kr_blk)
    w_wide = jnp.concatenate(cols, axis=1).astype(BF)
    assert w_wide.shape == (d, IN_WIDE)

    def head_gain(g):
        return _head_block(g[None, :], None)

    qd = MLA_NOPE_DIM + MLA_ROPE_DIM
    uq = mla_w_uq[layer]
    ukv = mla_w_ukv[layer]
    uq_cols, uk_cols, uv_cols = [], [], []
    zero_rot = jnp.zeros((MLA_KV_RANK, MLA_ROPE_DIM), F32)
    for h in range(MLA_HEADS):
        q_h = uq[:, h * qd:(h + 1) * qd]
        uq_cols.append(_head_block(q_h[:, MLA_NOPE_DIM:], q_h[:, :MLA_NOPE_DIM]))
        k0 = h * (MLA_NOPE_DIM + MLA_V_DIM)
        uk_cols.append(_head_block(zero_rot, ukv[:, k0:k0 + MLA_NOPE_DIM]))
        uv_cols += [ukv[:, k0 + MLA_NOPE_DIM:k0 + MLA_NOPE_DIM + MLA_V_DIM]]

    pw = pool_w[layer]
    w_bd = jnp.zeros((POOL_WIDTH, POOL_WIDTH), F32)
    for g in range(POOL_GROUPS):
        s = slice(g * POOL_GROUP_DIM, (g + 1) * POOL_GROUP_DIM)
        w_bd = w_bd.at[s, s].set(pw[g])

    wo = w_out[layer]
    na = GQA_Q_HEADS * HEAD_DIM
    w_o1 = jnp.concatenate([wo[0:na], wo[na + POOL_WIDTH:]], axis=0).astype(BF)
    w_o2 = wo[na:na + POOL_WIDTH].astype(BF)

    return {
        "n1_g": norm1_g[layer][None, :],
        "n2_g": norm2_g[layer][None, :],
        "w_in": w_wide,
        "gqa_qg": head_gain(gqa_qn_g[layer]),
        "gqa_kg": head_gain(gqa_kn_g[layer]),
        "mla_qg": mla_qn_g[layer][None, :],
        "mla_kvg": mla_kvn_g[layer][None, :],
        "w_uq": jnp.concatenate(uq_cols, axis=1).astype(BF),
        "w_uk": jnp.concatenate(uk_cols, axis=1).astype(BF),
        "w_uv": jnp.concatenate(uv_cols, axis=1).astype(BF),
        "pool_w": w_bd.astype(BF),
        "pool_scale": pool_scale[layer][None, :],
        "w_o1": w_o1,
        "w_o2": w_o2,
        "peer_wq": peer_wq[layer].reshape(d, PEER_HEADS, PEER_QUERY_DIM).transpose(1, 0, 2).astype(BF),
        "peer_sk": peer_subkeys[layer].astype(BF),
        "peer_u": peer_u[layer].astype(BF),
        "peer_vt": _peer_vt_blocks(peer_v[layer]),
    }


def _tile(l, pref):
    return pref if l % pref == 0 else l


def kernel(x, c, ctx, c_ctx, ada_w, ada_b, norm1_g, norm2_g, w_in, gqa_qn_g, gqa_kn_g, pool_w, pool_scale,
           mla_qn_g, mla_kvn_g, mla_w_uq, mla_w_ukv, w_out, peer_wq, peer_subkeys, peer_u, peer_v, final_g):
    b, l, d = x.shape
    lc = ctx.shape[1]
    depth = ada_w.shape[0]
    assert d == D_MODEL and l % GRID_W == 0
    assert (b * l) % PEER_TOK == 0 and (b * lc) % PEER_TOK == 0

    n_rows = -(-(b + 1) // 8) * 8
    cc = jnp.concatenate([c, c_ctx[None, :], jnp.zeros((n_rows - b - 1, d), F32)], axis=0)
    mod_all = _adaln(cc, ada_w, ada_b).reshape(depth, n_rows, N_MOD, d)

    rope_tabs = _rope_tables(l)
    tl_l, tl_c = _tile(l, 512), _tile(lc, 512)
    tq_l, tq_c = _tile(l, 256), _tile(lc, 256)
    fg = final_g[None, :]

    xl, xc = x, ctx
    for layer in range(depth):
        lw = _layer_weights(layer, norm1_g, norm2_g, w_in, gqa_qn_g, gqa_kn_g, pool_w, pool_scale, mla_qn_g,
                            mla_kvn_g, mla_w_uq, mla_w_ukv, w_out, peer_wq, peer_subkeys, peer_u, peer_v)
        mod = mod_all[layer]
        last = layer == depth - 1
        q_l, k_l, v_l, pb_l = _inproj(xl, mod, None, lw, rope_tabs, tl_l)
        q_c, k_c, v_c, pb_c = _inproj(xc, mod, b, lw, None, tl_c)

        att_l = _attn(q_l, [k_c, k_l], [v_c, v_l], tq_l)
        ob_l = _pool(pb_l, lw["pool_w"], lw["pool_scale"])
        xl_mid, hn_l = _outproj(xl, att_l, ob_l, mod, None, lw, tl_l)
        xl = _peer(hn_l.reshape(b * l, d), xl_mid.reshape(b * l, d), mod, None, l, fg, last, lw).reshape(b, l, d)

        if not last:
            att_c = _attn(q_c, [k_c], [v_c], tq_c)
            ob_c = _pool(pb_c, lw["pool_w"], lw["pool_scale"])
            xc_mid, hn_c = _outproj(xc, att_c, ob_c, mod, b, lw, tl_c)
            xc = _peer(hn_c.reshape(b * lc, d), xc_mid.reshape(b * lc, d), mod, b, lc, fg, False,
                       lw).reshape(b, lc, d)
    return xl
```

```python
import functools

import jax
import jax.numpy as jnp
from jax import lax
from jax.experimental import pallas as pl
from jax.experimental.pallas import tpu as pltpu

D_MODEL = 1024
GRID_W = 64
N_MOD = 6
EPS = 1e-6
ROPE_BASE = 10000.0
HEAD_DIM = 64
GQA_Q_HEADS = 6
GQA_KV_HEADS = 2
GQA_GROUP = GQA_Q_HEADS // GQA_KV_HEADS
POOL_GROUPS = 4
POOL_WINDOWS = (2, 4, 8, 16)
POOL_WIDTH = D_MODEL // 4
POOL_GROUP_DIM = POOL_WIDTH // POOL_GROUPS
MLA_HEADS = 6
MLA_NOPE_DIM = 64
MLA_ROPE_DIM = 32
MLA_V_DIM = 64
MLA_Q_RANK = 384
MLA_KV_RANK = 256
PEER_HEADS = 8
PEER_N_KEYS = 128
PEER_N_EXPERTS = PEER_N_KEYS * PEER_N_KEYS
PEER_TOPK = 16
PEER_QUERY_DIM = 256
PEER_HALF = PEER_QUERY_DIM // 2

LANES = 128
N_ATT_HEADS = GQA_Q_HEADS + MLA_HEADS
QK_WIDTH = N_ATT_HEADS * LANES
V_WIDTH = N_ATT_HEADS * HEAD_DIM
POOL_PAD = 16

_C_QA = 0
_C_KA = _C_QA + GQA_Q_HEADS * LANES
_C_VA = _C_KA + GQA_KV_HEADS * LANES
_C_B = _C_VA + GQA_Q_HEADS * HEAD_DIM
_C_CQ = _C_B + POOL_WIDTH
_C_CKV = _C_CQ + MLA_Q_RANK
_C_KR = _C_CKV + MLA_KV_RANK
IN_WIDE = _C_KR + LANES
ROLL_HALF = LANES // 2

PEER_TOK = 512
PEER_EBLK = 2048
PEER_I1_PER_BLK = PEER_EBLK // PEER_N_KEYS
PEER_CAND_ROWS = 2 * PEER_TOPK + 5 * 8
VMEM_LIMIT = 56 * 1024 * 1024

BF = jnp.bfloat16
F32 = jnp.float32
LOG2_E = 1.4426950408889634
NEG_INF = float("-inf")


def _cparams(sem):
    return pltpu.CompilerParams(dimension_semantics=sem, vmem_limit_bytes=VMEM_LIMIT)


def _dot(a, b):
    return jnp.dot(a, b, preferred_element_type=F32)


def _dot_nt(a, b):
    return lax.dot_general(a, b, (((1,), (1,)), ((), ())), preferred_element_type=F32)


def _split_bf16(a):
    hi = a.astype(BF)
    lo = (a - hi.astype(F32)).astype(BF)
    return hi, lo


def _dot3(a, b):
    ah, al = _split_bf16(a)
    bh, bl = _split_bf16(b)
    return _dot(ah, bh) + _dot(ah, bl) + _dot(al, bh)


def _rms(x):
    return x * lax.rsqrt(jnp.mean(x * x, axis=-1, keepdims=True) + EPS)


def _adaln_kernel(c_ref, w_ref, b_ref, o_ref):
    c = c_ref[...]
    s = c * (1.0 / (1.0 + jnp.exp(-c)))
    o_ref[...] = _dot3(s, w_ref[...]) + b_ref[...]


def _adaln(cc, ada_w, ada_b):
    depth, d, nd = ada_w.shape
    r = cc.shape[0]
    nblk = nd // d
    return pl.pallas_call(
        _adaln_kernel,
        out_shape=jax.ShapeDtypeStruct((depth, r, nd), F32),
        grid=(depth, nblk),
        in_specs=[
            pl.BlockSpec((r, d), lambda l, j: (0, 0)),
            pl.BlockSpec((None, d, d), lambda l, j: (l, 0, j)),
            pl.BlockSpec((None, 1, d), lambda l, j: (l, 0, j)),
        ],
        out_specs=pl.BlockSpec((None, r, d), lambda l, j: (l, 0, j)),
        compiler_params=_cparams(("arbitrary", "arbitrary")),
        name="adaln",
    )(cc, ada_w, ada_b.reshape(depth, 1, nd))


def _rope_block(xb, cb, sb):
    return xb * cb + pltpu.roll(xb, ROLL_HALF, 1) * sb


def _inproj_kernel(*refs, rope):
    if rope:
        (x_ref, mod_ref, n1_ref, w_ref, qg_ref, kg_ref, cqg_ref, ckvg_ref, wuq_ref, wuk_ref, wuv_ref,
         ca_ref, sa_ref, cc_ref, sc_ref, q_ref, k_ref, v_ref, b_ref) = refs
    else:
        (x_ref, mod_ref, n1_ref, w_ref, qg_ref, kg_ref, cqg_ref, ckvg_ref, wuq_ref, wuk_ref, wuv_ref,
         q_ref, k_ref, v_ref, b_ref) = refs
    x = x_ref[...]
    shift = mod_ref[0:1, :]
    scale = mod_ref[1:2, :]
    h = _rms(x) * n1_ref[...] * (1.0 + scale) + shift
    p = _dot(h.astype(BF), w_ref[...])

    sa = HEAD_DIM ** -0.5 * LOG2_E
    sc = (MLA_NOPE_DIM + MLA_ROPE_DIM) ** -0.5 * LOG2_E

    def gqa_head(c0, g_ref):
        blk = p[:, c0:c0 + LANES]
        ms = jnp.sum(blk * blk, axis=-1, keepdims=True) * (1.0 / HEAD_DIM)
        y = blk * lax.rsqrt(ms + EPS) * g_ref[...]
        if rope:
            y = _rope_block(y, ca_ref[...], sa_ref[...])
        return y

    for hh in range(GQA_Q_HEADS):
        q_ref[:, hh * LANES:(hh + 1) * LANES] = (gqa_head(_C_QA + hh * LANES, qg_ref) * sa).astype(BF)
    for g in range(GQA_KV_HEADS):
        kb = gqa_head(_C_KA + g * LANES, kg_ref).astype(BF)
        for hh in range(g * GQA_GROUP, (g + 1) * GQA_GROUP):
            k_ref[:, hh * LANES:(hh + 1) * LANES] = kb
    nva = GQA_Q_HEADS * HEAD_DIM
    v_ref[:, 0:nva] = p[:, _C_VA:_C_VA + nva].astype(BF)
    b_ref[...] = p[:, _C_B:_C_B + POOL_WIDTH]

    cq = _rms(p[:, _C_CQ:_C_CQ + MLA_Q_RANK]) * cqg_ref[...]
    ckv = (_rms(p[:, _C_CKV:_C_CKV + MLA_KV_RANK]) * ckvg_ref[...]).astype(BF)
    qc = _dot(cq.astype(BF), wuq_ref[...])
    kc = _dot(ckv, wuk_ref[...])
    vc = _dot(ckv, wuv_ref[...])
    kr = p[:, _C_KR:_C_KR + LANES]
    if rope:
        kr = _rope_block(kr, cc_ref[...], sc_ref[...])
    base = GQA_Q_HEADS * LANES
    for hh in range(MLA_HEADS):
        qb = qc[:, hh * LANES:(hh + 1) * LANES]
        if rope:
            qb = _rope_block(qb, cc_ref[...], sc_ref[...])
        q_ref[:, base + hh * LANES:base + (hh + 1) * LANES] = (qb * sc).astype(BF)
        k_ref[:, base + hh * LANES:base + (hh + 1) * LANES] = (kc[:, hh * LANES:(hh + 1) * LANES] + kr).astype(BF)
    v_ref[:, nva:V_WIDTH] = vc.astype(BF)


def _inproj(x, mod, mod_row0, lw, rope_tabs, tl):
    b, l, d = x.shape
    rope = rope_tabs is not None
    grid = (b, l // tl)

    def full(a):
        nd = a.ndim
        return pl.BlockSpec(a.shape, lambda i, j: (0,) * nd)

    if mod_row0 is None:
        mod_spec = pl.BlockSpec((None, N_MOD, d), lambda i, j: (i, 0, 0))
    else:
        mod_spec = pl.BlockSpec((None, N_MOD, d), lambda i, j: (mod_row0, 0, 0))
    weights = [lw["n1_g"], lw["w_in"], lw["gqa_qg"], lw["gqa_kg"], lw["mla_qg"], lw["mla_kvg"],
               lw["w_uq"], lw["w_uk"], lw["w_uv"]]
    in_specs = [pl.BlockSpec((None, tl, d), lambda i, j: (i, j, 0)), mod_spec] + [full(a) for a in weights]
    args = [x, mod] + weights
    if rope:
        in_specs += [pl.BlockSpec((tl, LANES), lambda i, j: (j, 0)) for _ in range(4)]
        args += list(rope_tabs)
    def rows(w, dt):
        return jax.ShapeDtypeStruct((b, l, w), dt), pl.BlockSpec((None, tl, w), lambda i, j: (i, j, 0))

    outs = [rows(QK_WIDTH, BF), rows(QK_WIDTH, BF), rows(V_WIDTH, BF), rows(POOL_WIDTH, F32)]
    return pl.pallas_call(
        functools.partial(_inproj_kernel, rope=rope),
        out_shape=[o[0] for o in outs],
        grid=grid,
        in_specs=in_specs,
        out_specs=[o[1] for o in outs],
        compiler_params=_cparams(("parallel", "parallel")),
        name="inproj",
    )(*args)


def _attn_kernel(*refs, n_src):
    q_ref = refs[0]
    k_refs = refs[1:1 + n_src]
    v_refs = refs[1 + n_src:1 + 2 * n_src]
    o_ref = refs[1 + 2 * n_src]
    tq = q_ref.shape[0]
    lane = lax.broadcasted_iota(jnp.int32, (tq, LANES), 1)
    for jb in range(N_ATT_HEADS // 2):
        halves = []
        for n in (2 * jb, 2 * jb + 1):
            q = q_ref[:, n * LANES:(n + 1) * LANES]
            ss = [_dot_nt(q, k_ref[:, n * LANES:(n + 1) * LANES]) for k_ref in k_refs]
            m = ss[0].max(axis=-1, keepdims=True)
            for s in ss[1:]:
                m = jnp.maximum(m, s.max(axis=-1, keepdims=True))
            acc = None
            den = None
            for s, v_ref in zip(ss, v_refs):
                e = jnp.exp2(s - m)
                dsum = e.sum(axis=-1, keepdims=True)
                pv = _dot(e.astype(BF), v_ref[:, jb * LANES:(jb + 1) * LANES])
                acc = pv if acc is None else acc + pv
                den = dsum if den is None else den + dsum
            halves.append(acc * (1.0 / den))
        o = jnp.where(lane < HEAD_DIM, halves[0], halves[1])
        o_ref[:, jb * LANES:(jb + 1) * LANES] = o.astype(BF)


def _attn(q, ks, vs, tq):
    b, l, _ = q.shape
    n_src = len(ks)
    in_specs = [pl.BlockSpec((None, tq, QK_WIDTH), lambda i, j: (i, j, 0))]
    in_specs += [pl.BlockSpec((None, k.shape[1], QK_WIDTH), lambda i, j: (i, 0, 0)) for k in ks]
    in_specs += [pl.BlockSpec((None, v.shape[1], V_WIDTH), lambda i, j: (i, 0, 0)) for v in vs]
    return pl.pallas_call(
        functools.partial(_attn_kernel, n_src=n_src),
        out_shape=jax.ShapeDtypeStruct((b, l, V_WIDTH), BF),
        grid=(b, l // tq),
        in_specs=in_specs,
        out_specs=pl.BlockSpec((None, tq, V_WIDTH), lambda i, j: (i, j, 0)),
        compiler_params=_cparams(("parallel", "arbitrary")),
        name="attn",
    )(q, *ks, *vs)


def _pool_kernel(b_ref, w_ref, s_ref, o_ref, xp_ref):
    l = b_ref.shape[0]
    x = b_ref[...]
    zeros = jnp.zeros((POOL_PAD, POOL_WIDTH), F32)
    xp_ref[0:POOL_PAD, :] = zeros
    xp_ref[POOL_PAD + l:POOL_PAD + l + POOL_PAD, :] = zeros
    xp_ref[POOL_PAD:POOL_PAD + l, :] = x
    t = lax.broadcasted_iota(jnp.int32, (l, LANES), 0)
    lane = lax.broadcasted_iota(jnp.int32, (l, LANES), 1)
    outs = []
    for half in range(POOL_WIDTH // LANES):
        w_small = POOL_WINDOWS[2 * half]
        w_big = POOL_WINDOWS[2 * half + 1]
        lo, hi = half * LANES, (half + 1) * LANES

        def win(j):
            return xp_ref[pl.ds(POOL_PAD + j, l), lo:hi]

        s_small = None
        for j in range(-(w_small // 2), w_small // 2):
            s_small = win(j) if s_small is None else s_small + win(j)
        s_big = s_small
        for j in range(-(w_big // 2), w_big // 2):
            if not (-(w_small // 2) <= j < w_small // 2):
                s_big = s_big + win(j)

        def cnt(w):
            lo_i = jnp.maximum(t - w // 2, 0)
            hi_i = jnp.minimum(t - w // 2 + w, l)
            return (hi_i - lo_i).astype(F32)

        left = lane < POOL_GROUP_DIM
        s = jnp.where(left, s_small, s_big)
        c = jnp.where(left, cnt(w_small), cnt(w_big))
        outs.append(s / c - x[:, lo:hi])
    pooled = jnp.concatenate(outs, axis=1).astype(BF)
    o_ref[...] = (_dot(pooled, w_ref[...]) * s_ref[...]).astype(BF)


def _pool(pb, w_bd, pscale):
    b, l, w = pb.shape
    return pl.pallas_call(
        _pool_kernel,
        out_shape=jax.ShapeDtypeStruct((b, l, w), BF),
        grid=(b,),
        in_specs=[
            pl.BlockSpec((None, l, w), lambda i: (i, 0, 0)),
            pl.BlockSpec((w, w), lambda i: (0, 0)),
            pl.BlockSpec((1, w), lambda i: (0, 0)),
        ],
        out_specs=pl.BlockSpec((None, l, w), lambda i: (i, 0, 0)),
        scratch_shapes=[pltpu.VMEM((l + 2 * POOL_PAD, w), F32)],
        compiler_params=_cparams(("parallel",)),
        name="pool",
    )(pb, w_bd, pscale)


def _outproj_kernel(x_ref, att_ref, ob_ref, mod_ref, n2_ref, w1_ref, w2_ref, xo_ref, h_ref):
    y = _dot(att_ref[...], w1_ref[...]) + _dot(ob_ref[...], w2_ref[...])
    xn = x_ref[...] + mod_ref[2:3, :] * y
    xo_ref[...] = xn
    h = _rms(xn) * n2_ref[...] * (1.0 + mod_ref[4:5, :]) + mod_ref[3:4, :]
    h_ref[...] = h.astype(BF)


def _outproj(x, att, ob, mod, mod_row0, lw, tl):
    b, l, d = x.shape
    if mod_row0 is None:
        mod_spec = pl.BlockSpec((None, N_MOD, d), lambda i, j: (i, 0, 0))
    else:
        mod_spec = pl.BlockSpec((None, N_MOD, d), lambda i, j: (mod_row0, 0, 0))

    def tile(w):
        return pl.BlockSpec((None, tl, w), lambda i, j: (i, j, 0))

    def full(a):
        return pl.BlockSpec(a.shape, lambda i, j: (0, 0))

    return pl.pallas_call(
        _outproj_kernel,
        out_shape=[jax.ShapeDtypeStruct((b, l, d), F32), jax.ShapeDtypeStruct((b, l, d), BF)],
        grid=(b, l // tl),
        in_specs=[tile(d), tile(V_WIDTH), tile(POOL_WIDTH), mod_spec, full(lw["n2_g"]), full(lw["w_o1"]),
                  full(lw["w_o2"])],
        out_specs=[tile(d), tile(d)],
        compiler_params=_cparams(("parallel", "parallel")),
        name="outproj",
    )(x, att, ob, mod, lw["n2_g"], lw["w_o1"], lw["w_o2"])


def _gelu_tanh(x):
    return 0.5 * x * (1.0 + jnp.tanh(0.7978845608028654 * (x + 0.044715 * (x * x * x))))


def _peer_prologue(hn_ref, wq_ref, sk_ref, c1_ref, e1_ref, i2_ref, s_ref, v_ref, cand_ref):
    t = hn_ref.shape[0]
    hn = hn_ref[...]

    sub8 = lax.broadcasted_iota(jnp.int32, (8, t), 0)
    sub16 = lax.broadcasted_iota(jnp.int32, (PEER_TOPK, t), 0)
    zeros8 = jnp.zeros((8, t), F32)

    def head(h, carry):
        qh = _dot(hn, wq_ref[h]).astype(BF)
        r2 = None
        for p in range(2):
            st = _dot_nt(sk_ref[h, p], qh[:, p * PEER_HALF:(p + 1) * PEER_HALF])
            s_ref[p, :, 0:t] = st
            m = st.max(axis=0, keepdims=True)
            v_ref[p, 0:1, 0:t] = m
            rank = jnp.zeros((PEER_N_KEYS, t), F32)
            for k in range(1, PEER_TOPK):
                below = st < m
                if p == 1:
                    rank = rank + jnp.where(below, 1.0, 0.0)
                m = jnp.where(below, st, NEG_INF).max(axis=0, keepdims=True)
                v_ref[p, k:k + 1, 0:t] = m
            if p == 1:
                r2 = rank + jnp.where(st < m, 1.0, 0.0)
        v1 = v_ref[0, :, 0:t]
        v2 = v_ref[1, :, 0:t]

        groups = [
            v1[0:1, :] + v2,
            jnp.where(sub16 >= 1, v1 + v2[0:1, :], NEG_INF),
            jnp.where(sub8 >= 1, v1[1:2, :] + v2[0:8, :], NEG_INF),
            jnp.where(sub8 >= 2, v1[0:8, :] + v2[1:2, :], NEG_INF),
            jnp.where((sub8 >= 2) & (sub8 <= 4), v1[2:3, :] + v2[0:8, :], NEG_INF),
            jnp.where((sub8 >= 3) & (sub8 <= 4), v1[0:8, :] + v2[2:3, :], NEG_INF),
            jnp.where(sub8 == 3, v1[3:4, :] + v2[0:8, :], NEG_INF),
        ]
        cand_ref[:, 0:t] = jnp.concatenate(groups, axis=0)
        c00 = v1[0:1, :] + v2[0:1, :]

        def below_max(k, m):
            c = cand_ref[:, 0:t]
            return jnp.where(c < m, c, NEG_INF).max(axis=0, keepdims=True)

        tau = lax.fori_loop(1, PEER_TOPK, below_max, c00)
        c = cand_ref[:, 0:t]
        sel = c >= tau
        z = jnp.where(sel, jnp.exp(c - c00), 0.0).sum(axis=0, keepdims=True)
        ind = jnp.where(sel, 1.0, 0.0)

        def row_total(r0, rows, a):
            return jnp.where(sub16 == a, ind[r0:r0 + rows, :].sum(axis=0, keepdims=True), 0.0)

        cnt = (ind[16:32, :] + jnp.concatenate([ind[40:48, :] + ind[56:64, :], zeros8], axis=0)
               + row_total(0, 16, 0) + row_total(32, 8, 1) + row_total(48, 8, 2) + row_total(64, 8, 3))
        s1 = s_ref[0, :, 0:t]
        s2 = s_ref[1, :, 0:t]
        c1 = jnp.zeros((PEER_N_KEYS, t), F32)
        for a in range(PEER_TOPK):
            c1 = jnp.where(s1 == v1[a:a + 1, :], cnt[a:a + 1, :], c1)
        c1_ref[h, :, 0:t] = c1
        e1_ref[h, :, 0:t] = jnp.exp(s1 - v1[0:1, :]) * (1.0 / z)
        i2_ref[h, 0, 0:PEER_N_KEYS, 0:t] = r2.astype(BF)
        i2_ref[h, 1, 0:PEER_N_KEYS, 0:t] = jnp.exp(s2 - v2[0:1, :]).astype(BF)
        return carry

    lax.fori_loop(0, PEER_HEADS, head, 0)


def _peer_dense(at_ref, wt_ref, blk, t, c1_ref, e1_ref, i2_ref):
    i1_0 = pl.multiple_of(blk * PEER_I1_PER_BLK, PEER_I1_PER_BLK)
    zero = jnp.zeros((), BF)
    for lg in range(t // LANES):
        ls = slice(lg * LANES, (lg + 1) * LANES)
        c1_s = [c1_ref[h, pl.ds(i1_0, PEER_I1_PER_BLK), ls].astype(BF) for h in range(PEER_HEADS)]
        e1_s = [e1_ref[h, pl.ds(i1_0, PEER_I1_PER_BLK), ls].astype(BF) for h in range(PEER_HEADS)]
        for j in range(PEER_I1_PER_BLK):
            rs = slice(j * PEER_N_KEYS, (j + 1) * PEER_N_KEYS)
            g = jnp.zeros((PEER_N_KEYS, LANES), BF)
            for h in range(PEER_HEADS):
                c1 = jnp.broadcast_to(c1_s[h][j:j + 1, :], (PEER_N_KEYS, LANES))
                e1 = jnp.broadcast_to(e1_s[h][j:j + 1, :], (PEER_N_KEYS, LANES))
                gate1 = jnp.minimum(jnp.maximum(c1 - i2_ref[h, 0, 0:PEER_N_KEYS, ls], zero), e1)
                g = g + gate1 * i2_ref[h, 1, 0:PEER_N_KEYS, ls]
            wt_ref[rs, ls] = _gelu_tanh(at_ref[rs, ls].astype(BF)) * g


def _peer_kernel(hn_ref, x_ref, mod_ref, fg_ref, wq_ref, sk_ref, u_ref, vt_ref, o_ref,
                 acc_ref, at_ref, wt_ref, c1_ref, e1_ref, i2_ref, s_ref, v_ref, cand_ref, *, final):
    e = pl.program_id(1)
    t = hn_ref.shape[0]

    @pl.when(e == 0)
    def _():
        _peer_prologue(hn_ref, wq_ref, sk_ref, c1_ref, e1_ref, i2_ref, s_ref, v_ref, cand_ref)
        acc_ref[...] = jnp.zeros_like(acc_ref)

    at_ref[:, 0:t] = _dot_nt(u_ref[...], hn_ref[...])
    _peer_dense(at_ref, wt_ref, e, t, c1_ref, e1_ref, i2_ref)
    acc_ref[...] += _dot(vt_ref[...], wt_ref[:, 0:t])

    @pl.when(e == pl.num_programs(1) - 1)
    def _():
        y = x_ref[...] + mod_ref[5:6, :] * acc_ref[...].T
        if final:
            y = _rms(y) * fg_ref[...]
        o_ref[...] = y


def _peer(hn, x_mid, mod, mod_row0, tok_per_row, final_g, final, lw):
    ntok, d = hn.shape
    t = PEER_TOK
    n_blk = PEER_N_EXPERTS // PEER_EBLK
    if mod_row0 is None:
        assert tok_per_row % t == 0
        tiles_per_row = tok_per_row // t
        mod_spec = pl.BlockSpec((None, N_MOD, d), lambda i, e: (i // tiles_per_row, 0, 0))
    else:
        mod_spec = pl.BlockSpec((None, N_MOD, d), lambda i, e: (mod_row0, 0, 0))
    tw = t + LANES
    tab_f = pltpu.VMEM((PEER_HEADS, PEER_N_KEYS, tw), F32)
    tab_i2 = pltpu.VMEM((PEER_HEADS, 2, PEER_N_KEYS + 16, tw), BF)
    return pl.pallas_call(
        functools.partial(_peer_kernel, final=final),
        out_shape=jax.ShapeDtypeStruct((ntok, d), F32),
        grid=(ntok // t, n_blk),
        in_specs=[
            pl.BlockSpec((t, d), lambda i, e: (i, 0)),
            pl.BlockSpec((t, d), lambda i, e: (i, 0)),
            mod_spec,
            pl.BlockSpec((1, d), lambda i, e: (0, 0)),
            pl.BlockSpec(lw["peer_wq"].shape, lambda i, e: (0, 0, 0)),
            pl.BlockSpec(lw["peer_sk"].shape, lambda i, e: (0, 0, 0, 0)),
            pl.BlockSpec((PEER_EBLK, d), lambda i, e: (e, 0)),
            pl.BlockSpec((None, d, PEER_EBLK), lambda i, e: (e, 0, 0)),
        ],
        out_specs=pl.BlockSpec((t, d), lambda i, e: (i, 0)),
        scratch_shapes=[
            pltpu.VMEM((d, t), F32),
            pltpu.VMEM((PEER_EBLK, tw), F32),
            pltpu.VMEM((PEER_EBLK, tw), BF),
            tab_f, tab_f, tab_i2,
            pltpu.VMEM((2, PEER_N_KEYS, tw), F32),
            pltpu.VMEM((2, PEER_TOPK, tw), F32),
            pltpu.VMEM((PEER_CAND_ROWS, tw), F32),
        ],
        compiler_params=_cparams(("parallel", "arbitrary")),
        name="peer",
    )(hn, x_mid, mod, final_g, lw["peer_wq"], lw["peer_sk"], lw["peer_u"], lw["peer_vt"])


def _rope_tables(l):
    rows = l // GRID_W
    row = jnp.repeat(jnp.arange(rows), GRID_W).astype(F32)
    col = jnp.tile(jnp.arange(GRID_W), rows).astype(F32)

    def cs(rot_dim):
        n = rot_dim // 4
        inv = ROPE_BASE ** (-jnp.arange(n, dtype=F32) / n)
        ang = jnp.concatenate([row[:, None] * inv, col[:, None] * inv], axis=-1)
        return jnp.cos(ang), jnp.sin(ang)

    ones = lambda w: jnp.ones((l, w), F32)
    zeros = lambda w: jnp.zeros((l, w), F32)
    def block_tables(c, s):
        pad = ROLL_HALF - c.shape[1]
        return (jnp.concatenate([c, ones(pad), c, ones(pad)], axis=1),
                jnp.concatenate([-s, zeros(pad), s, zeros(pad)], axis=1))

    cos_a, sin_a = block_tables(*cs(HEAD_DIM))
    cos_c, sin_c = block_tables(*cs(MLA_ROPE_DIM))
    return cos_a, sin_a, cos_c, sin_c


def _peer_vt_blocks(v_tab):
    n, d = v_tab.shape
    return v_tab.astype(BF).reshape(n // PEER_EBLK, PEER_EBLK, d).transpose(0, 2, 1)


def _head_block(rot, nope):
    rows = (rot if rot is not None else nope).shape[0]
    n = 0 if rot is None else rot.shape[1] // 2
    m = 0 if nope is None else nope.shape[1]
    low = min(m, ROLL_HALF - n)
    parts = []
    if n:
        parts.append(rot[:, :n])
    parts.append(nope[:, :low] if m else jnp.zeros((rows, 0), F32))
    parts.append(jnp.zeros((rows, ROLL_HALF - n - low), F32))
    if n:
        parts.append(rot[:, n:])
    parts.append(nope[:, low:] if m else jnp.zeros((rows, 0), F32))
    parts.append(jnp.zeros((rows, ROLL_HALF - n - (m - low)), F32))
    blk = jnp.concatenate(parts, axis=1)
    assert blk.shape == (rows, LANES)
    return blk


def _layer_weights(layer, norm1_g, norm2_g, w_in, gqa_qn_g, gqa_kn_g, pool_w, pool_scale, mla_qn_g, mla_kvn_g,
                   mla_w_uq, mla_w_ukv, w_out, peer_wq, peer_subkeys, peer_u, peer_v):
    d = D_MODEL
    w = w_in[layer]
    o = 0
    aq = w[:, o:o + GQA_Q_HEADS * HEAD_DIM]; o += GQA_Q_HEADS * HEAD_DIM
    ak = w[:, o:o + GQA_KV_HEADS * HEAD_DIM]; o += GQA_KV_HEADS * HEAD_DIM
    av = w[:, o:o + GQA_KV_HEADS * HEAD_DIM]; o += GQA_KV_HEADS * HEAD_DIM
    wb = w[:, o:o + POOL_WIDTH]; o += POOL_WIDTH
    wcq = w[:, o:o + MLA_Q_RANK]; o += MLA_Q_RANK
    wckv = w[:, o:o + MLA_KV_RANK]; o += MLA_KV_RANK
    wkr = w[:, o:o + MLA_ROPE_DIM]
    cols = []
    for h in range(GQA_Q_HEADS):
        cols.append(_head_block(aq[:, h * HEAD_DIM:(h + 1) * HEAD_DIM], None))
    for g in range(GQA_KV_HEADS):
        cols.append(_head_block(ak[:, g * HEAD_DIM:(g + 1) * HEAD_DIM], None))
    for h in range(GQA_Q_HEADS):
        g = h // GQA_GROUP
        cols.append(av[:, g * HEAD_DIM:(g + 1) * HEAD_DIM])
    cols += [wb, wcq, wckv]
    kr_blk = _head_block(wkr, jnp.zeros((d, MLA_NOPE_DIM), F32))
    cols.append(kr_blk)
    w_wide = jnp.concatenate(cols, axis=1).astype(BF)
    assert w_wide.shape == (d, IN_WIDE)

    def head_gain(g):
        return _head_block(g[None, :], None)

    qd = MLA_NOPE_DIM + MLA_ROPE_DIM
    uq = mla_w_uq[layer]
    ukv = mla_w_ukv[layer]
    uq_cols, uk_cols, uv_cols = [], [], []
    zero_rot = jnp.zeros((MLA_KV_RANK, MLA_ROPE_DIM), F32)
    for h in range(MLA_HEADS):
        q_h = uq[:, h * qd:(h + 1) * qd]
        uq_cols.append(_head_block(q_h[:, MLA_NOPE_DIM:], q_h[:, :MLA_NOPE_DIM]))
        k0 = h * (MLA_NOPE_DIM + MLA_V_DIM)
        uk_cols.append(_head_block(zero_rot, ukv[:, k0:k0 + MLA_NOPE_DIM]))
        uv_cols += [ukv[:, k0 + MLA_NOPE_DIM:k0 + MLA_NOPE_DIM + MLA_V_DIM]]

    pw = pool_w[layer]
    w_bd = jnp.zeros((POOL_WIDTH, POOL_WIDTH), F32)
    for g in range(POOL_GROUPS):
        s = slice(g * POOL_GROUP_DIM, (g + 1) * POOL_GROUP_DIM)
        w_bd = w_bd.at[s, s].set(pw[g])

    wo = w_out[layer]
    na = GQA_Q_HEADS * HEAD_DIM
    w_o1 = jnp.concatenate([wo[0:na], wo[na + POOL_WIDTH:]], axis=0).astype(BF)
    w_o2 = wo[na:na + POOL_WIDTH].astype(BF)

    return {
        "n1_g": norm1_g[layer][None, :],
        "n2_g": norm2_g[layer][None, :],
        "w_in": w_wide,
        "gqa_qg": head_gain(gqa_qn_g[layer]),
        "gqa_kg": head_gain(gqa_kn_g[layer]),
        "mla_qg": mla_qn_g[layer][None, :],
        "mla_kvg": mla_kvn_g[layer][None, :],
        "w_uq": jnp.concatenate(uq_cols, axis=1).astype(BF),
        "w_uk": jnp.concatenate(uk_cols, axis=1).astype(BF),
        "w_uv": jnp.concatenate(uv_cols, axis=1).astype(BF),
        "pool_w": w_bd.astype(BF),
        "pool_scale": pool_scale[layer][None, :],
        "w_o1": w_o1,
        "w_o2": w_o2,
        "peer_wq": peer_wq[layer].reshape(d, PEER_HEADS, PEER_QUERY_DIM).transpose(1, 0, 2).astype(BF),
        "peer_sk": peer_subkeys[layer].astype(BF),
        "peer_u": peer_u[layer].astype(BF),
        "peer_vt": _peer_vt_blocks(peer_v[layer]),
    }


def _tile(l, pref):
    return pref if l % pref == 0 else l


def kernel(x, c, ctx, c_ctx, ada_w, ada_b, norm1_g, norm2_g, w_in, gqa_qn_g, gqa_kn_g, pool_w, pool_scale,
           mla_qn_g, mla_kvn_g, mla_w_uq, mla_w_ukv, w_out, peer_wq, peer_subkeys, peer_u, peer_v, final_g):
    b, l, d = x.shape
    lc = ctx.shape[1]
    depth = ada_w.shape[0]
    assert d == D_MODEL and l % GRID_W == 0
    assert (b * l) % PEER_TOK == 0 and (b * lc) % PEER_TOK == 0

    n_rows = -(-(b + 1) // 8) * 8
    cc = jnp.concatenate([c, c_ctx[None, :], jnp.zeros((n_rows - b - 1, d), F32)], axis=0)
    mod_all = _adaln(cc, ada_w, ada_b).reshape(depth, n_rows, N_MOD, d)

    rope_tabs = _rope_tables(l)
    tl_l, tl_c = _tile(l, 512), _tile(lc, 512)
    tq_l, tq_c = _tile(l, 256), _tile(lc, 256)
    fg = final_g[None, :]

    xl, xc = x, ctx
    for layer in range(depth):
        lw = _layer_weights(layer, norm1_g, norm2_g, w_in, gqa_qn_g, gqa_kn_g, pool_w, pool_scale, mla_qn_g,
                            mla_kvn_g, mla_w_uq, mla_w_ukv, w_out, peer_wq, peer_subkeys, peer_u, peer_v)
        mod = mod_all[layer]
        last = layer == depth - 1
        q_l, k_l, v_l, pb_l = _inproj(xl, mod, None, lw, rope_tabs, tl_l)
        q_c, k_c, v_c, pb_c = _inproj(xc, mod, b, lw, None, tl_c)

        att_l = _attn(q_l, [k_c, k_l], [v_c, v_l], tq_l)
        ob_l = _pool(pb_l, lw["pool_w"], lw["pool_scale"])
        xl_mid, hn_l = _outproj(xl, att_l, ob_l, mod, None, lw, tl_l)
        xl = _peer(hn_l.reshape(b * l, d), xl_mid.reshape(b * l, d), mod, None, l, fg, last, lw).reshape(b, l, d)

        if not last:
            att_c = _attn(q_c, [k_c], [v_c], tq_c)
            ob_c = _pool(pb_c, lw["pool_w"], lw["pool_scale"])
            xc_mid, hn_c = _outproj(xc, att_c, ob_c, mod, b, lw, tl_c)
            xc = _peer(hn_c.reshape(b * lc, d), xc_mid.reshape(b * lc, d), mod, b, lc, fg, False,
                       lw).reshape(b, lc, d)
    return xl
```

```python
import functools

import jax
import jax.numpy as jnp
from jax import lax
from jax.experimental import pallas as pl
from jax.experimental.pallas import tpu as pltpu

D_MODEL = 1024
GRID_W = 64
N_MOD = 6
EPS = 1e-6
ROPE_BASE = 10000.0
HEAD_DIM = 64
GQA_Q_HEADS = 6
GQA_KV_HEADS = 2
GQA_GROUP = GQA_Q_HEADS // GQA_KV_HEADS
POOL_GROUPS = 4
POOL_WINDOWS = (2, 4, 8, 16)
POOL_WIDTH = D_MODEL // 4
POOL_GROUP_DIM = POOL_WIDTH // POOL_GROUPS
MLA_HEADS = 6
MLA_NOPE_DIM = 64
MLA_ROPE_DIM = 32
MLA_V_DIM = 64
MLA_Q_RANK = 384
MLA_KV_RANK = 256
PEER_HEADS = 8
PEER_N_KEYS = 128
PEER_N_EXPERTS = PEER_N_KEYS * PEER_N_KEYS
PEER_TOPK = 16
PEER_QUERY_DIM = 256
PEER_HALF = PEER_QUERY_DIM // 2

LANES = 128
N_ATT_HEADS = GQA_Q_HEADS + MLA_HEADS
QK_WIDTH = N_ATT_HEADS * LANES
V_WIDTH = N_ATT_HEADS * HEAD_DIM
POOL_PAD = 16

_C_QA = 0
_C_KA = _C_QA + GQA_Q_HEADS * LANES
_C_VA = _C_KA + GQA_KV_HEADS * LANES
_C_B = _C_VA + GQA_Q_HEADS * HEAD_DIM
_C_CQ = _C_B + POOL_WIDTH
_C_CKV = _C_CQ + MLA_Q_RANK
_C_KR = _C_CKV + MLA_KV_RANK
IN_WIDE = _C_KR + LANES
ROLL_HALF = LANES // 2

PEER_TOK = 512
PEER_EBLK = 2048
PEER_I1_PER_BLK = PEER_EBLK // PEER_N_KEYS
PEER_CAND_ROWS = 2 * PEER_TOPK + 5 * 8
VMEM_LIMIT = 56 * 1024 * 1024

BF = jnp.bfloat16
F32 = jnp.float32
LOG2_E = 1.4426950408889634
NEG_INF = float("-inf")


def _cparams(sem):
    return pltpu.CompilerParams(dimension_semantics=sem, vmem_limit_bytes=VMEM_LIMIT)


def _dot(a, b):
    return jnp.dot(a, b, preferred_element_type=F32)


def _dot_nt(a, b):
    return lax.dot_general(a, b, (((1,), (1,)), ((), ())), preferred_element_type=F32)


def _split_bf16(a):
    hi = a.astype(BF)
    lo = (a - hi.astype(F32)).astype(BF)
    return hi, lo


def _dot3(a, b):
    ah, al = _split_bf16(a)
    bh, bl = _split_bf16(b)
    return _dot(ah, bh) + _dot(ah, bl) + _dot(al, bh)


def _rms(x):
    return x * lax.rsqrt(jnp.mean(x * x, axis=-1, keepdims=True) + EPS)


def _adaln_kernel(c_ref, w_ref, b_ref, o_ref):
    c = c_ref[...]
    s = c * (1.0 / (1.0 + jnp.exp(-c)))
    o_ref[...] = _dot3(s, w_ref[...]) + b_ref[...]


def _adaln(cc, ada_w, ada_b):
    depth, d, nd = ada_w.shape
    r = cc.shape[0]
    nblk = nd // d
    return pl.pallas_call(
        _adaln_kernel,
        out_shape=jax.ShapeDtypeStruct((depth, r, nd), F32),
        grid=(depth, nblk),
        in_specs=[
            pl.BlockSpec((r, d), lambda l, j: (0, 0)),
            pl.BlockSpec((None, d, d), lambda l, j: (l, 0, j)),
            pl.BlockSpec((None, 1, d), lambda l, j: (l, 0, j)),
        ],
        out_specs=pl.BlockSpec((None, r, d), lambda l, j: (l, 0, j)),
        compiler_params=_cparams(("arbitrary", "arbitrary")),
        name="adaln",
    )(cc, ada_w, ada_b.reshape(depth, 1, nd))


def _rope_block(xb, cb, sb):
    return xb * cb + pltpu.roll(xb, ROLL_HALF, 1) * sb


def _inproj_kernel(*refs, rope):
    if rope:
        (x_ref, mod_ref, n1_ref, w_ref, qg_ref, kg_ref, cqg_ref, ckvg_ref, wuq_ref, wuk_ref, wuv_ref,
         ca_ref, sa_ref, cc_ref, sc_ref, q_ref, k_ref, v_ref, b_ref) = refs
    else:
        (x_ref, mod_ref, n1_ref, w_ref, qg_ref, kg_ref, cqg_ref, ckvg_ref, wuq_ref, wuk_ref, wuv_ref,
         q_ref, k_ref, v_ref, b_ref) = refs
    x = x_ref[...]
    shift = mod_ref[0:1, :]
    scale = mod_ref[1:2, :]
    h = _rms(x) * n1_ref[...] * (1.0 + scale) + shift
    p = _dot(h.astype(BF), w_ref[...])

    sa = HEAD_DIM ** -0.5 * LOG2_E
    sc = (MLA_NOPE_DIM + MLA_ROPE_DIM) ** -0.5 * LOG2_E

    def gqa_head(c0, g_ref):
        blk = p[:, c0:c0 + LANES]
        ms = jnp.sum(blk * blk, axis=-1, keepdims=True) * (1.0 / HEAD_DIM)
        y = blk * lax.rsqrt(ms + EPS) * g_ref[...]
        if rope:
            y = _rope_block(y, ca_ref[...], sa_ref[...])
        return y

    for hh in range(GQA_Q_HEADS):
        q_ref[:, hh * LANES:(hh + 1) * LANES] = (gqa_head(_C_QA + hh * LANES, qg_ref) * sa).astype(BF)
    for g in range(GQA_KV_HEADS):
        kb = gqa_head(_C_KA + g * LANES, kg_ref).astype(BF)
        for hh in range(g * GQA_GROUP, (g + 1) * GQA_GROUP):
            k_ref[:, hh * LANES:(hh + 1) * LANES] = kb
    nva = GQA_Q_HEADS * HEAD_DIM
    v_ref[:, 0:nva] = p[:, _C_VA:_C_VA + nva].astype(BF)
    b_ref[...] = p[:, _C_B:_C_B + POOL_WIDTH]

    cq = _rms(p[:, _C_CQ:_C_CQ + MLA_Q_RANK]) * cqg_ref[...]
    ckv = (_rms(p[:, _C_CKV:_C_CKV + MLA_KV_RANK]) * ckvg_ref[...]).astype(BF)
    qc = _dot(cq.astype(BF), wuq_ref[...])
    kc = _dot(ckv, wuk_ref[...])
    vc = _dot(ckv, wuv_ref[...])
    kr = p[:, _C_KR:_C_KR + LANES]
    if rope:
        kr = _rope_block(kr, cc_ref[...], sc_ref[...])
    base = GQA_Q_HEADS * LANES
    for hh in range(MLA_HEADS):
        qb = qc[:, hh * LANES:(hh + 1) * LANES]
        if rope:
            qb = _rope_block(qb, cc_ref[...], sc_ref[...])
        q_ref[:, base + hh * LANES:base + (hh + 1) * LANES] = (qb * sc).astype(BF)
        k_ref[:, base + hh * LANES:base + (hh + 1) * LANES] = (kc[:, hh * LANES:(hh + 1) * LANES] + kr).astype(BF)
    v_ref[:, nva:V_WIDTH] = vc.astype(BF)


def _inproj(x, mod, mod_row0, lw, rope_tabs, tl):
    b, l, d = x.shape
    rope = rope_tabs is not None
    grid = (b, l // tl)

    def full(a):
        nd = a.ndim
        return pl.BlockSpec(a.shape, lambda i, j: (0,) * nd)

    if mod_row0 is None:
        mod_spec = pl.BlockSpec((None, N_MOD, d), lambda i, j: (i, 0, 0))
    else:
        mod_spec = pl.BlockSpec((None, N_MOD, d), lambda i, j: (mod_row0, 0, 0))
    weights = [lw["n1_g"], lw["w_in"], lw["gqa_qg"], lw["gqa_kg"], lw["mla_qg"], lw["mla_kvg"],
               lw["w_uq"], lw["w_uk"], lw["w_uv"]]
    in_specs = [pl.BlockSpec((None, tl, d), lambda i, j: (i, j, 0)), mod_spec] + [full(a) for a in weights]
    args = [x, mod] + weights
    if rope:
        in_specs += [pl.BlockSpec((tl, LANES), lambda i, j: (j, 0)) for _ in range(4)]
        args += list(rope_tabs)
    def rows(w, dt):
        return jax.ShapeDtypeStruct((b, l, w), dt), pl.BlockSpec((None, tl, w), lambda i, j: (i, j, 0))

    outs = [rows(QK_WIDTH, BF), rows(QK_WIDTH, BF), rows(V_WIDTH, BF), rows(POOL_WIDTH, F32)]
    return pl.pallas_call(
        functools.partial(_inproj_kernel, rope=rope),
        out_shape=[o[0] for o in outs],
        grid=grid,
        in_specs=in_specs,
        out_specs=[o[1] for o in outs],
        compiler_params=_cparams(("parallel", "parallel")),
        name="inproj",
    )(*args)


def _attn_kernel(*refs, n_src):
    q_ref = refs[0]
    k_refs = refs[1:1 + n_src]
    v_refs = refs[1 + n_src:1 + 2 * n_src]
    o_ref = refs[1 + 2 * n_src]
    tq = q_ref.shape[0]
    lane = lax.broadcasted_iota(jnp.int32, (tq, LANES), 1)
    for jb in range(N_ATT_HEADS // 2):
        halves = []
        for n in (2 * jb, 2 * jb + 1):
            q = q_ref[:, n * LANES:(n + 1) * LANES]
            ss = [_dot_nt(q, k_ref[:, n * LANES:(n + 1) * LANES]) for k_ref in k_refs]
            m = ss[0].max(axis=-1, keepdims=True)
            for s in ss[1:]:
                m = jnp.maximum(m, s.max(axis=-1, keepdims=True))
            acc = None
            den = None
            for s, v_ref in zip(ss, v_refs):
                e = jnp.exp2(s - m)
                dsum = e.sum(axis=-1, keepdims=True)
                pv = _dot(e.astype(BF), v_ref[:, jb * LANES:(jb + 1) * LANES])
                acc = pv if acc is None else acc + pv
                den = dsum if den is None else den + dsum
            halves.append(acc * (1.0 / den))
        o = jnp.where(lane < HEAD_DIM, halves[0], halves[1])
        o_ref[:, jb * LANES:(jb + 1) * LANES] = o.astype(BF)


def _attn(q, ks, vs, tq):
    b, l, _ = q.shape
    n_src = len(ks)
    in_specs = [pl.BlockSpec((None, tq, QK_WIDTH), lambda i, j: (i, j, 0))]
    in_specs += [pl.BlockSpec((None, k.shape[1], QK_WIDTH), lambda i, j: (i, 0, 0)) for k in ks]
    in_specs += [pl.BlockSpec((None, v.shape[1], V_WIDTH), lambda i, j: (i, 0, 0)) for v in vs]
    return pl.pallas_call(
        functools.partial(_attn_kernel, n_src=n_src),
        out_shape=jax.ShapeDtypeStruct((b, l, V_WIDTH), BF),
        grid=(b, l // tq),
        in_specs=in_specs,
        out_specs=pl.BlockSpec((None, tq, V_WIDTH), lambda i, j: (i, j, 0)),
        compiler_params=_cparams(("parallel", "arbitrary")),
        name="attn",
    )(q, *ks, *vs)


def _pool_kernel(b_ref, w_ref, s_ref, o_ref, xp_ref):
    l = b_ref.shape[0]
    x = b_ref[...]
    zeros = jnp.zeros((POOL_PAD, POOL_WIDTH), F32)
    xp_ref[0:POOL_PAD, :] = zeros
    xp_ref[POOL_PAD + l:POOL_PAD + l + POOL_PAD, :] = zeros
    xp_ref[POOL_PAD:POOL_PAD + l, :] = x
    t = lax.broadcasted_iota(jnp.int32, (l, LANES), 0)
    lane = lax.broadcasted_iota(jnp.int32, (l, LANES), 1)
    outs = []
    for half in range(POOL_WIDTH // LANES):
        w_small = POOL_WINDOWS[2 * half]
        w_big = POOL_WINDOWS[2 * half + 1]
        lo, hi = half * LANES, (half + 1) * LANES

        def win(j):
            return xp_ref[pl.ds(POOL_PAD + j, l), lo:hi]

        s_small = None
        for j in range(-(w_small // 2), w_small // 2):
            s_small = win(j) if s_small is None else s_small + win(j)
        s_big = s_small
        for j in range(-(w_big // 2), w_big // 2):
            if not (-(w_small // 2) <= j < w_small // 2):
                s_big = s_big + win(j)

        def cnt(w):
            lo_i = jnp.maximum(t - w // 2, 0)
            hi_i = jnp.minimum(t - w // 2 + w, l)
            return (hi_i - lo_i).astype(F32)

        left = lane < POOL_GROUP_DIM
        s = jnp.where(left, s_small, s_big)
        c = jnp.where(left, cnt(w_small), cnt(w_big))
        outs.append(s / c - x[:, lo:hi])
    pooled = jnp.concatenate(outs, axis=1).astype(BF)
    o_ref[...] = (_dot(pooled, w_ref[...]) * s_ref[...]).astype(BF)


def _pool(pb, w_bd, pscale):
    b, l, w = pb.shape
    return pl.pallas_call(
        _pool_kernel,
        out_shape=jax.ShapeDtypeStruct((b, l, w), BF),
        grid=(b,),
        in_specs=[
            pl.BlockSpec((None, l, w), lambda i: (i, 0, 0)),
            pl.BlockSpec((w, w), lambda i: (0, 0)),
            pl.BlockSpec((1, w), lambda i: (0, 0)),
        ],
        out_specs=pl.BlockSpec((None, l, w), lambda i: (i, 0, 0)),
        scratch_shapes=[pltpu.VMEM((l + 2 * POOL_PAD, w), F32)],
        compiler_params=_cparams(("parallel",)),
        name="pool",
    )(pb, w_bd, pscale)


def _outproj_kernel(x_ref, att_ref, ob_ref, mod_ref, n2_ref, w1_ref, w2_ref, xo_ref, h_ref):
    y = _dot(att_ref[...], w1_ref[...]) + _dot(ob_ref[...], w2_ref[...])
    xn = x_ref[...] + mod_ref[2:3, :] * y
    xo_ref[...] = xn
    h = _rms(xn) * n2_ref[...] * (1.0 + mod_ref[4:5, :]) + mod_ref[3:4, :]
    h_ref[...] = h.astype(BF)


def _outproj(x, att, ob, mod, mod_row0, lw, tl):
    b, l, d = x.shape
    if mod_row0 is None:
        mod_spec = pl.BlockSpec((None, N_MOD, d), lambda i, j: (i, 0, 0))
    else:
        mod_spec = pl.BlockSpec((None, N_MOD, d), lambda i, j: (mod_row0, 0, 0))

    def tile(w):
        return pl.BlockSpec((None, tl, w), lambda i, j: (i, j, 0))

    def full(a):
        return pl.BlockSpec(a.shape, lambda i, j: (0, 0))

    return pl.pallas_call(
        _outproj_kernel,
        out_shape=[jax.ShapeDtypeStruct((b, l, d), F32), jax.ShapeDtypeStruct((b, l, d), BF)],
        grid=(b, l // tl),
        in_specs=[tile(d), tile(V_WIDTH), tile(POOL_WIDTH), mod_spec, full(lw["n2_g"]), full(lw["w_o1"]),
                  full(lw["w_o2"])],
        out_specs=[tile(d), tile(d)],
        compiler_params=_cparams(("parallel", "parallel")),
        name="outproj",
    )(x, att, ob, mod, lw["n2_g"], lw["w_o1"], lw["w_o2"])


def _gelu_tanh(x):
    return 0.5 * x * (1.0 + jnp.tanh(0.7978845608028654 * (x + 0.044715 * (x * x * x))))


def _sort16_network():
    def merge(lo, hi, r):
        step = r * 2
        if step < hi - lo:
            yield from merge(lo, hi, step)
            yield from merge(lo + r, hi, step)
            yield from [(i, i + r) for i in range(lo + r, hi - r, step)]
        else:
            yield (lo, lo + r)

    def sort(lo, hi):
        if hi - lo >= 1:
            mid = lo + (hi - lo) // 2
            yield from sort(lo, mid)
            yield from sort(mid + 1, hi)
            yield from merge(lo, hi, 1)

    return tuple(sort(0, PEER_TOPK - 1))


def _top16_desc(blocks):
    a = list(blocks)
    for i, j in _sort16_network():
        a[i], a[j] = jnp.maximum(a[i], a[j]), jnp.minimum(a[i], a[j])
    for shift in (4, 2, 1):
        c = [jnp.maximum(a[i], pltpu.roll(a[PEER_TOPK - 1 - i], shift, 0)) for i in range(PEER_TOPK)]
        stride = PEER_TOPK // 2
        while stride >= 1:
            for i in range(PEER_TOPK):
                if i & stride == 0:
                    c[i], c[i + stride] = jnp.maximum(c[i], c[i + stride]), jnp.minimum(c[i], c[i + stride])
            stride //= 2
        a = c
    return a


def _peer_prologue(hn_ref, wq_ref, sk_ref, c1_ref, e1_ref, i2_ref, s_ref, v_ref, cand_ref):
    t = hn_ref.shape[0]
    hn = hn_ref[...]

    sub8 = lax.broadcasted_iota(jnp.int32, (8, t), 0)
    sub16 = lax.broadcasted_iota(jnp.int32, (PEER_TOPK, t), 0)
    zeros8 = jnp.zeros((8, t), F32)

    def head(h, carry):
        qh = _dot(hn, wq_ref[h]).astype(BF)
        r2 = None
        for p in range(2):
            st = _dot_nt(sk_ref[h, p], qh[:, p * PEER_HALF:(p + 1) * PEER_HALF])
            s_ref[p, :, 0:t] = st
            top = _top16_desc([st[8 * v:8 * v + 8, :] for v in range(PEER_N_KEYS // 8)])
            for k in range(PEER_TOPK):
                v_ref[p, k:k + 1, 0:t] = top[k][0:1, :]
            if p == 1:
                r2 = jnp.zeros((PEER_N_KEYS, t), F32)
                for k in range(PEER_TOPK):
                    r2 = r2 + jnp.where(st < top[k][0:1, :], 1.0, 0.0)
        v1 = v_ref[0, :, 0:t]
        v2 = v_ref[1, :, 0:t]

        groups = [
            v1[0:1, :] + v2,
            jnp.where(sub16 >= 1, v1 + v2[0:1, :], NEG_INF),
            jnp.where(sub8 >= 1, v1[1:2, :] + v2[0:8, :], NEG_INF),
            jnp.where(sub8 >= 2, v1[0:8, :] + v2[1:2, :], NEG_INF),
            jnp.where((sub8 >= 2) & (sub8 <= 4), v1[2:3, :] + v2[0:8, :], NEG_INF),
            jnp.where((sub8 >= 3) & (sub8 <= 4), v1[0:8, :] + v2[2:3, :], NEG_INF),
            jnp.where(sub8 == 3, v1[3:4, :] + v2[0:8, :], NEG_INF),
        ]
        cand_ref[:, 0:t] = jnp.concatenate(groups, axis=0)
        c00 = v1[0:1, :] + v2[0:1, :]

        def below_max(k, m):
            c = cand_ref[:, 0:t]
            return jnp.where(c < m, c, NEG_INF).max(axis=0, keepdims=True)

        tau = lax.fori_loop(1, PEER_TOPK, below_max, c00)
        c = cand_ref[:, 0:t]
        sel = c >= tau
        z = jnp.where(sel, jnp.exp(c - c00), 0.0).sum(axis=0, keepdims=True)
        ind = jnp.where(sel, 1.0, 0.0)

        def row_total(r0, rows, a):
            return jnp.where(sub16 == a, ind[r0:r0 + rows, :].sum(axis=0, keepdims=True), 0.0)

        cnt = (ind[16:32, :] + jnp.concatenate([ind[40:48, :] + ind[56:64, :], zeros8], axis=0)
               + row_total(0, 16, 0) + row_total(32, 8, 1) + row_total(48, 8, 2) + row_total(64, 8, 3))
        s1 = s_ref[0, :, 0:t]
        s2 = s_ref[1, :, 0:t]
        c1 = jnp.zeros((PEER_N_KEYS, t), F32)
        for a in range(PEER_TOPK):
            c1 = jnp.where(s1 == v1[a:a + 1, :], cnt[a:a + 1, :], c1)
        c1_ref[h, :, 0:t] = c1
        e1_ref[h, :, 0:t] = jnp.exp(s1 - v1[0:1, :]) * (1.0 / z)
        i2_ref[h, 0, 0:PEER_N_KEYS, 0:t] = r2.astype(BF)
        i2_ref[h, 1, 0:PEER_N_KEYS, 0:t] = jnp.exp(s2 - v2[0:1, :]).astype(BF)
        return carry

    lax.fori_loop(0, PEER_HEADS, head, 0)


def _peer_dense(at_ref, wt_ref, blk, t, c1_ref, e1_ref, i2_ref):
    i1_0 = pl.multiple_of(blk * PEER_I1_PER_BLK, PEER_I1_PER_BLK)
    zero = jnp.zeros((), BF)
    for lg in range(t // LANES):
        ls = slice(lg * LANES, (lg + 1) * LANES)
        c1_s = [c1_ref[h, pl.ds(i1_0, PEER_I1_PER_BLK), ls].astype(BF) for h in range(PEER_HEADS)]
        e1_s = [e1_ref[h, pl.ds(i1_0, PEER_I1_PER_BLK), ls].astype(BF) for h in range(PEER_HEADS)]
        for j in range(PEER_I1_PER_BLK):
            rs = slice(j * PEER_N_KEYS, (j + 1) * PEER_N_KEYS)
            g = jnp.zeros((PEER_N_KEYS, LANES), BF)
            for h in range(PEER_HEADS):
                c1 = jnp.broadcast_to(c1_s[h][j:j + 1, :], (PEER_N_KEYS, LANES))
                e1 = jnp.broadcast_to(e1_s[h][j:j + 1, :], (PEER_N_KEYS, LANES))
                gate1 = jnp.minimum(jnp.maximum(c1 - i2_ref[h, 0, 0:PEER_N_KEYS, ls], zero), e1)
                g = g + gate1 * i2_ref[h, 1, 0:PEER_N_KEYS, ls]
            wt_ref[rs, ls] = _gelu_tanh(at_ref[rs, ls].astype(BF)) * g


def _peer_kernel(hn_ref, x_ref, mod_ref, fg_ref, wq_ref, sk_ref, u_ref, vt_ref, o_ref,
                 acc_ref, at_ref, wt_ref, c1_ref, e1_ref, i2_ref, s_ref, v_ref, cand_ref, *, final):
    e = pl.program_id(1)
    t = hn_ref.shape[0]

    @pl.when(e == 0)
    def _():
        _peer_prologue(hn_ref, wq_ref, sk_ref, c1_ref, e1_ref, i2_ref, s_ref, v_ref, cand_ref)
        acc_ref[...] = jnp.zeros_like(acc_ref)

    at_ref[:, 0:t] = _dot_nt(u_ref[...], hn_ref[...])
    _peer_dense(at_ref, wt_ref, e, t, c1_ref, e1_ref, i2_ref)
    acc_ref[...] += _dot(vt_ref[...], wt_ref[:, 0:t])

    @pl.when(e == pl.num_programs(1) - 1)
    def _():
        y = x_ref[...] + mod_ref[5:6, :] * acc_ref[...].T
        if final:
            y = _rms(y) * fg_ref[...]
        o_ref[...] = y


def _peer(hn, x_mid, mod, mod_row0, tok_per_row, final_g, final, lw):
    ntok, d = hn.shape
    t = PEER_TOK
    n_blk = PEER_N_EXPERTS // PEER_EBLK
    if mod_row0 is None:
        assert tok_per_row % t == 0
        tiles_per_row = tok_per_row // t
        mod_spec = pl.BlockSpec((None, N_MOD, d), lambda i, e: (i // tiles_per_row, 0, 0))
    else:
        mod_spec = pl.BlockSpec((None, N_MOD, d), lambda i, e: (mod_row0, 0, 0))
    tw = t + LANES
    tab_f = pltpu.VMEM((PEER_HEADS, PEER_N_KEYS, tw), F32)
    tab_i2 = pltpu.VMEM((PEER_HEADS, 2, PEER_N_KEYS + 16, tw), BF)
    return pl.pallas_call(
        functools.partial(_peer_kernel, final=final),
        out_shape=jax.ShapeDtypeStruct((ntok, d), F32),
        grid=(ntok // t, n_blk),
        in_specs=[
            pl.BlockSpec((t, d), lambda i, e: (i, 0)),
            pl.BlockSpec((t, d), lambda i, e: (i, 0)),
            mod_spec,
            pl.BlockSpec((1, d), lambda i, e: (0, 0)),
            pl.BlockSpec(lw["peer_wq"].shape, lambda i, e: (0, 0, 0)),
            pl.BlockSpec(lw["peer_sk"].shape, lambda i, e: (0, 0, 0, 0)),
            pl.BlockSpec((PEER_EBLK, d), lambda i, e: (e, 0)),
            pl.BlockSpec((None, d, PEER_EBLK), lambda i, e: (e, 0, 0)),
        ],
        out_specs=pl.BlockSpec((t, d), lambda i, e: (i, 0)),
        scratch_shapes=[
            pltpu.VMEM((d, t), F32),
            pltpu.VMEM((PEER_EBLK, tw), F32),
            pltpu.VMEM((PEER_EBLK, tw), BF),
            tab_f, tab_f, tab_i2,
            pltpu.VMEM((2, PEER_N_KEYS, tw), F32),
            pltpu.VMEM((2, PEER_TOPK, tw), F32),
            pltpu.VMEM((PEER_CAND_ROWS, tw), F32),
        ],
        compiler_params=_cparams(("parallel", "arbitrary")),
        name="peer",
    )(hn, x_mid, mod, final_g, lw["peer_wq"], lw["peer_sk"], lw["peer_u"], lw["peer_vt"])


def _rope_tables(l):
    rows = l // GRID_W
    row = jnp.repeat(jnp.arange(rows), GRID_W).astype(F32)
    col = jnp.tile(jnp.arange(GRID_W), rows).astype(F32)

    def cs(rot_dim):
        n = rot_dim // 4
        inv = ROPE_BASE ** (-jnp.arange(n, dtype=F32) / n)
        ang = jnp.concatenate([row[:, None] * inv, col[:, None] * inv], axis=-1)
        return jnp.cos(ang), jnp.sin(ang)

    ones = lambda w: jnp.ones((l, w), F32)
    zeros = lambda w: jnp.zeros((l, w), F32)
    def block_tables(c, s):
        pad = ROLL_HALF - c.shape[1]
        return (jnp.concatenate([c, ones(pad), c, ones(pad)], axis=1),
                jnp.concatenate([-s, zeros(pad), s, zeros(pad)], axis=1))

    cos_a, sin_a = block_tables(*cs(HEAD_DIM))
    cos_c, sin_c = block_tables(*cs(MLA_ROPE_DIM))
    return cos_a, sin_a, cos_c, sin_c


def _peer_vt_blocks(v_tab):
    n, d = v_tab.shape
    return v_tab.astype(BF).reshape(n // PEER_EBLK, PEER_EBLK, d).transpose(0, 2, 1)


def _head_block(rot, nope):
    rows = (rot if rot is not None else nope).shape[0]
    n = 0 if rot is None else rot.shape[1] // 2
    m = 0 if nope is None else nope.shape[1]
    low = min(m, ROLL_HALF - n)
    parts = []
    if n:
        parts.append(rot[:, :n])
    parts.append(nope[:, :low] if m else jnp.zeros((rows, 0), F32))
    parts.append(jnp.zeros((rows, ROLL_HALF - n - low), F32))
    if n:
        parts.append(rot[:, n:])
    parts.append(nope[:, low:] if m else jnp.zeros((rows, 0), F32))
    parts.append(jnp.zeros((rows, ROLL_HALF - n - (m - low)), F32))
    blk = jnp.concatenate(parts, axis=1)
    assert blk.shape == (rows, LANES)
    return blk


def _layer_weights(layer, norm1_g, norm2_g, w_in, gqa_qn_g, gqa_kn_g, pool_w, pool_scale, mla_qn_g, mla_kvn_g,
                   mla_w_uq, mla_w_ukv, w_out, peer_wq, peer_subkeys, peer_u, peer_v):
    d = D_MODEL
    w = w_in[layer]
    o = 0
    aq = w[:, o:o + GQA_Q_HEADS * HEAD_DIM]; o += GQA_Q_HEADS * HEAD_DIM
    ak = w[:, o:o + GQA_KV_HEADS * HEAD_DIM]; o += GQA_KV_HEADS * HEAD_DIM
    av = w[:, o:o + GQA_KV_HEADS * HEAD_DIM]; o += GQA_KV_HEADS * HEAD_DIM
    wb = w[:, o:o + POOL_WIDTH]; o += POOL_WIDTH
    wcq = w[:, o:o + MLA_Q_RANK]; o += MLA_Q_RANK
    wckv = w[:, o:o + MLA_KV_RANK]; o += MLA_KV_RANK
    wkr = w[:, o:o + MLA_ROPE_DIM]
    cols = []
    for h in range(GQA_Q_HEADS):
        cols.append(_head_block(aq[:, h * HEAD_DIM:(h + 1) * HEAD_DIM], None))
    for g in range(GQA_KV_HEADS):
        cols.append(_head_block(ak[:, g * HEAD_DIM:(g + 1) * HEAD_DIM], None))
    for h in range(GQA_Q_HEADS):
        g = h // GQA_GROUP
        cols.append(av[:, g * HEAD_DIM:(g + 1) * HEAD_DIM])
    cols += [wb, wcq, wckv]
    kr_blk = _head_block(wkr, jnp.zeros((d, MLA_NOPE_DIM), F32))
    cols.append(kr_blk)
    w_wide = jnp.concatenate(cols, axis=1).astype(BF)
    assert w_wide.shape == (d, IN_WIDE)

    def head_gain(g):
        return _head_block(g[None, :], None)

    qd = MLA_NOPE_DIM + MLA_ROPE_DIM
    uq = mla_w_uq[layer]
    ukv = mla_w_ukv[layer]
    uq_cols, uk_cols, uv_cols = [], [], []
    zero_rot = jnp.zeros((MLA_KV_RANK, MLA_ROPE_DIM), F32)
    for h in range(MLA_HEADS):
        q_h = uq[:, h * qd:(h + 1) * qd]
        uq_cols.append(_head_block(q_h[:, MLA_NOPE_DIM:], q_h[:, :MLA_NOPE_DIM]))
        k0 = h * (MLA_NOPE_DIM + MLA_V_DIM)
        uk_cols.append(_head_block(zero_rot, ukv[:, k0:k0 + MLA_NOPE_DIM]))
        uv_cols += [ukv[:, k0 + MLA_NOPE_DIM:k0 + MLA_NOPE_DIM + MLA_V_DIM]]

    pw = pool_w[layer]
    w_bd = jnp.zeros((POOL_WIDTH, POOL_WIDTH), F32)
    for g in range(POOL_GROUPS):
        s = slice(g * POOL_GROUP_DIM, (g + 1) * POOL_GROUP_DIM)
        w_bd = w_bd.at[s, s].set(pw[g])

    wo = w_out[layer]
    na = GQA_Q_HEADS * HEAD_DIM
    w_o1 = jnp.concatenate([wo[0:na], wo[na + POOL_WIDTH:]], axis=0).astype(BF)
    w_o2 = wo[na:na + POOL_WIDTH].astype(BF)

    return {
        "n1_g": norm1_g[layer][None, :],
        "n2_g": norm2_g[layer][None, :],
        "w_in": w_wide,
        "gqa_qg": head_gain(gqa_qn_g[layer]),
        "gqa_kg": head_gain(gqa_kn_g[layer]),
        "mla_qg": mla_qn_g[layer][None, :],
        "mla_kvg": mla_kvn_g[layer][None, :],
        "w_uq": jnp.concatenate(uq_cols, axis=1).astype(BF),
        "w_uk": jnp.concatenate(uk_cols, axis=1).astype(BF),
        "w_uv": jnp.concatenate(uv_cols, axis=1).astype(BF),
        "pool_w": w_bd.astype(BF),
        "pool_scale": pool_scale[layer][None, :],
        "w_o1": w_o1,
        "w_o2": w_o2,
        "peer_wq": peer_wq[layer].reshape(d, PEER_HEADS, PEER_QUERY_DIM).transpose(1, 0, 2).astype(BF),
        "peer_sk": peer_subkeys[layer].astype(BF),
        "peer_u": peer_u[layer].astype(BF),
        "peer_vt": _peer_vt_blocks(peer_v[layer]),
    }


def _tile(l, pref):
    return pref if l % pref == 0 else l


def kernel(x, c, ctx, c_ctx, ada_w, ada_b, norm1_g, norm2_g, w_in, gqa_qn_g, gqa_kn_g, pool_w, pool_scale,
           mla_qn_g, mla_kvn_g, mla_w_uq, mla_w_ukv, w_out, peer_wq, peer_subkeys, peer_u, peer_v, final_g):
    b, l, d = x.shape
    lc = ctx.shape[1]
    depth = ada_w.shape[0]
    assert d == D_MODEL and l % GRID_W == 0
    assert (b * l) % PEER_TOK == 0 and (b * lc) % PEER_TOK == 0

    n_rows = -(-(b + 1) // 8) * 8
    cc = jnp.concatenate([c, c_ctx[None, :], jnp.zeros((n_rows - b - 1, d), F32)], axis=0)
    mod_all = _adaln(cc, ada_w, ada_b).reshape(depth, n_rows, N_MOD, d)

    rope_tabs = _rope_tables(l)
    tl_l, tl_c = _tile(l, 512), _tile(lc, 512)
    tq_l, tq_c = _tile(l, 256), _tile(lc, 256)
    fg = final_g[None, :]

    xl, xc = x, ctx
    for layer in range(depth):
        lw = _layer_weights(layer, norm1_g, norm2_g, w_in, gqa_qn_g, gqa_kn_g, pool_w, pool_scale, mla_qn_g,
                            mla_kvn_g, mla_w_uq, mla_w_ukv, w_out, peer_wq, peer_subkeys, peer_u, peer_v)
        mod = mod_all[layer]
        last = layer == depth - 1
        q_l, k_l, v_l, pb_l = _inproj(xl, mod, None, lw, rope_tabs, tl_l)
        q_c, k_c, v_c, pb_c = _inproj(xc, mod, b, lw, None, tl_c)

        att_l = _attn(q_l, [k_c, k_l], [v_c, v_l], tq_l)
        ob_l = _pool(pb_l, lw["pool_w"], lw["pool_scale"])
        xl_mid, hn_l = _outproj(xl, att_l, ob_l, mod, None, lw, tl_l)
        xl = _peer(hn_l.reshape(b * l, d), xl_mid.reshape(b * l, d), mod, None, l, fg, last, lw).reshape(b, l, d)

        if not last:
            att_c = _attn(q_c, [k_c], [v_c], tq_c)
            ob_c = _pool(pb_c, lw["pool_w"], lw["pool_scale"])
            xc_mid, hn_c = _outproj(xc, att_c, ob_c, mod, b, lw, tl_c)
            xc = _peer(hn_c.reshape(b * lc, d), xc_mid.reshape(b * lc, d), mod, b, lc, fg, False,
                       lw).reshape(b, lc, d)
    return xl
```

```python
import functools

import jax
import jax.numpy as jnp
from jax import lax
from jax.experimental import pallas as pl
from jax.experimental.pallas import tpu as pltpu

D_MODEL = 1024
GRID_W = 64
N_MOD = 6
EPS = 1e-6
ROPE_BASE = 10000.0
HEAD_DIM = 64
GQA_Q_HEADS = 6
GQA_KV_HEADS = 2
GQA_GROUP = GQA_Q_HEADS // GQA_KV_HEADS
POOL_GROUPS = 4
POOL_WINDOWS = (2, 4, 8, 16)
POOL_WIDTH = D_MODEL // 4
POOL_GROUP_DIM = POOL_WIDTH // POOL_GROUPS
MLA_HEADS = 6
MLA_NOPE_DIM = 64
MLA_ROPE_DIM = 32
MLA_V_DIM = 64
MLA_Q_RANK = 384
MLA_KV_RANK = 256
PEER_HEADS = 8
PEER_N_KEYS = 128
PEER_N_EXPERTS = PEER_N_KEYS * PEER_N_KEYS
PEER_TOPK = 16
PEER_QUERY_DIM = 256
PEER_HALF = PEER_QUERY_DIM // 2

LANES = 128
N_ATT_HEADS = GQA_Q_HEADS + MLA_HEADS
QK_WIDTH = N_ATT_HEADS * LANES
V_WIDTH = N_ATT_HEADS * HEAD_DIM
POOL_PAD = 16

_C_QA = 0
_C_KA = _C_QA + GQA_Q_HEADS * LANES
_C_VA = _C_KA + GQA_KV_HEADS * LANES
_C_B = _C_VA + GQA_Q_HEADS * HEAD_DIM
_C_CQ = _C_B + POOL_WIDTH
_C_CKV = _C_CQ + MLA_Q_RANK
_C_KR = _C_CKV + MLA_KV_RANK
IN_WIDE = _C_KR + LANES
ROLL_HALF = LANES // 2

PEER_TOK = 512
PEER_EBLK = 2048
PEER_I1_PER_BLK = PEER_EBLK // PEER_N_KEYS
PEER_CAND_ROWS = 2 * PEER_TOPK + 5 * 8
VMEM_LIMIT = 56 * 1024 * 1024
ATTN_VMEM_LIMIT = 60 * 1024 * 1024

BF = jnp.bfloat16
F32 = jnp.float32
LOG2_E = 1.4426950408889634
NEG_INF = float("-inf")


def _cparams(sem, vmem_limit=VMEM_LIMIT):
    return pltpu.CompilerParams(dimension_semantics=sem, vmem_limit_bytes=vmem_limit)


def _dot(a, b):
    return jnp.dot(a, b, preferred_element_type=F32)


def _dot_nt(a, b):
    return lax.dot_general(a, b, (((1,), (1,)), ((), ())), preferred_element_type=F32)


def _split_bf16(a):
    hi = a.astype(BF)
    lo = (a - hi.astype(F32)).astype(BF)
    return hi, lo


def _dot3(a, b):
    ah, al = _split_bf16(a)
    bh, bl = _split_bf16(b)
    return _dot(ah, bh) + _dot(ah, bl) + _dot(al, bh)


def _rms(x):
    return x * lax.rsqrt(jnp.mean(x * x, axis=-1, keepdims=True) + EPS)


def _adaln_kernel(c_ref, w_ref, b_ref, o_ref):
    c = c_ref[...]
    s = c * (1.0 / (1.0 + jnp.exp(-c)))
    o_ref[...] = _dot3(s, w_ref[...]) + b_ref[...]


def _adaln(cc, ada_w, ada_b):
    depth, d, nd = ada_w.shape
    r = cc.shape[0]
    nblk = nd // d
    return pl.pallas_call(
        _adaln_kernel,
        out_shape=jax.ShapeDtypeStruct((depth, r, nd), F32),
        grid=(depth, nblk),
        in_specs=[
            pl.BlockSpec((r, d), lambda l, j: (0, 0)),
            pl.BlockSpec((None, d, d), lambda l, j: (l, 0, j)),
            pl.BlockSpec((None, 1, d), lambda l, j: (l, 0, j)),
        ],
        out_specs=pl.BlockSpec((None, r, d), lambda l, j: (l, 0, j)),
        compiler_params=_cparams(("arbitrary", "arbitrary")),
        name="adaln",
    )(cc, ada_w, ada_b.reshape(depth, 1, nd))


def _rope_block(xb, cb, sb):
    return xb * cb + pltpu.roll(xb, ROLL_HALF, 1) * sb


def _inproj_kernel(*refs, rope):
    if rope:
        (x_ref, mod_ref, n1_ref, w_ref, qg_ref, kg_ref, cqg_ref, ckvg_ref, wuq_ref, wuk_ref, wuv_ref,
         ca_ref, sa_ref, cc_ref, sc_ref, q_ref, k_ref, v_ref, b_ref) = refs
    else:
        (x_ref, mod_ref, n1_ref, w_ref, qg_ref, kg_ref, cqg_ref, ckvg_ref, wuq_ref, wuk_ref, wuv_ref,
         q_ref, k_ref, v_ref, b_ref) = refs
    x = x_ref[...]
    shift = mod_ref[0:1, :]
    scale = mod_ref[1:2, :]
    h = _rms(x) * n1_ref[...] * (1.0 + scale) + shift
    p = _dot(h.astype(BF), w_ref[...])

    sa = HEAD_DIM ** -0.5 * LOG2_E
    sc = (MLA_NOPE_DIM + MLA_ROPE_DIM) ** -0.5 * LOG2_E

    def gqa_head(c0, g_ref):
        blk = p[:, c0:c0 + LANES]
        ms = jnp.sum(blk * blk, axis=-1, keepdims=True) * (1.0 / HEAD_DIM)
        y = blk * lax.rsqrt(ms + EPS) * g_ref[...]
        if rope:
            y = _rope_block(y, ca_ref[...], sa_ref[...])
        return y

    for hh in range(GQA_Q_HEADS):
        q_ref[:, hh * LANES:(hh + 1) * LANES] = (gqa_head(_C_QA + hh * LANES, qg_ref) * sa).astype(BF)
    for g in range(GQA_KV_HEADS):
        kb = gqa_head(_C_KA + g * LANES, kg_ref).astype(BF)
        for hh in range(g * GQA_GROUP, (g + 1) * GQA_GROUP):
            k_ref[:, hh * LANES:(hh + 1) * LANES] = kb
    nva = GQA_Q_HEADS * HEAD_DIM
    v_ref[:, 0:nva] = p[:, _C_VA:_C_VA + nva].astype(BF)
    b_ref[...] = p[:, _C_B:_C_B + POOL_WIDTH]

    cq = _rms(p[:, _C_CQ:_C_CQ + MLA_Q_RANK]) * cqg_ref[...]
    ckv = (_rms(p[:, _C_CKV:_C_CKV + MLA_KV_RANK]) * ckvg_ref[...]).astype(BF)
    qc = _dot(cq.astype(BF), wuq_ref[...])
    kc = _dot(ckv, wuk_ref[...])
    vc = _dot(ckv, wuv_ref[...])
    kr = p[:, _C_KR:_C_KR + LANES]
    if rope:
        kr = _rope_block(kr, cc_ref[...], sc_ref[...])
    base = GQA_Q_HEADS * LANES
    for hh in range(MLA_HEADS):
        qb = qc[:, hh * LANES:(hh + 1) * LANES]
        if rope:
            qb = _rope_block(qb, cc_ref[...], sc_ref[...])
        q_ref[:, base + hh * LANES:base + (hh + 1) * LANES] = (qb * sc).astype(BF)
        k_ref[:, base + hh * LANES:base + (hh + 1) * LANES] = (kc[:, hh * LANES:(hh + 1) * LANES] + kr).astype(BF)
    v_ref[:, nva:V_WIDTH] = vc.astype(BF)


def _inproj(x, mod, mod_row0, lw, rope_tabs, tl):
    b, l, d = x.shape
    rope = rope_tabs is not None
    grid = (b, l // tl)

    def full(a):
        nd = a.ndim
        return pl.BlockSpec(a.shape, lambda i, j: (0,) * nd)

    if mod_row0 is None:
        mod_spec = pl.BlockSpec((None, N_MOD, d), lambda i, j: (i, 0, 0))
    else:
        mod_spec = pl.BlockSpec((None, N_MOD, d), lambda i, j: (mod_row0, 0, 0))
    weights = [lw["n1_g"], lw["w_in"], lw["gqa_qg"], lw["gqa_kg"], lw["mla_qg"], lw["mla_kvg"],
               lw["w_uq"], lw["w_uk"], lw["w_uv"]]
    in_specs = [pl.BlockSpec((None, tl, d), lambda i, j: (i, j, 0)), mod_spec] + [full(a) for a in weights]
    args = [x, mod] + weights
    if rope:
        in_specs += [pl.BlockSpec((tl, LANES), lambda i, j: (j, 0)) for _ in range(4)]
        args += list(rope_tabs)
    def rows(w, dt):
        return jax.ShapeDtypeStruct((b, l, w), dt), pl.BlockSpec((None, tl, w), lambda i, j: (i, j, 0))

    outs = [rows(QK_WIDTH, BF), rows(QK_WIDTH, BF), rows(V_WIDTH, BF), rows(POOL_WIDTH, F32)]
    return pl.pallas_call(
        functools.partial(_inproj_kernel, rope=rope),
        out_shape=[o[0] for o in outs],
        grid=grid,
        in_specs=in_specs,
        out_specs=[o[1] for o in outs],
        compiler_params=_cparams(("parallel", "parallel")),
        name="inproj",
    )(*args)


def _attn_kernel(*refs, n_src):
    q_ref = refs[0]
    k_refs = refs[1:1 + n_src]
    v_refs = refs[1 + n_src:1 + 2 * n_src]
    o_ref = refs[1 + 2 * n_src]
    tq = q_ref.shape[0]
    lane = lax.broadcasted_iota(jnp.int32, (tq, LANES), 1)
    for jb in range(N_ATT_HEADS // 2):
        halves = []
        for n in (2 * jb, 2 * jb + 1):
            q = q_ref[:, n * LANES:(n + 1) * LANES]
            ss = [_dot_nt(q, k_ref[:, n * LANES:(n + 1) * LANES]) for k_ref in k_refs]
            m = ss[0].max(axis=-1, keepdims=True)
            for s in ss[1:]:
                m = jnp.maximum(m, s.max(axis=-1, keepdims=True))
            acc = None
            den = None
            for s, v_ref in zip(ss, v_refs):
                e = jnp.exp2(s - m)
                dsum = e.sum(axis=-1, keepdims=True)
                pv = _dot(e.astype(BF), v_ref[:, jb * LANES:(jb + 1) * LANES])
                acc = pv if acc is None else acc + pv
                den = dsum if den is None else den + dsum
            halves.append(acc * (1.0 / den))
        o = jnp.where(lane < HEAD_DIM, halves[0], halves[1])
        o_ref[:, jb * LANES:(jb + 1) * LANES] = o.astype(BF)


def _attn(q, ks, vs, tq):
    b, l, _ = q.shape
    n_src = len(ks)
    once = pl.Buffered(1)
    in_specs = [pl.BlockSpec((None, tq, QK_WIDTH), lambda i, j: (i, j, 0))]
    in_specs += [pl.BlockSpec((None, k.shape[1], QK_WIDTH), lambda i, j: (i, 0, 0), pipeline_mode=once) for k in ks]
    in_specs += [pl.BlockSpec((None, v.shape[1], V_WIDTH), lambda i, j: (i, 0, 0), pipeline_mode=once) for v in vs]
    return pl.pallas_call(
        functools.partial(_attn_kernel, n_src=n_src),
        out_shape=jax.ShapeDtypeStruct((b, l, V_WIDTH), BF),
        grid=(b, l // tq),
        in_specs=in_specs,
        out_specs=pl.BlockSpec((None, tq, V_WIDTH), lambda i, j: (i, j, 0)),
        compiler_params=_cparams(("parallel", "arbitrary"), ATTN_VMEM_LIMIT),
        name="attn",
    )(q, *ks, *vs)


def _pool_kernel(b_ref, w_ref, s_ref, o_ref, xp_ref):
    l = b_ref.shape[0]
    x = b_ref[...]
    zeros = jnp.zeros((POOL_PAD, POOL_WIDTH), F32)
    xp_ref[0:POOL_PAD, :] = zeros
    xp_ref[POOL_PAD + l:POOL_PAD + l + POOL_PAD, :] = zeros
    xp_ref[POOL_PAD:POOL_PAD + l, :] = x
    t = lax.broadcasted_iota(jnp.int32, (l, LANES), 0)
    lane = lax.broadcasted_iota(jnp.int32, (l, LANES), 1)
    outs = []
    for half in range(POOL_WIDTH // LANES):
        w_small = POOL_WINDOWS[2 * half]
        w_big = POOL_WINDOWS[2 * half + 1]
        lo, hi = half * LANES, (half + 1) * LANES

        def win(j):
            return xp_ref[pl.ds(POOL_PAD + j, l), lo:hi]

        s_small = None
        for j in range(-(w_small // 2), w_small // 2):
            s_small = win(j) if s_small is None else s_small + win(j)
        s_big = s_small
        for j in range(-(w_big // 2), w_big // 2):
            if not (-(w_small // 2) <= j < w_small // 2):
                s_big = s_big + win(j)

        def cnt(w):
            lo_i = jnp.maximum(t - w // 2, 0)
            hi_i = jnp.minimum(t - w // 2 + w, l)
            return (hi_i - lo_i).astype(F32)

        left = lane < POOL_GROUP_DIM
        s = jnp.where(left, s_small, s_big)
        c = jnp.where(left, cnt(w_small), cnt(w_big))
        outs.append(s / c - x[:, lo:hi])
    pooled = jnp.concatenate(outs, axis=1).astype(BF)
    o_ref[...] = (_dot(pooled, w_ref[...]) * s_ref[...]).astype(BF)


def _pool(pb, w_bd, pscale):
    b, l, w = pb.shape
    return pl.pallas_call(
        _pool_kernel,
        out_shape=jax.ShapeDtypeStruct((b, l, w), BF),
        grid=(b,),
        in_specs=[
            pl.BlockSpec((None, l, w), lambda i: (i, 0, 0)),
            pl.BlockSpec((w, w), lambda i: (0, 0)),
            pl.BlockSpec((1, w), lambda i: (0, 0)),
        ],
        out_specs=pl.BlockSpec((None, l, w), lambda i: (i, 0, 0)),
        scratch_shapes=[pltpu.VMEM((l + 2 * POOL_PAD, w), F32)],
        compiler_params=_cparams(("parallel",)),
        name="pool",
    )(pb, w_bd, pscale)


def _outproj_kernel(x_ref, att_ref, ob_ref, mod_ref, n2_ref, w1_ref, w2_ref, xo_ref, h_ref):
    y = _dot(att_ref[...], w1_ref[...]) + _dot(ob_ref[...], w2_ref[...])
    xn = x_ref[...] + mod_ref[2:3, :] * y
    xo_ref[...] = xn
    h = _rms(xn) * n2_ref[...] * (1.0 + mod_ref[4:5, :]) + mod_ref[3:4, :]
    h_ref[...] = h.astype(BF)


def _outproj(x, att, ob, mod, mod_row0, lw, tl):
    b, l, d = x.shape
    if mod_row0 is None:
        mod_spec = pl.BlockSpec((None, N_MOD, d), lambda i, j: (i, 0, 0))
    else:
        mod_spec = pl.BlockSpec((None, N_MOD, d), lambda i, j: (mod_row0, 0, 0))

    def tile(w):
        return pl.BlockSpec((None, tl, w), lambda i, j: (i, j, 0))

    def full(a):
        return pl.BlockSpec(a.shape, lambda i, j: (0, 0))

    return pl.pallas_call(
        _outproj_kernel,
        out_shape=[jax.ShapeDtypeStruct((b, l, d), F32), jax.ShapeDtypeStruct((b, l, d), BF)],
        grid=(b, l // tl),
        in_specs=[tile(d), tile(V_WIDTH), tile(POOL_WIDTH), mod_spec, full(lw["n2_g"]), full(lw["w_o1"]),
                  full(lw["w_o2"])],
        out_specs=[tile(d), tile(d)],
        compiler_params=_cparams(("parallel", "parallel")),
        name="outproj",
    )(x, att, ob, mod, lw["n2_g"], lw["w_o1"], lw["w_o2"])


def _gelu_tanh(x):
    return 0.5 * x * (1.0 + jnp.tanh(0.7978845608028654 * (x + 0.044715 * (x * x * x))))


def _sort16_network():
    def merge(lo, hi, r):
        step = r * 2
        if step < hi - lo:
            yield from merge(lo, hi, step)
            yield from merge(lo + r, hi, step)
            yield from [(i, i + r) for i in range(lo + r, hi - r, step)]
        else:
            yield (lo, lo + r)

    def sort(lo, hi):
        if hi - lo >= 1:
            mid = lo + (hi - lo) // 2
            yield from sort(lo, mid)
            yield from sort(mid + 1, hi)
            yield from merge(lo, hi, 1)

    return tuple(sort(0, PEER_TOPK - 1))


def _top16_desc(blocks):
    a = list(blocks)
    for i, j in _sort16_network():
        a[i], a[j] = jnp.maximum(a[i], a[j]), jnp.minimum(a[i], a[j])
    for shift in (4, 2, 1):
        c = [jnp.maximum(a[i], pltpu.roll(a[PEER_TOPK - 1 - i], shift, 0)) for i in range(PEER_TOPK)]
        stride = PEER_TOPK // 2
        while stride >= 1:
            for i in range(PEER_TOPK):
                if i & stride == 0:
                    c[i], c[i + stride] = jnp.maximum(c[i], c[i + stride]), jnp.minimum(c[i], c[i + stride])
            stride //= 2
        a = c
    return a


def _peer_prologue(hn_ref, wq_ref, sk_ref, c1_ref, e1_ref, i2_ref, s_ref, v_ref, cand_ref):
    t = hn_ref.shape[0]
    hn = hn_ref[...]

    sub8 = lax.broadcasted_iota(jnp.int32, (8, t), 0)
    sub16 = lax.broadcasted_iota(jnp.int32, (PEER_TOPK, t), 0)
    zeros8 = jnp.zeros((8, t), F32)

    def head(h, carry):
        qh = _dot(hn, wq_ref[h]).astype(BF)
        r2 = None
        for p in range(2):
            st = _dot_nt(sk_ref[h, p], qh[:, p * PEER_HALF:(p + 1) * PEER_HALF])
            s_ref[p, :, 0:t] = st
            top = _top16_desc([st[8 * v:8 * v + 8, :] for v in range(PEER_N_KEYS // 8)])
            for k in range(PEER_TOPK):
                v_ref[p, k:k + 1, 0:t] = top[k][0:1, :]
            if p == 1:
                r2 = jnp.zeros((PEER_N_KEYS, t), F32)
                for k in range(PEER_TOPK):
                    r2 = r2 + jnp.where(st < top[k][0:1, :], 1.0, 0.0)
        v1 = v_ref[0, :, 0:t]
        v2 = v_ref[1, :, 0:t]

        groups = [
            v1[0:1, :] + v2,
            jnp.where(sub16 >= 1, v1 + v2[0:1, :], NEG_INF),
            jnp.where(sub8 >= 1, v1[1:2, :] + v2[0:8, :], NEG_INF),
            jnp.where(sub8 >= 2, v1[0:8, :] + v2[1:2, :], NEG_INF),
            jnp.where((sub8 >= 2) & (sub8 <= 4), v1[2:3, :] + v2[0:8, :], NEG_INF),
            jnp.where((sub8 >= 3) & (sub8 <= 4), v1[0:8, :] + v2[2:3, :], NEG_INF),
            jnp.where(sub8 == 3, v1[3:4, :] + v2[0:8, :], NEG_INF),
        ]
        cand_ref[:, 0:t] = jnp.concatenate(groups, axis=0)
        c00 = v1[0:1, :] + v2[0:1, :]

        def below_max(k, m):
            c = cand_ref[:, 0:t]
            return jnp.where(c < m, c, NEG_INF).max(axis=0, keepdims=True)

        tau = lax.fori_loop(1, PEER_TOPK, below_max, c00)
        c = cand_ref[:, 0:t]
        sel = c >= tau
        z = jnp.where(sel, jnp.exp(c - c00), 0.0).sum(axis=0, keepdims=True)
        ind = jnp.where(sel, 1.0, 0.0)

        def row_total(r0, rows, a):
            return jnp.where(sub16 == a, ind[r0:r0 + rows, :].sum(axis=0, keepdims=True), 0.0)

        cnt = (ind[16:32, :] + jnp.concatenate([ind[40:48, :] + ind[56:64, :], zeros8], axis=0)
               + row_total(0, 16, 0) + row_total(32, 8, 1) + row_total(48, 8, 2) + row_total(64, 8, 3))
        s1 = s_ref[0, :, 0:t]
        s2 = s_ref[1, :, 0:t]
        c1 = jnp.zeros((PEER_N_KEYS, t), F32)
        for a in range(PEER_TOPK):
            c1 = jnp.where(s1 == v1[a:a + 1, :], cnt[a:a + 1, :], c1)
        c1_ref[h, :, 0:t] = c1
        e1_ref[h, :, 0:t] = jnp.exp(s1 - v1[0:1, :]) * (1.0 / z)
        i2_ref[h, 0, 0:PEER_N_KEYS, 0:t] = r2.astype(BF)
        i2_ref[h, 1, 0:PEER_N_KEYS, 0:t] = jnp.exp(s2 - v2[0:1, :]).astype(BF)
        return carry

    lax.fori_loop(0, PEER_HEADS, head, 0)


def _peer_dense(at_ref, wt_ref, blk, t, c1_ref, e1_ref, i2_ref):
    i1_0 = pl.multiple_of(blk * PEER_I1_PER_BLK, PEER_I1_PER_BLK)
    zero = jnp.zeros((), BF)
    for lg in range(t // LANES):
        ls = slice(lg * LANES, (lg + 1) * LANES)
        c1_s = [c1_ref[h, pl.ds(i1_0, PEER_I1_PER_BLK), ls].astype(BF) for h in range(PEER_HEADS)]
        e1_s = [e1_ref[h, pl.ds(i1_0, PEER_I1_PER_BLK), ls].astype(BF) for h in range(PEER_HEADS)]
        for j in range(PEER_I1_PER_BLK):
            rs = slice(j * PEER_N_KEYS, (j + 1) * PEER_N_KEYS)
            g = jnp.zeros((PEER_N_KEYS, LANES), BF)
            for h in range(PEER_HEADS):
                c1 = jnp.broadcast_to(c1_s[h][j:j + 1, :], (PEER_N_KEYS, LANES))
                e1 = jnp.broadcast_to(e1_s[h][j:j + 1, :], (PEER_N_KEYS, LANES))
                gate1 = jnp.minimum(jnp.maximum(c1 - i2_ref[h, 0, 0:PEER_N_KEYS, ls], zero), e1)
                g = g + gate1 * i2_ref[h, 1, 0:PEER_N_KEYS, ls]
            wt_ref[rs, ls] = _gelu_tanh(at_ref[rs, ls].astype(BF)) * g


def _peer_kernel(hn_ref, x_ref, mod_ref, fg_ref, wq_ref, sk_ref, u_ref, vt_ref, o_ref,
                 acc_ref, at_ref, wt_ref, c1_ref, e1_ref, i2_ref, s_ref, v_ref, cand_ref, *, final):
    e = pl.program_id(1)
    t = hn_ref.shape[0]

    @pl.when(e == 0)
    def _():
        _peer_prologue(hn_ref, wq_ref, sk_ref, c1_ref, e1_ref, i2_ref, s_ref, v_ref, cand_ref)
        acc_ref[...] = jnp.zeros_like(acc_ref)

    at_ref[:, 0:t] = _dot_nt(u_ref[...], hn_ref[...])
    _peer_dense(at_ref, wt_ref, e, t, c1_ref, e1_ref, i2_ref)
    acc_ref[...] += _dot(vt_ref[...], wt_ref[:, 0:t])

    @pl.when(e == pl.num_programs(1) - 1)
    def _():
        y = x_ref[...] + mod_ref[5:6, :] * acc_ref[...].T
        if final:
            y = _rms(y) * fg_ref[...]
        o_ref[...] = y


def _peer(hn, x_mid, mod, mod_row0, tok_per_row, final_g, final, lw):
    ntok, d = hn.shape
    t = PEER_TOK
    n_blk = PEER_N_EXPERTS // PEER_EBLK
    if mod_row0 is None:
        assert tok_per_row % t == 0
        tiles_per_row = tok_per_row // t
        mod_spec = pl.BlockSpec((None, N_MOD, d), lambda i, e: (i // tiles_per_row, 0, 0))
    else:
        mod_spec = pl.BlockSpec((None, N_MOD, d), lambda i, e: (mod_row0, 0, 0))
    tw = t + LANES
    tab_f = pltpu.VMEM((PEER_HEADS, PEER_N_KEYS, tw), F32)
    tab_i2 = pltpu.VMEM((PEER_HEADS, 2, PEER_N_KEYS + 16, tw), BF)
    return pl.pallas_call(
        functools.partial(_peer_kernel, final=final),
        out_shape=jax.ShapeDtypeStruct((ntok, d), F32),
        grid=(ntok // t, n_blk),
        in_specs=[
            pl.BlockSpec((t, d), lambda i, e: (i, 0)),
            pl.BlockSpec((t, d), lambda i, e: (i, 0)),
            mod_spec,
            pl.BlockSpec((1, d), lambda i, e: (0, 0)),
            pl.BlockSpec(lw["peer_wq"].shape, lambda i, e: (0, 0, 0)),
            pl.BlockSpec(lw["peer_sk"].shape, lambda i, e: (0, 0, 0, 0)),
            pl.BlockSpec((PEER_EBLK, d), lambda i, e: (e, 0)),
            pl.BlockSpec((None, d, PEER_EBLK), lambda i, e: (e, 0, 0)),
        ],
        out_specs=pl.BlockSpec((t, d), lambda i, e: (i, 0)),
        scratch_shapes=[
            pltpu.VMEM((d, t), F32),
            pltpu.VMEM((PEER_EBLK, tw), F32),
            pltpu.VMEM((PEER_EBLK, tw), BF),
            tab_f, tab_f, tab_i2,
            pltpu.VMEM((2, PEER_N_KEYS, tw), F32),
            pltpu.VMEM((2, PEER_TOPK, tw), F32),
            pltpu.VMEM((PEER_CAND_ROWS, tw), F32),
        ],
        compiler_params=_cparams(("parallel", "arbitrary")),
        name="peer",
    )(hn, x_mid, mod, final_g, lw["peer_wq"], lw["peer_sk"], lw["peer_u"], lw["peer_vt"])


def _rope_tables(l):
    rows = l // GRID_W
    row = jnp.repeat(jnp.arange(rows), GRID_W).astype(F32)
    col = jnp.tile(jnp.arange(GRID_W), rows).astype(F32)

    def cs(rot_dim):
        n = rot_dim // 4
        inv = ROPE_BASE ** (-jnp.arange(n, dtype=F32) / n)
        ang = jnp.concatenate([row[:, None] * inv, col[:, None] * inv], axis=-1)
        return jnp.cos(ang), jnp.sin(ang)

    ones = lambda w: jnp.ones((l, w), F32)
    zeros = lambda w: jnp.zeros((l, w), F32)
    def block_tables(c, s):
        pad = ROLL_HALF - c.shape[1]
        return (jnp.concatenate([c, ones(pad), c, ones(pad)], axis=1),
                jnp.concatenate([-s, zeros(pad), s, zeros(pad)], axis=1))

    cos_a, sin_a = block_tables(*cs(HEAD_DIM))
    cos_c, sin_c = block_tables(*cs(MLA_ROPE_DIM))
    return cos_a, sin_a, cos_c, sin_c


def _peer_vt_blocks(v_tab):
    n, d = v_tab.shape
    return v_tab.astype(BF).reshape(n // PEER_EBLK, PEER_EBLK, d).transpose(0, 2, 1)


def _head_block(rot, nope):
    rows = (rot if rot is not None else nope).shape[0]
    n = 0 if rot is None else rot.shape[1] // 2
    m = 0 if nope is None else nope.shape[1]
    low = min(m, ROLL_HALF - n)
    parts = []
    if n:
        parts.append(rot[:, :n])
    parts.append(nope[:, :low] if m else jnp.zeros((rows, 0), F32))
    parts.append(jnp.zeros((rows, ROLL_HALF - n - low), F32))
    if n:
        parts.append(rot[:, n:])
    parts.append(nope[:, low:] if m else jnp.zeros((rows, 0), F32))
    parts.append(jnp.zeros((rows, ROLL_HALF - n - (m - low)), F32))
    blk = jnp.concatenate(parts, axis=1)
    assert blk.shape == (rows, LANES)
    return blk


def _layer_weights(layer, norm1_g, norm2_g, w_in, gqa_qn_g, gqa_kn_g, pool_w, pool_scale, mla_qn_g, mla_kvn_g,
                   mla_w_uq, mla_w_ukv, w_out, peer_wq, peer_subkeys, peer_u, peer_v):
    d = D_MODEL
    w = w_in[layer]
    o = 0
    aq = w[:, o:o + GQA_Q_HEADS * HEAD_DIM]; o += GQA_Q_HEADS * HEAD_DIM
    ak = w[:, o:o + GQA_KV_HEADS * HEAD_DIM]; o += GQA_KV_HEADS * HEAD_DIM
    av = w[:, o:o + GQA_KV_HEADS * HEAD_DIM]; o += GQA_KV_HEADS * HEAD_DIM
    wb = w[:, o:o + POOL_WIDTH]; o += POOL_WIDTH
    wcq = w[:, o:o + MLA_Q_RANK]; o += MLA_Q_RANK
    wckv = w[:, o:o + MLA_KV_RANK]; o += MLA_KV_RANK
    wkr = w[:, o:o + MLA_ROPE_DIM]
    cols = []
    for h in range(GQA_Q_HEADS):
        cols.append(_head_block(aq[:, h * HEAD_DIM:(h + 1) * HEAD_DIM], None))
    for g in range(GQA_KV_HEADS):
        cols.append(_head_block(ak[:, g * HEAD_DIM:(g + 1) * HEAD_DIM], None))
    for h in range(GQA_Q_HEADS):
        g = h // GQA_GROUP
        cols.append(av[:, g * HEAD_DIM:(g + 1) * HEAD_DIM])
    cols += [wb, wcq, wckv]
    kr_blk = _head_block(wkr, jnp.zeros((d, MLA_NOPE_DIM), F32))
    cols.append(kr_blk)
    w_wide = jnp.concatenate(cols, axis=1).astype(BF)
    assert w_wide.shape == (d, IN_WIDE)

    def head_gain(g):
        return _head_block(g[None, :], None)

    qd = MLA_NOPE_DIM + MLA_ROPE_DIM
    uq = mla_w_uq[layer]
    ukv = mla_w_ukv[layer]
    uq_cols, uk_cols, uv_cols = [], [], []
    zero_rot = jnp.zeros((MLA_KV_RANK, MLA_ROPE_DIM), F32)
    for h in range(MLA_HEADS):
        q_h = uq[:, h * qd:(h + 1) * qd]
        uq_cols.append(_head_block(q_h[:, MLA_NOPE_DIM:], q_h[:, :MLA_NOPE_DIM]))
        k0 = h * (MLA_NOPE_DIM + MLA_V_DIM)
        uk_cols.append(_head_block(zero_rot, ukv[:, k0:k0 + MLA_NOPE_DIM]))
        uv_cols += [ukv[:, k0 + MLA_NOPE_DIM:k0 + MLA_NOPE_DIM + MLA_V_DIM]]

    pw = pool_w[layer]
    w_bd = jnp.zeros((POOL_WIDTH, POOL_WIDTH), F32)
    for g in range(POOL_GROUPS):
        s = slice(g * POOL_GROUP_DIM, (g + 1) * POOL_GROUP_DIM)
        w_bd = w_bd.at[s, s].set(pw[g])

    wo = w_out[layer]
    na = GQA_Q_HEADS * HEAD_DIM
    w_o1 = jnp.concatenate([wo[0:na], wo[na + POOL_WIDTH:]], axis=0).astype(BF)
    w_o2 = wo[na:na + POOL_WIDTH].astype(BF)

    return {
        "n1_g": norm1_g[layer][None, :],
        "n2_g": norm2_g[layer][None, :],
        "w_in": w_wide,
        "gqa_qg": head_gain(gqa_qn_g[layer]),
        "gqa_kg": head_gain(gqa_kn_g[layer]),
        "mla_qg": mla_qn_g[layer][None, :],
        "mla_kvg": mla_kvn_g[layer][None, :],
        "w_uq": jnp.concatenate(uq_cols, axis=1).astype(BF),
        "w_uk": jnp.concatenate(uk_cols, axis=1).astype(BF),
        "w_uv": jnp.concatenate(uv_cols, axis=1).astype(BF),
        "pool_w": w_bd.astype(BF),
        "pool_scale": pool_scale[layer][None, :],
        "w_o1": w_o1,
        "w_o2": w_o2,
        "peer_wq": peer_wq[layer].reshape(d, PEER_HEADS, PEER_QUERY_DIM).transpose(1, 0, 2).astype(BF),
        "peer_sk": peer_subkeys[layer].astype(BF),
        "peer_u": peer_u[layer].astype(BF),
        "peer_vt": _peer_vt_blocks(peer_v[layer]),
    }


def _tile(l, pref):
    return pref if l % pref == 0 else l


def kernel(x, c, ctx, c_ctx, ada_w, ada_b, norm1_g, norm2_g, w_in, gqa_qn_g, gqa_kn_g, pool_w, pool_scale,
           mla_qn_g, mla_kvn_g, mla_w_uq, mla_w_ukv, w_out, peer_wq, peer_subkeys, peer_u, peer_v, final_g):
    b, l, d = x.shape
    lc = ctx.shape[1]
    depth = ada_w.shape[0]
    assert d == D_MODEL and l % GRID_W == 0
    assert (b * l) % PEER_TOK == 0 and (b * lc) % PEER_TOK == 0

    n_rows = -(-(b + 1) // 8) * 8
    cc = jnp.concatenate([c, c_ctx[None, :], jnp.zeros((n_rows - b - 1, d), F32)], axis=0)
    mod_all = _adaln(cc, ada_w, ada_b).reshape(depth, n_rows, N_MOD, d)

    rope_tabs = _rope_tables(l)
    tl_l, tl_c = _tile(l, 512), _tile(lc, 512)
    tq_l, tq_c = _tile(l, 512), _tile(lc, 256)
    fg = final_g[None, :]

    xl, xc = x, ctx
    for layer in range(depth):
        lw = _layer_weights(layer, norm1_g, norm2_g, w_in, gqa_qn_g, gqa_kn_g, pool_w, pool_scale, mla_qn_g,
                            mla_kvn_g, mla_w_uq, mla_w_ukv, w_out, peer_wq, peer_subkeys, peer_u, peer_v)
        mod = mod_all[layer]
        last = layer == depth - 1
        q_l, k_l, v_l, pb_l = _inproj(xl, mod, None, lw, rope_tabs, tl_l)
        q_c, k_c, v_c, pb_c = _inproj(xc, mod, b, lw, None, tl_c)

        att_l = _attn(q_l, [k_c, k_l], [v_c, v_l], tq_l)
        ob_l = _pool(pb_l, lw["pool_w"], lw["pool_scale"])
        xl_mid, hn_l = _outproj(xl, att_l, ob_l, mod, None, lw, tl_l)
        xl = _peer(hn_l.reshape(b * l, d), xl_mid.reshape(b * l, d), mod, None, l, fg, last, lw).reshape(b, l, d)

        if not last:
            att_c = _attn(q_c, [k_c], [v_c], tq_c)
            ob_c = _pool(pb_c, lw["pool_w"], lw["pool_scale"])
            xc_mid, hn_c = _outproj(xc, att_c, ob_c, mod, b, lw, tl_c)
            xc = _peer(hn_c.reshape(b * lc, d), xc_mid.reshape(b * lc, d), mod, b, lc, fg, False,
                       lw).reshape(b, lc, d)
    return xl
```

```python
import functools

import jax
import jax.numpy as jnp
from jax import lax
from jax.experimental import pallas as pl
from jax.experimental.pallas import tpu as pltpu

D_MODEL = 1024
GRID_W = 64
N_MOD = 6
EPS = 1e-6
ROPE_BASE = 10000.0
HEAD_DIM = 64
GQA_Q_HEADS = 6
GQA_KV_HEADS = 2
GQA_GROUP = GQA_Q_HEADS // GQA_KV_HEADS
POOL_GROUPS = 4
POOL_WINDOWS = (2, 4, 8, 16)
POOL_WIDTH = D_MODEL // 4
POOL_GROUP_DIM = POOL_WIDTH // POOL_GROUPS
MLA_HEADS = 6
MLA_NOPE_DIM = 64
MLA_ROPE_DIM = 32
MLA_V_DIM = 64
MLA_Q_RANK = 384
MLA_KV_RANK = 256
PEER_HEADS = 8
PEER_N_KEYS = 128
PEER_N_EXPERTS = PEER_N_KEYS * PEER_N_KEYS
PEER_TOPK = 16
PEER_QUERY_DIM = 256
PEER_HALF = PEER_QUERY_DIM // 2

LANES = 128
N_ATT_HEADS = GQA_Q_HEADS + MLA_HEADS
QK_WIDTH = N_ATT_HEADS * LANES
V_WIDTH = N_ATT_HEADS * HEAD_DIM
POOL_PAD = 16

_C_QA = 0
_C_KA = _C_QA + GQA_Q_HEADS * LANES
_C_VA = _C_KA + GQA_KV_HEADS * LANES
_C_B = _C_VA + GQA_Q_HEADS * HEAD_DIM
_C_CQ = _C_B + POOL_WIDTH
_C_CKV = _C_CQ + MLA_Q_RANK
_C_KR = _C_CKV + MLA_KV_RANK
IN_WIDE = _C_KR + LANES
ROLL_HALF = LANES // 2

PEER_TOK = 512
PEER_EBLK = 2048
PEER_I1_PER_BLK = PEER_EBLK // PEER_N_KEYS
PEER_CAND_ROWS = 2 * PEER_TOPK + 5 * 8
VMEM_LIMIT = 56 * 1024 * 1024
ATTN_VMEM_LIMIT = 60 * 1024 * 1024

BF = jnp.bfloat16
F32 = jnp.float32
LOG2_E = 1.4426950408889634
NEG_INF = float("-inf")


def _cparams(sem, vmem_limit=VMEM_LIMIT):
    return pltpu.CompilerParams(dimension_semantics=sem, vmem_limit_bytes=vmem_limit)


def _dot(a, b):
    return jnp.dot(a, b, preferred_element_type=F32)


def _dot_nt(a, b):
    return lax.dot_general(a, b, (((1,), (1,)), ((), ())), preferred_element_type=F32)


def _split_bf16(a):
    hi = a.astype(BF)
    lo = (a - hi.astype(F32)).astype(BF)
    return hi, lo


def _dot3(a, b):
    ah, al = _split_bf16(a)
    bh, bl = _split_bf16(b)
    return _dot(ah, bh) + _dot(ah, bl) + _dot(al, bh)


def _rms(x):
    return x * lax.rsqrt(jnp.mean(x * x, axis=-1, keepdims=True) + EPS)


def _adaln_kernel(c_ref, w_ref, b_ref, o_ref):
    c = c_ref[...]
    s = c * (1.0 / (1.0 + jnp.exp(-c)))
    o_ref[...] = _dot3(s, w_ref[...]) + b_ref[...]


def _adaln(cc, ada_w, ada_b):
    depth, d, nd = ada_w.shape
    r = cc.shape[0]
    nblk = nd // d
    return pl.pallas_call(
        _adaln_kernel,
        out_shape=jax.ShapeDtypeStruct((depth, r, nd), F32),
        grid=(depth, nblk),
        in_specs=[
            pl.BlockSpec((r, d), lambda l, j: (0, 0)),
            pl.BlockSpec((None, d, d), lambda l, j: (l, 0, j)),
            pl.BlockSpec((None, 1, d), lambda l, j: (l, 0, j)),
        ],
        out_specs=pl.BlockSpec((None, r, d), lambda l, j: (l, 0, j)),
        compiler_params=_cparams(("arbitrary", "arbitrary")),
        name="adaln",
    )(cc, ada_w, ada_b.reshape(depth, 1, nd))


def _rope_block(xb, cb, sb):
    return xb * cb + pltpu.roll(xb, ROLL_HALF, 1) * sb


def _inproj_kernel(*refs, rope):
    if rope:
        (x_ref, mod_ref, n1_ref, w_ref, qg_ref, kg_ref, cqg_ref, ckvg_ref, wuq_ref, wuk_ref, wuv_ref,
         ca_ref, sa_ref, cc_ref, sc_ref, q_ref, k_ref, v_ref, b_ref) = refs
    else:
        (x_ref, mod_ref, n1_ref, w_ref, qg_ref, kg_ref, cqg_ref, ckvg_ref, wuq_ref, wuk_ref, wuv_ref,
         q_ref, k_ref, v_ref, b_ref) = refs
    x = x_ref[...]
    shift = mod_ref[0:1, :]
    scale = mod_ref[1:2, :]
    h = _rms(x) * n1_ref[...] * (1.0 + scale) + shift
    p = _dot(h.astype(BF), w_ref[...])

    sa = HEAD_DIM ** -0.5 * LOG2_E
    sc = (MLA_NOPE_DIM + MLA_ROPE_DIM) ** -0.5 * LOG2_E

    def gqa_head(c0, g_ref):
        blk = p[:, c0:c0 + LANES]
        ms = jnp.sum(blk * blk, axis=-1, keepdims=True) * (1.0 / HEAD_DIM)
        y = blk * lax.rsqrt(ms + EPS) * g_ref[...]
        if rope:
            y = _rope_block(y, ca_ref[...], sa_ref[...])
        return y

    for hh in range(GQA_Q_HEADS):
        q_ref[:, hh * LANES:(hh + 1) * LANES] = (gqa_head(_C_QA + hh * LANES, qg_ref) * sa).astype(BF)
    for g in range(GQA_KV_HEADS):
        kb = gqa_head(_C_KA + g * LANES, kg_ref).astype(BF)
        for hh in range(g * GQA_GROUP, (g + 1) * GQA_GROUP):
            k_ref[:, hh * LANES:(hh + 1) * LANES] = kb
    nva = GQA_Q_HEADS * HEAD_DIM
    v_ref[:, 0:nva] = p[:, _C_VA:_C_VA + nva].astype(BF)
    b_ref[...] = p[:, _C_B:_C_B + POOL_WIDTH]

    cq = _rms(p[:, _C_CQ:_C_CQ + MLA_Q_RANK]) * cqg_ref[...]
    ckv = (_rms(p[:, _C_CKV:_C_CKV + MLA_KV_RANK]) * ckvg_ref[...]).astype(BF)
    qc = _dot(cq.astype(BF), wuq_ref[...])
    kc = _dot(ckv, wuk_ref[...])
    vc = _dot(ckv, wuv_ref[...])
    kr = p[:, _C_KR:_C_KR + LANES]
    if rope:
        kr = _rope_block(kr, cc_ref[...], sc_ref[...])
    base = GQA_Q_HEADS * LANES
    for hh in range(MLA_HEADS):
        qb = qc[:, hh * LANES:(hh + 1) * LANES]
        if rope:
            qb = _rope_block(qb, cc_ref[...], sc_ref[...])
        q_ref[:, base + hh * LANES:base + (hh + 1) * LANES] = (qb * sc).astype(BF)
        k_ref[:, base + hh * LANES:base + (hh + 1) * LANES] = (kc[:, hh * LANES:(hh + 1) * LANES] + kr).astype(BF)
    v_ref[:, nva:V_WIDTH] = vc.astype(BF)


def _inproj(x, mod, mod_row0, lw, rope_tabs, tl):
    b, l, d = x.shape
    rope = rope_tabs is not None
    grid = (b, l // tl)

    def full(a):
        nd = a.ndim
        return pl.BlockSpec(a.shape, lambda i, j: (0,) * nd)

    if mod_row0 is None:
        mod_spec = pl.BlockSpec((None, N_MOD, d), lambda i, j: (i, 0, 0))
    else:
        mod_spec = pl.BlockSpec((None, N_MOD, d), lambda i, j: (mod_row0, 0, 0))
    weights = [lw["n1_g"], lw["w_in"], lw["gqa_qg"], lw["gqa_kg"], lw["mla_qg"], lw["mla_kvg"],
               lw["w_uq"], lw["w_uk"], lw["w_uv"]]
    in_specs = [pl.BlockSpec((None, tl, d), lambda i, j: (i, j, 0)), mod_spec] + [full(a) for a in weights]
    args = [x, mod] + weights
    if rope:
        in_specs += [pl.BlockSpec((tl, LANES), lambda i, j: (j, 0)) for _ in range(4)]
        args += list(rope_tabs)
    def rows(w, dt):
        return jax.ShapeDtypeStruct((b, l, w), dt), pl.BlockSpec((None, tl, w), lambda i, j: (i, j, 0))

    outs = [rows(QK_WIDTH, BF), rows(QK_WIDTH, BF), rows(V_WIDTH, BF), rows(POOL_WIDTH, F32)]
    return pl.pallas_call(
        functools.partial(_inproj_kernel, rope=rope),
        out_shape=[o[0] for o in outs],
        grid=grid,
        in_specs=in_specs,
        out_specs=[o[1] for o in outs],
        compiler_params=_cparams(("parallel", "parallel")),
        name="inproj",
    )(*args)


def _attn_kernel(*refs, n_src):
    q_ref = refs[0]
    k_refs = refs[1:1 + n_src]
    v_refs = refs[1 + n_src:1 + 2 * n_src]
    o_ref = refs[1 + 2 * n_src]
    tq = q_ref.shape[0]
    lane = lax.broadcasted_iota(jnp.int32, (tq, LANES), 1)
    for jb in range(N_ATT_HEADS // 2):
        halves = []
        for n in (2 * jb, 2 * jb + 1):
            q = q_ref[:, n * LANES:(n + 1) * LANES]
            ss = [_dot_nt(q, k_ref[:, n * LANES:(n + 1) * LANES]) for k_ref in k_refs]
            m = ss[0].max(axis=-1, keepdims=True)
            for s in ss[1:]:
                m = jnp.maximum(m, s.max(axis=-1, keepdims=True))
            acc = None
            den = None
            for s, v_ref in zip(ss, v_refs):
                e = jnp.exp2(s - m)
                dsum = e.sum(axis=-1, keepdims=True)
                pv = _dot(e.astype(BF), v_ref[:, jb * LANES:(jb + 1) * LANES])
                acc = pv if acc is None else acc + pv
                den = dsum if den is None else den + dsum
            halves.append(acc * (1.0 / den))
        o = jnp.where(lane < HEAD_DIM, halves[0], halves[1])
        o_ref[:, jb * LANES:(jb + 1) * LANES] = o.astype(BF)


def _attn(q, ks, vs, tq):
    b, l, _ = q.shape
    n_src = len(ks)
    once = pl.Buffered(1)
    in_specs = [pl.BlockSpec((None, tq, QK_WIDTH), lambda i, j: (i, j, 0))]
    in_specs += [pl.BlockSpec((None, k.shape[1], QK_WIDTH), lambda i, j: (i, 0, 0), pipeline_mode=once) for k in ks]
    in_specs += [pl.BlockSpec((None, v.shape[1], V_WIDTH), lambda i, j: (i, 0, 0), pipeline_mode=once) for v in vs]
    return pl.pallas_call(
        functools.partial(_attn_kernel, n_src=n_src),
        out_shape=jax.ShapeDtypeStruct((b, l, V_WIDTH), BF),
        grid=(b, l // tq),
        in_specs=in_specs,
        out_specs=pl.BlockSpec((None, tq, V_WIDTH), lambda i, j: (i, j, 0)),
        compiler_params=_cparams(("parallel", "arbitrary"), ATTN_VMEM_LIMIT),
        name="attn",
    )(q, *ks, *vs)


def _pool_kernel(b_ref, w_ref, s_ref, o_ref, xp_ref):
    l = b_ref.shape[0]
    x = b_ref[...]
    zeros = jnp.zeros((POOL_PAD, POOL_WIDTH), F32)
    xp_ref[0:POOL_PAD, :] = zeros
    xp_ref[POOL_PAD + l:POOL_PAD + l + POOL_PAD, :] = zeros
    xp_ref[POOL_PAD:POOL_PAD + l, :] = x
    t = lax.broadcasted_iota(jnp.int32, (l, LANES), 0)
    lane = lax.broadcasted_iota(jnp.int32, (l, LANES), 1)
    outs = []
    for half in range(POOL_WIDTH // LANES):
        w_small = POOL_WINDOWS[2 * half]
        w_big = POOL_WINDOWS[2 * half + 1]
        lo, hi = half * LANES, (half + 1) * LANES

        def win(j):
            return xp_ref[pl.ds(POOL_PAD + j, l), lo:hi]

        s_small = None
        for j in range(-(w_small // 2), w_small // 2):
            s_small = win(j) if s_small is None else s_small + win(j)
        s_big = s_small
        for j in range(-(w_big // 2), w_big // 2):
            if not (-(w_small // 2) <= j < w_small // 2):
                s_big = s_big + win(j)

        def cnt(w):
            lo_i = jnp.maximum(t - w // 2, 0)
            hi_i = jnp.minimum(t - w // 2 + w, l)
            return (hi_i - lo_i).astype(F32)

        left = lane < POOL_GROUP_DIM
        s = jnp.where(left, s_small, s_big)
        c = jnp.where(left, cnt(w_small), cnt(w_big))
        outs.append(s / c - x[:, lo:hi])
    pooled = jnp.concatenate(outs, axis=1).astype(BF)
    o_ref[...] = (_dot(pooled, w_ref[...]) * s_ref[...]).astype(BF)


def _pool(pb, w_bd, pscale):
    b, l, w = pb.shape
    return pl.pallas_call(
        _pool_kernel,
        out_shape=jax.ShapeDtypeStruct((b, l, w), BF),
        grid=(b,),
        in_specs=[
            pl.BlockSpec((None, l, w), lambda i: (i, 0, 0)),
            pl.BlockSpec((w, w), lambda i: (0, 0)),
            pl.BlockSpec((1, w), lambda i: (0, 0)),
        ],
        out_specs=pl.BlockSpec((None, l, w), lambda i: (i, 0, 0)),
        scratch_shapes=[pltpu.VMEM((l + 2 * POOL_PAD, w), F32)],
        compiler_params=_cparams(("parallel",)),
        name="pool",
    )(pb, w_bd, pscale)


def _outproj_kernel(x_ref, att_ref, ob_ref, mod_ref, n2_ref, w1_ref, w2_ref, xo_ref, h_ref):
    y = _dot(att_ref[...], w1_ref[...]) + _dot(ob_ref[...], w2_ref[...])
    xn = x_ref[...] + mod_ref[2:3, :] * y
    xo_ref[...] = xn
    h = _rms(xn) * n2_ref[...] * (1.0 + mod_ref[4:5, :]) + mod_ref[3:4, :]
    h_ref[...] = h.astype(BF)


def _outproj(x, att, ob, mod, mod_row0, lw, tl):
    b, l, d = x.shape
    if mod_row0 is None:
        mod_spec = pl.BlockSpec((None, N_MOD, d), lambda i, j: (i, 0, 0))
    else:
        mod_spec = pl.BlockSpec((None, N_MOD, d), lambda i, j: (mod_row0, 0, 0))

    def tile(w):
        return pl.BlockSpec((None, tl, w), lambda i, j: (i, j, 0))

    def full(a):
        return pl.BlockSpec(a.shape, lambda i, j: (0, 0))

    return pl.pallas_call(
        _outproj_kernel,
        out_shape=[jax.ShapeDtypeStruct((b, l, d), F32), jax.ShapeDtypeStruct((b, l, d), BF)],
        grid=(b, l // tl),
        in_specs=[tile(d), tile(V_WIDTH), tile(POOL_WIDTH), mod_spec, full(lw["n2_g"]), full(lw["w_o1"]),
                  full(lw["w_o2"])],
        out_specs=[tile(d), tile(d)],
        compiler_params=_cparams(("parallel", "parallel")),
        name="outproj",
    )(x, att, ob, mod, lw["n2_g"], lw["w_o1"], lw["w_o2"])


def _gelu_tanh(x):
    return 0.5 * x * (1.0 + jnp.tanh(0.7978845608028654 * (x + 0.044715 * (x * x * x))))


def _sort16_network():
    def merge(lo, hi, r):
        step = r * 2
        if step < hi - lo:
            yield from merge(lo, hi, step)
            yield from merge(lo + r, hi, step)
            yield from [(i, i + r) for i in range(lo + r, hi - r, step)]
        else:
            yield (lo, lo + r)

    def sort(lo, hi):
        if hi - lo >= 1:
            mid = lo + (hi - lo) // 2
            yield from sort(lo, mid)
            yield from sort(mid + 1, hi)
            yield from merge(lo, hi, 1)

    return tuple(sort(0, PEER_TOPK - 1))


def _top16_desc(blocks):
    a = list(blocks)
    for i, j in _sort16_network():
        a[i], a[j] = jnp.maximum(a[i], a[j]), jnp.minimum(a[i], a[j])
    for shift in (4, 2, 1):
        c = [jnp.maximum(a[i], pltpu.roll(a[PEER_TOPK - 1 - i], shift, 0)) for i in range(PEER_TOPK)]
        stride = PEER_TOPK // 2
        while stride >= 1:
            for i in range(PEER_TOPK):
                if i & stride == 0:
                    c[i], c[i + stride] = jnp.maximum(c[i], c[i + stride]), jnp.minimum(c[i], c[i + stride])
            stride //= 2
        a = c
    return a


def _rank_in_top16(s, tops):
    def pick(bits, lo, step):
        if not bits:
            return tops[lo]
        (b, w), rest = bits[0], bits[1:]
        return jnp.where(b, pick(rest, lo + w, step), pick(rest, lo, step))

    bits = []
    rank = jnp.zeros(s.shape, F32)
    for w in (8, 4, 2, 1):
        b = s < pick(bits, w - 1, w)
        bits.append((b, w))
        rank = rank + jnp.where(b, float(w), 0.0)
    return jnp.where(s < tops[PEER_TOPK - 1], float(PEER_TOPK), rank)


def _peer_prologue(hn_ref, wq_ref, sk_ref, c1_ref, e1_ref, i2_ref, s_ref, v_ref, cand_ref):
    t = hn_ref.shape[0]
    hn = hn_ref[...]

    sub8 = lax.broadcasted_iota(jnp.int32, (8, t), 0)
    sub16 = lax.broadcasted_iota(jnp.int32, (PEER_TOPK, t), 0)
    zeros8 = jnp.zeros((8, t), F32)

    def head(h, carry):
        qh = _dot(hn, wq_ref[h]).astype(BF)
        r2 = None
        for p in range(2):
            st = _dot_nt(sk_ref[h, p], qh[:, p * PEER_HALF:(p + 1) * PEER_HALF])
            s_ref[p, :, 0:t] = st
            top = _top16_desc([st[8 * v:8 * v + 8, :] for v in range(PEER_N_KEYS // 8)])
            for k in range(PEER_TOPK):
                v_ref[p, k:k + 1, 0:t] = top[k][0:1, :]
            if p == 1:
                r2 = _rank_in_top16(st, [top[k][0:1, :] for k in range(PEER_TOPK)])
        v1 = v_ref[0, :, 0:t]
        v2 = v_ref[1, :, 0:t]

        groups = [
            v1[0:1, :] + v2,
            jnp.where(sub16 >= 1, v1 + v2[0:1, :], NEG_INF),
            jnp.where(sub8 >= 1, v1[1:2, :] + v2[0:8, :], NEG_INF),
            jnp.where(sub8 >= 2, v1[0:8, :] + v2[1:2, :], NEG_INF),
            jnp.where((sub8 >= 2) & (sub8 <= 4), v1[2:3, :] + v2[0:8, :], NEG_INF),
            jnp.where((sub8 >= 3) & (sub8 <= 4), v1[0:8, :] + v2[2:3, :], NEG_INF),
            jnp.where(sub8 == 3, v1[3:4, :] + v2[0:8, :], NEG_INF),
        ]
        cand_ref[:, 0:t] = jnp.concatenate(groups, axis=0)
        c00 = v1[0:1, :] + v2[0:1, :]

        def below_max(k, m):
            c = cand_ref[:, 0:t]
            return jnp.where(c < m, c, NEG_INF).max(axis=0, keepdims=True)

        tau = lax.fori_loop(1, PEER_TOPK, below_max, c00)
        c = cand_ref[:, 0:t]
        sel = c >= tau
        z = jnp.where(sel, jnp.exp(c - c00), 0.0).sum(axis=0, keepdims=True)
        ind = jnp.where(sel, 1.0, 0.0)

        def row_total(r0, rows, a):
            return jnp.where(sub16 == a, ind[r0:r0 + rows, :].sum(axis=0, keepdims=True), 0.0)

        cnt = (ind[16:32, :] + jnp.concatenate([ind[40:48, :] + ind[56:64, :], zeros8], axis=0)
               + row_total(0, 16, 0) + row_total(32, 8, 1) + row_total(48, 8, 2) + row_total(64, 8, 3))
        s1 = s_ref[0, :, 0:t]
        s2 = s_ref[1, :, 0:t]
        c1 = jnp.zeros((PEER_N_KEYS, t), F32)
        for a in range(PEER_TOPK):
            c1 = jnp.where(s1 == v1[a:a + 1, :], cnt[a:a + 1, :], c1)
        c1_ref[h, :, 0:t] = c1
        e1_ref[h, :, 0:t] = jnp.exp(s1 - v1[0:1, :]) * (1.0 / z)
        i2_ref[h, 0, 0:PEER_N_KEYS, 0:t] = r2.astype(BF)
        i2_ref[h, 1, 0:PEER_N_KEYS, 0:t] = jnp.exp(s2 - v2[0:1, :]).astype(BF)
        return carry

    lax.fori_loop(0, PEER_HEADS, head, 0)


def _peer_dense(at_ref, wt_ref, blk, t, c1_ref, e1_ref, i2_ref):
    i1_0 = pl.multiple_of(blk * PEER_I1_PER_BLK, PEER_I1_PER_BLK)
    zero = jnp.zeros((), BF)
    for lg in range(t // LANES):
        ls = slice(lg * LANES, (lg + 1) * LANES)
        c1_s = [c1_ref[h, pl.ds(i1_0, PEER_I1_PER_BLK), ls].astype(BF) for h in range(PEER_HEADS)]
        e1_s = [e1_ref[h, pl.ds(i1_0, PEER_I1_PER_BLK), ls].astype(BF) for h in range(PEER_HEADS)]
        for j in range(PEER_I1_PER_BLK):
            rs = slice(j * PEER_N_KEYS, (j + 1) * PEER_N_KEYS)
            g = jnp.zeros((PEER_N_KEYS, LANES), BF)
            for h in range(PEER_HEADS):
                c1 = jnp.broadcast_to(c1_s[h][j:j + 1, :], (PEER_N_KEYS, LANES))
                e1 = jnp.broadcast_to(e1_s[h][j:j + 1, :], (PEER_N_KEYS, LANES))
                gate1 = jnp.minimum(jnp.maximum(c1 - i2_ref[h, 0, 0:PEER_N_KEYS, ls], zero), e1)
                g = g + gate1 * i2_ref[h, 1, 0:PEER_N_KEYS, ls]
            wt_ref[rs, ls] = _gelu_tanh(at_ref[rs, ls].astype(BF)) * g


def _peer_kernel(hn_ref, x_ref, mod_ref, fg_ref, wq_ref, sk_ref, u_ref, vt_ref, o_ref,
                 acc_ref, at_ref, wt_ref, c1_ref, e1_ref, i2_ref, s_ref, v_ref, cand_ref, *, final):
    e = pl.program_id(1)
    t = hn_ref.shape[0]

    @pl.when(e == 0)
    def _():
        _peer_prologue(hn_ref, wq_ref, sk_ref, c1_ref, e1_ref, i2_ref, s_ref, v_ref, cand_ref)
        acc_ref[...] = jnp.zeros_like(acc_ref)

    at_ref[:, 0:t] = _dot_nt(u_ref[...], hn_ref[...])
    _peer_dense(at_ref, wt_ref, e, t, c1_ref, e1_ref, i2_ref)
    acc_ref[...] += _dot(vt_ref[...], wt_ref[:, 0:t])

    @pl.when(e == pl.num_programs(1) - 1)
    def _():
        y = x_ref[...] + mod_ref[5:6, :] * acc_ref[...].T
        if final:
            y = _rms(y) * fg_ref[...]
        o_ref[...] = y


def _peer(hn, x_mid, mod, mod_row0, tok_per_row, final_g, final, lw):
    ntok, d = hn.shape
    t = PEER_TOK
    n_blk = PEER_N_EXPERTS // PEER_EBLK
    if mod_row0 is None:
        assert tok_per_row % t == 0
        tiles_per_row = tok_per_row // t
        mod_spec = pl.BlockSpec((None, N_MOD, d), lambda i, e: (i // tiles_per_row, 0, 0))
    else:
        mod_spec = pl.BlockSpec((None, N_MOD, d), lambda i, e: (mod_row0, 0, 0))
    tw = t + LANES
    tab_f = pltpu.VMEM((PEER_HEADS, PEER_N_KEYS, tw), F32)
    tab_i2 = pltpu.VMEM((PEER_HEADS, 2, PEER_N_KEYS + 16, tw), BF)
    return pl.pallas_call(
        functools.partial(_peer_kernel, final=final),
        out_shape=jax.ShapeDtypeStruct((ntok, d), F32),
        grid=(ntok // t, n_blk),
        in_specs=[
            pl.BlockSpec((t, d), lambda i, e: (i, 0)),
            pl.BlockSpec((t, d), lambda i, e: (i, 0)),
            mod_spec,
            pl.BlockSpec((1, d), lambda i, e: (0, 0)),
            pl.BlockSpec(lw["peer_wq"].shape, lambda i, e: (0, 0, 0)),
            pl.BlockSpec(lw["peer_sk"].shape, lambda i, e: (0, 0, 0, 0)),
            pl.BlockSpec((PEER_EBLK, d), lambda i, e: (e, 0)),
            pl.BlockSpec((None, d, PEER_EBLK), lambda i, e: (e, 0, 0)),
        ],
        out_specs=pl.BlockSpec((t, d), lambda i, e: (i, 0)),
        scratch_shapes=[
            pltpu.VMEM((d, t), F32),
            pltpu.VMEM((PEER_EBLK, tw), F32),
            pltpu.VMEM((PEER_EBLK, tw), BF),
            tab_f, tab_f, tab_i2,
            pltpu.VMEM((2, PEER_N_KEYS, tw), F32),
            pltpu.VMEM((2, PEER_TOPK, tw), F32),
            pltpu.VMEM((PEER_CAND_ROWS, tw), F32),
        ],
        compiler_params=_cparams(("parallel", "arbitrary")),
        name="peer",
    )(hn, x_mid, mod, final_g, lw["peer_wq"], lw["peer_sk"], lw["peer_u"], lw["peer_vt"])


def _rope_tables(l):
    rows = l // GRID_W
    row = jnp.repeat(jnp.arange(rows), GRID_W).astype(F32)
    col = jnp.tile(jnp.arange(GRID_W), rows).astype(F32)

    def cs(rot_dim):
        n = rot_dim // 4
        inv = ROPE_BASE ** (-jnp.arange(n, dtype=F32) / n)
        ang = jnp.concatenate([row[:, None] * inv, col[:, None] * inv], axis=-1)
        return jnp.cos(ang), jnp.sin(ang)

    ones = lambda w: jnp.ones((l, w), F32)
    zeros = lambda w: jnp.zeros((l, w), F32)
    def block_tables(c, s):
        pad = ROLL_HALF - c.shape[1]
        return (jnp.concatenate([c, ones(pad), c, ones(pad)], axis=1),
                jnp.concatenate([-s, zeros(pad), s, zeros(pad)], axis=1))

    cos_a, sin_a = block_tables(*cs(HEAD_DIM))
    cos_c, sin_c = block_tables(*cs(MLA_ROPE_DIM))
    return cos_a, sin_a, cos_c, sin_c


def _peer_vt_blocks(v_tab):
    n, d = v_tab.shape
    return v_tab.astype(BF).reshape(n // PEER_EBLK, PEER_EBLK, d).transpose(0, 2, 1)


def _head_block(rot, nope):
    rows = (rot if rot is not None else nope).shape[0]
    n = 0 if rot is None else rot.shape[1] // 2
    m = 0 if nope is None else nope.shape[1]
    low = min(m, ROLL_HALF - n)
    parts = []
    if n:
        parts.append(rot[:, :n])
    parts.append(nope[:, :low] if m else jnp.zeros((rows, 0), F32))
    parts.append(jnp.zeros((rows, ROLL_HALF - n - low), F32))
    if n:
        parts.append(rot[:, n:])
    parts.append(nope[:, low:] if m else jnp.zeros((rows, 0), F32))
    parts.append(jnp.zeros((rows, ROLL_HALF - n - (m - low)), F32))
    blk = jnp.concatenate(parts, axis=1)
    assert blk.shape == (rows, LANES)
    return blk


def _layer_weights(layer, norm1_g, norm2_g, w_in, gqa_qn_g, gqa_kn_g, pool_w, pool_scale, mla_qn_g, mla_kvn_g,
                   mla_w_uq, mla_w_ukv, w_out, peer_wq, peer_subkeys, peer_u, peer_v):
    d = D_MODEL
    w = w_in[layer]
    o = 0
    aq = w[:, o:o + GQA_Q_HEADS * HEAD_DIM]; o += GQA_Q_HEADS * HEAD_DIM
    ak = w[:, o:o + GQA_KV_HEADS * HEAD_DIM]; o += GQA_KV_HEADS * HEAD_DIM
    av = w[:, o:o + GQA_KV_HEADS * HEAD_DIM]; o += GQA_KV_HEADS * HEAD_DIM
    wb = w[:, o:o + POOL_WIDTH]; o += POOL_WIDTH
    wcq = w[:, o:o + MLA_Q_RANK]; o += MLA_Q_RANK
    wckv = w[:, o:o + MLA_KV_RANK]; o += MLA_KV_RANK
    wkr = w[:, o:o + MLA_ROPE_DIM]
    cols = []
    for h in range(GQA_Q_HEADS):
        cols.append(_head_block(aq[:, h * HEAD_DIM:(h + 1) * HEAD_DIM], None))
    for g in range(GQA_KV_HEADS):
        cols.append(_head_block(ak[:, g * HEAD_DIM:(g + 1) * HEAD_DIM], None))
    for h in range(GQA_Q_HEADS):
        g = h // GQA_GROUP
        cols.append(av[:, g * HEAD_DIM:(g + 1) * HEAD_DIM])
    cols += [wb, wcq, wckv]
    kr_blk = _head_block(wkr, jnp.zeros((d, MLA_NOPE_DIM), F32))
    cols.append(kr_blk)
    w_wide = jnp.concatenate(cols, axis=1).astype(BF)
    assert w_wide.shape == (d, IN_WIDE)

    def head_gain(g):
        return _head_block(g[None, :], None)

    qd = MLA_NOPE_DIM + MLA_ROPE_DIM
    uq = mla_w_uq[layer]
    ukv = mla_w_ukv[layer]
    uq_cols, uk_cols, uv_cols = [], [], []
    zero_rot = jnp.zeros((MLA_KV_RANK, MLA_ROPE_DIM), F32)
    for h in range(MLA_HEADS):
        q_h = uq[:, h * qd:(h + 1) * qd]
        uq_cols.append(_head_block(q_h[:, MLA_NOPE_DIM:], q_h[:, :MLA_NOPE_DIM]))
        k0 = h * (MLA_NOPE_DIM + MLA_V_DIM)
        uk_cols.append(_head_block(zero_rot, ukv[:, k0:k0 + MLA_NOPE_DIM]))
        uv_cols += [ukv[:, k0 + MLA_NOPE_DIM:k0 + MLA_NOPE_DIM + MLA_V_DIM]]

    pw = pool_w[layer]
    w_bd = jnp.zeros((POOL_WIDTH, POOL_WIDTH), F32)
    for g in range(POOL_GROUPS):
        s = slice(g * POOL_GROUP_DIM, (g + 1) * POOL_GROUP_DIM)
        w_bd = w_bd.at[s, s].set(pw[g])

    wo = w_out[layer]
    na = GQA_Q_HEADS * HEAD_DIM
    w_o1 = jnp.concatenate([wo[0:na], wo[na + POOL_WIDTH:]], axis=0).astype(BF)
    w_o2 = wo[na:na + POOL_WIDTH].astype(BF)

    return {
        "n1_g": norm1_g[layer][None, :],
        "n2_g": norm2_g[layer][None, :],
        "w_in": w_wide,
        "gqa_qg": head_gain(gqa_qn_g[layer]),
        "gqa_kg": head_gain(gqa_kn_g[layer]),
        "mla_qg": mla_qn_g[layer][None, :],
        "mla_kvg": mla_kvn_g[layer][None, :],
        "w_uq": jnp.concatenate(uq_cols, axis=1).astype(BF),
        "w_uk": jnp.concatenate(uk_cols, axis=1).astype(BF),
        "w_uv": jnp.concatenate(uv_cols, axis=1).astype(BF),
        "pool_w": w_bd.astype(BF),
        "pool_scale": pool_scale[layer][None, :],
        "w_o1": w_o1,
        "w_o2": w_o2,
        "peer_wq": peer_wq[layer].reshape(d, PEER_HEADS, PEER_QUERY_DIM).transpose(1, 0, 2).astype(BF),
        "peer_sk": peer_subkeys[layer].astype(BF),
        "peer_u": peer_u[layer].astype(BF),
        "peer_vt": _peer_vt_blocks(peer_v[layer]),
    }


def _tile(l, pref):
    return pref if l % pref == 0 else l


def kernel(x, c, ctx, c_ctx, ada_w, ada_b, norm1_g, norm2_g, w_in, gqa_qn_g, gqa_kn_g, pool_w, pool_scale,
           mla_qn_g, mla_kvn_g, mla_w_uq, mla_w_ukv, w_out, peer_wq, peer_subkeys, peer_u, peer_v, final_g):
    b, l, d = x.shape
    lc = ctx.shape[1]
    depth = ada_w.shape[0]
    assert d == D_MODEL and l % GRID_W == 0
    assert (b * l) % PEER_TOK == 0 and (b * lc) % PEER_TOK == 0

    n_rows = -(-(b + 1) // 8) * 8
    cc = jnp.concatenate([c, c_ctx[None, :], jnp.zeros((n_rows - b - 1, d), F32)], axis=0)
    mod_all = _adaln(cc, ada_w, ada_b).reshape(depth, n_rows, N_MOD, d)

    rope_tabs = _rope_tables(l)
    tl_l, tl_c = _tile(l, 512), _tile(lc, 512)
    ti_l, ti_c = _tile(l, 256), _tile(lc, 256)
    tq_l, tq_c = _tile(l, 512), _tile(lc, 256)
    fg = final_g[None, :]

    xl, xc = x, ctx
    for layer in range(depth):
        lw = _layer_weights(layer, norm1_g, norm2_g, w_in, gqa_qn_g, gqa_kn_g, pool_w, pool_scale, mla_qn_g,
                            mla_kvn_g, mla_w_uq, mla_w_ukv, w_out, peer_wq, peer_subkeys, peer_u, peer_v)
        mod = mod_all[layer]
        last = layer == depth - 1
        q_l, k_l, v_l, pb_l = _inproj(xl, mod, None, lw, rope_tabs, ti_l)
        q_c, k_c, v_c, pb_c = _inproj(xc, mod, b, lw, None, ti_c)

        att_l = _attn(q_l, [k_c, k_l], [v_c, v_l], tq_l)
        ob_l = _pool(pb_l, lw["pool_w"], lw["pool_scale"])
        xl_mid, hn_l = _outproj(xl, att_l, ob_l, mod, None, lw, tl_l)
        xl = _peer(hn_l.reshape(b * l, d), xl_mid.reshape(b * l, d), mod, None, l, fg, last, lw).reshape(b, l, d)

        if not last:
            att_c = _attn(q_c, [k_c], [v_c], tq_c)
            ob_c = _pool(pb_c, lw["pool_w"], lw["pool_scale"])
            xc_mid, hn_c = _outproj(xc, att_c, ob_c, mod, b, lw, tl_c)
            xc = _peer(hn_c.reshape(b * lc, d), xc_mid.reshape(b * lc, d), mod, b, lc, fg, False,
                       lw).reshape(b, lc, d)
    return xl
```

```python
import functools

import jax
import jax.numpy as jnp
from jax import lax
from jax.experimental import pallas as pl
from jax.experimental.pallas import tpu as pltpu

D_MODEL = 1024
GRID_W = 64
N_MOD = 6
EPS = 1e-6
ROPE_BASE = 10000.0
HEAD_DIM = 64
GQA_Q_HEADS = 6
GQA_KV_HEADS = 2
GQA_GROUP = GQA_Q_HEADS // GQA_KV_HEADS
POOL_GROUPS = 4
POOL_WINDOWS = (2, 4, 8, 16)
POOL_WIDTH = D_MODEL // 4
POOL_GROUP_DIM = POOL_WIDTH // POOL_GROUPS
MLA_HEADS = 6
MLA_NOPE_DIM = 64
MLA_ROPE_DIM = 32
MLA_V_DIM = 64
MLA_Q_RANK = 384
MLA_KV_RANK = 256
PEER_HEADS = 8
PEER_N_KEYS = 128
PEER_N_EXPERTS = PEER_N_KEYS * PEER_N_KEYS
PEER_TOPK = 16
PEER_QUERY_DIM = 256
PEER_HALF = PEER_QUERY_DIM // 2

LANES = 128
N_ATT_HEADS = GQA_Q_HEADS + MLA_HEADS
QK_WIDTH = N_ATT_HEADS * LANES
V_WIDTH = N_ATT_HEADS * HEAD_DIM
POOL_PAD = 16

_C_QA = 0
_C_KA = _C_QA + GQA_Q_HEADS * LANES
_C_VA = _C_KA + GQA_KV_HEADS * LANES
_C_B = _C_VA + GQA_Q_HEADS * HEAD_DIM
_C_CQ = _C_B + POOL_WIDTH
_C_CKV = _C_CQ + MLA_Q_RANK
_C_KR = _C_CKV + MLA_KV_RANK
IN_WIDE = _C_KR + LANES
ROLL_HALF = LANES // 2

PEER_TOK = 512
PEER_EBLK = 2048
PEER_I1_PER_BLK = PEER_EBLK // PEER_N_KEYS
PEER_CAND_ROWS = 2 * PEER_TOPK + 5 * 8
VMEM_LIMIT = 56 * 1024 * 1024
ATTN_VMEM_LIMIT = 60 * 1024 * 1024

BF = jnp.bfloat16
F32 = jnp.float32
LOG2_E = 1.4426950408889634
NEG_INF = float("-inf")


def _cparams(sem, vmem_limit=VMEM_LIMIT):
    return pltpu.CompilerParams(dimension_semantics=sem, vmem_limit_bytes=vmem_limit)


def _dot(a, b):
    return jnp.dot(a, b, preferred_element_type=F32)


def _dot_nt(a, b):
    return lax.dot_general(a, b, (((1,), (1,)), ((), ())), preferred_element_type=F32)


def _split_bf16(a):
    hi = a.astype(BF)
    lo = (a - hi.astype(F32)).astype(BF)
    return hi, lo


def _dot3(a, b):
    ah, al = _split_bf16(a)
    bh, bl = _split_bf16(b)
    return _dot(ah, bh) + _dot(ah, bl) + _dot(al, bh)


def _rms(x):
    return x * lax.rsqrt(jnp.mean(x * x, axis=-1, keepdims=True) + EPS)


def _adaln_kernel(c_ref, w_ref, b_ref, o_ref):
    c = c_ref[...]
    s = c * (1.0 / (1.0 + jnp.exp(-c)))
    o_ref[...] = _dot3(s, w_ref[...]) + b_ref[...]


def _adaln(cc, ada_w, ada_b):
    depth, d, nd = ada_w.shape
    r = cc.shape[0]
    nblk = nd // d
    return pl.pallas_call(
        _adaln_kernel,
        out_shape=jax.ShapeDtypeStruct((depth, r, nd), F32),
        grid=(depth, nblk),
        in_specs=[
            pl.BlockSpec((r, d), lambda l, j: (0, 0)),
            pl.BlockSpec((None, d, d), lambda l, j: (l, 0, j)),
            pl.BlockSpec((None, 1, d), lambda l, j: (l, 0, j)),
        ],
        out_specs=pl.BlockSpec((None, r, d), lambda l, j: (l, 0, j)),
        compiler_params=_cparams(("arbitrary", "arbitrary")),
        name="adaln",
    )(cc, ada_w, ada_b.reshape(depth, 1, nd))


def _rope_block(xb, cb, sb):
    return xb * cb + pltpu.roll(xb, ROLL_HALF, 1) * sb


def _inproj_kernel(*refs, rope):
    if rope:
        (x_ref, mod_ref, n1_ref, w_ref, qg_ref, kg_ref, cqg_ref, ckvg_ref, wuq_ref, wuk_ref, wuv_ref,
         ca_ref, sa_ref, cc_ref, sc_ref, q_ref, k_ref, v_ref, b_ref) = refs
    else:
        (x_ref, mod_ref, n1_ref, w_ref, qg_ref, kg_ref, cqg_ref, ckvg_ref, wuq_ref, wuk_ref, wuv_ref,
         q_ref, k_ref, v_ref, b_ref) = refs
    x = x_ref[...]
    shift = mod_ref[0:1, :]
    scale = mod_ref[1:2, :]
    h = _rms(x) * n1_ref[...] * (1.0 + scale) + shift
    p = _dot(h.astype(BF), w_ref[...])

    sa = HEAD_DIM ** -0.5 * LOG2_E
    sc = (MLA_NOPE_DIM + MLA_ROPE_DIM) ** -0.5 * LOG2_E

    def gqa_head(c0, g_ref):
        blk = p[:, c0:c0 + LANES]
        ms = jnp.sum(blk * blk, axis=-1, keepdims=True) * (1.0 / HEAD_DIM)
        y = blk * lax.rsqrt(ms + EPS) * g_ref[...]
        if rope:
            y = _rope_block(y, ca_ref[...], sa_ref[...])
        return y

    for hh in range(GQA_Q_HEADS):
        q_ref[:, hh * LANES:(hh + 1) * LANES] = (gqa_head(_C_QA + hh * LANES, qg_ref) * sa).astype(BF)
    for g in range(GQA_KV_HEADS):
        kb = gqa_head(_C_KA + g * LANES, kg_ref).astype(BF)
        for hh in range(g * GQA_GROUP, (g + 1) * GQA_GROUP):
            k_ref[:, hh * LANES:(hh + 1) * LANES] = kb
    nva = GQA_Q_HEADS * HEAD_DIM
    v_ref[:, 0:nva] = p[:, _C_VA:_C_VA + nva].astype(BF)
    b_ref[...] = p[:, _C_B:_C_B + POOL_WIDTH]

    cq = _rms(p[:, _C_CQ:_C_CQ + MLA_Q_RANK]) * cqg_ref[...]
    ckv = (_rms(p[:, _C_CKV:_C_CKV + MLA_KV_RANK]) * ckvg_ref[...]).astype(BF)
    qc = _dot(cq.astype(BF), wuq_ref[...])
    kc = _dot(ckv, wuk_ref[...])
    vc = _dot(ckv, wuv_ref[...])
    kr = p[:, _C_KR:_C_KR + LANES]
    if rope:
        kr = _rope_block(kr, cc_ref[...], sc_ref[...])
    base = GQA_Q_HEADS * LANES
    for hh in range(MLA_HEADS):
        qb = qc[:, hh * LANES:(hh + 1) * LANES]
        if rope:
            qb = _rope_block(qb, cc_ref[...], sc_ref[...])
        q_ref[:, base + hh * LANES:base + (hh + 1) * LANES] = (qb * sc).astype(BF)
        k_ref[:, base + hh * LANES:base + (hh + 1) * LANES] = (kc[:, hh * LANES:(hh + 1) * LANES] + kr).astype(BF)
    v_ref[:, nva:V_WIDTH] = vc.astype(BF)


def _inproj(x, mod, mod_row0, lw, rope_tabs, tl):
    b, l, d = x.shape
    rope = rope_tabs is not None
    grid = (b, l // tl)

    def full(a):
        nd = a.ndim
        return pl.BlockSpec(a.shape, lambda i, j: (0,) * nd)

    if mod_row0 is None:
        mod_spec = pl.BlockSpec((None, N_MOD, d), lambda i, j: (i, 0, 0))
    else:
        mod_spec = pl.BlockSpec((None, N_MOD, d), lambda i, j: (mod_row0, 0, 0))
    weights = [lw["n1_g"], lw["w_in"], lw["gqa_qg"], lw["gqa_kg"], lw["mla_qg"], lw["mla_kvg"],
               lw["w_uq"], lw["w_uk"], lw["w_uv"]]
    in_specs = [pl.BlockSpec((None, tl, d), lambda i, j: (i, j, 0)), mod_spec] + [full(a) for a in weights]
    args = [x, mod] + weights
    if rope:
        in_specs += [pl.BlockSpec((tl, LANES), lambda i, j: (j, 0)) for _ in range(4)]
        args += list(rope_tabs)
    def rows(w, dt):
        return jax.ShapeDtypeStruct((b, l, w), dt), pl.BlockSpec((None, tl, w), lambda i, j: (i, j, 0))

    outs = [rows(QK_WIDTH, BF), rows(QK_WIDTH, BF), rows(V_WIDTH, BF), rows(POOL_WIDTH, F32)]
    return pl.pallas_call(
        functools.partial(_inproj_kernel, rope=rope),
        out_shape=[o[0] for o in outs],
        grid=grid,
        in_specs=in_specs,
        out_specs=[o[1] for o in outs],
        compiler_params=_cparams(("parallel", "parallel")),
        name="inproj",
    )(*args)


def _attn_kernel(*refs, n_src):
    q_ref = refs[0]
    k_refs = refs[1:1 + n_src]
    v_refs = refs[1 + n_src:1 + 2 * n_src]
    o_ref = refs[1 + 2 * n_src]
    tq = q_ref.shape[0]
    lane = lax.broadcasted_iota(jnp.int32, (tq, LANES), 1)
    for jb in range(N_ATT_HEADS // 2):
        halves = []
        for n in (2 * jb, 2 * jb + 1):
            q = q_ref[:, n * LANES:(n + 1) * LANES]
            ss = [_dot_nt(q, k_ref[:, n * LANES:(n + 1) * LANES]) for k_ref in k_refs]
            m = ss[0].max(axis=-1, keepdims=True)
            for s in ss[1:]:
                m = jnp.maximum(m, s.max(axis=-1, keepdims=True))
            acc = None
            den = None
            for s, v_ref in zip(ss, v_refs):
                e = jnp.exp2(s - m)
                dsum = e.sum(axis=-1, keepdims=True)
                pv = _dot(e.astype(BF), v_ref[:, jb * LANES:(jb + 1) * LANES])
                acc = pv if acc is None else acc + pv
                den = dsum if den is None else den + dsum
            halves.append(acc * (1.0 / den))
        o = jnp.where(lane < HEAD_DIM, halves[0], halves[1])
        o_ref[:, jb * LANES:(jb + 1) * LANES] = o.astype(BF)


def _attn(q, ks, vs, tq):
    b, l, _ = q.shape
    n_src = len(ks)
    once = pl.Buffered(1)
    in_specs = [pl.BlockSpec((None, tq, QK_WIDTH), lambda i, j: (i, j, 0))]
    in_specs += [pl.BlockSpec((None, k.shape[1], QK_WIDTH), lambda i, j: (i, 0, 0), pipeline_mode=once) for k in ks]
    in_specs += [pl.BlockSpec((None, v.shape[1], V_WIDTH), lambda i, j: (i, 0, 0), pipeline_mode=once) for v in vs]
    return pl.pallas_call(
        functools.partial(_attn_kernel, n_src=n_src),
        out_shape=jax.ShapeDtypeStruct((b, l, V_WIDTH), BF),
        grid=(b, l // tq),
        in_specs=in_specs,
        out_specs=pl.BlockSpec((None, tq, V_WIDTH), lambda i, j: (i, j, 0)),
        compiler_params=_cparams(("parallel", "arbitrary"), ATTN_VMEM_LIMIT),
        name="attn",
    )(q, *ks, *vs)


def _pool_kernel(b_ref, w_ref, s_ref, o_ref, xp_ref):
    l = b_ref.shape[0]
    x = b_ref[...]
    zeros = jnp.zeros((POOL_PAD, POOL_WIDTH), F32)
    xp_ref[0:POOL_PAD, :] = zeros
    xp_ref[POOL_PAD + l:POOL_PAD + l + POOL_PAD, :] = zeros
    xp_ref[POOL_PAD:POOL_PAD + l, :] = x
    t = lax.broadcasted_iota(jnp.int32, (l, LANES), 0)
    lane = lax.broadcasted_iota(jnp.int32, (l, LANES), 1)
    outs = []
    for half in range(POOL_WIDTH // LANES):
        w_small = POOL_WINDOWS[2 * half]
        w_big = POOL_WINDOWS[2 * half + 1]
        lo, hi = half * LANES, (half + 1) * LANES

        def win(j):
            return xp_ref[pl.ds(POOL_PAD + j, l), lo:hi]

        s_small = None
        for j in range(-(w_small // 2), w_small // 2):
            s_small = win(j) if s_small is None else s_small + win(j)
        s_big = s_small
        for j in range(-(w_big // 2), w_big // 2):
            if not (-(w_small // 2) <= j < w_small // 2):
                s_big = s_big + win(j)

        def cnt(w):
            lo_i = jnp.maximum(t - w // 2, 0)
            hi_i = jnp.minimum(t - w // 2 + w, l)
            return (hi_i - lo_i).astype(F32)

        left = lane < POOL_GROUP_DIM
        s = jnp.where(left, s_small, s_big)
        c = jnp.where(left, cnt(w_small), cnt(w_big))
        outs.append(s / c - x[:, lo:hi])
    pooled = jnp.concatenate(outs, axis=1).astype(BF)
    o_ref[...] = (_dot(pooled, w_ref[...]) * s_ref[...]).astype(BF)


def _pool(pb, w_bd, pscale):
    b, l, w = pb.shape
    return pl.pallas_call(
        _pool_kernel,
        out_shape=jax.ShapeDtypeStruct((b, l, w), BF),
        grid=(b,),
        in_specs=[
            pl.BlockSpec((None, l, w), lambda i: (i, 0, 0)),
            pl.BlockSpec((w, w), lambda i: (0, 0)),
            pl.BlockSpec((1, w), lambda i: (0, 0)),
        ],
        out_specs=pl.BlockSpec((None, l, w), lambda i: (i, 0, 0)),
        scratch_shapes=[pltpu.VMEM((l + 2 * POOL_PAD, w), F32)],
        compiler_params=_cparams(("parallel",)),
        name="pool",
    )(pb, w_bd, pscale)


def _outproj_kernel(x_ref, att_ref, ob_ref, mod_ref, n2_ref, w1_ref, w2_ref, xo_ref, h_ref):
    y = _dot(att_ref[...], w1_ref[...]) + _dot(ob_ref[...], w2_ref[...])
    xn = x_ref[...] + mod_ref[2:3, :] * y
    xo_ref[...] = xn
    h = _rms(xn) * n2_ref[...] * (1.0 + mod_ref[4:5, :]) + mod_ref[3:4, :]
    h_ref[...] = h.astype(BF)


def _outproj(x, att, ob, mod, mod_row0, lw, tl):
    b, l, d = x.shape
    if mod_row0 is None:
        mod_spec = pl.BlockSpec((None, N_MOD, d), lambda i, j: (i, 0, 0))
    else:
        mod_spec = pl.BlockSpec((None, N_MOD, d), lambda i, j: (mod_row0, 0, 0))

    def tile(w):
        return pl.BlockSpec((None, tl, w), lambda i, j: (i, j, 0))

    def full(a):
        return pl.BlockSpec(a.shape, lambda i, j: (0, 0))

    return pl.pallas_call(
        _outproj_kernel,
        out_shape=[jax.ShapeDtypeStruct((b, l, d), F32), jax.ShapeDtypeStruct((b, l, d), BF)],
        grid=(b, l // tl),
        in_specs=[tile(d), tile(V_WIDTH), tile(POOL_WIDTH), mod_spec, full(lw["n2_g"]), full(lw["w_o1"]),
                  full(lw["w_o2"])],
        out_specs=[tile(d), tile(d)],
        compiler_params=_cparams(("parallel", "parallel")),
        name="outproj",
    )(x, att, ob, mod, lw["n2_g"], lw["w_o1"], lw["w_o2"])


def _gelu_tanh(x):
    return 0.5 * x * (1.0 + jnp.tanh(0.7978845608028654 * (x + 0.044715 * (x * x * x))))


def _sort16_network():
    def merge(lo, hi, r):
        step = r * 2
        if step < hi - lo:
            yield from merge(lo, hi, step)
            yield from merge(lo + r, hi, step)
            yield from [(i, i + r) for i in range(lo + r, hi - r, step)]
        else:
            yield (lo, lo + r)

    def sort(lo, hi):
        if hi - lo >= 1:
            mid = lo + (hi - lo) // 2
            yield from sort(lo, mid)
            yield from sort(mid + 1, hi)
            yield from merge(lo, hi, 1)

    return tuple(sort(0, PEER_TOPK - 1))


def _top16_desc(blocks):
    a = list(blocks)
    for i, j in _sort16_network():
        a[i], a[j] = jnp.maximum(a[i], a[j]), jnp.minimum(a[i], a[j])
    for shift in (4, 2, 1):
        c = [jnp.maximum(a[i], pltpu.roll(a[PEER_TOPK - 1 - i], shift, 0)) for i in range(PEER_TOPK)]
        stride = PEER_TOPK // 2
        while stride >= 1:
            for i in range(PEER_TOPK):
                if i & stride == 0:
                    c[i], c[i + stride] = jnp.maximum(c[i], c[i + stride]), jnp.minimum(c[i], c[i + stride])
            stride //= 2
        a = c
    return a


def _rank_in_top16(s, tops):
    def pick(bits, lo, step):
        if not bits:
            return tops[lo]
        (b, w), rest = bits[0], bits[1:]
        return jnp.where(b, pick(rest, lo + w, step), pick(rest, lo, step))

    bits = []
    rank = jnp.zeros(s.shape, F32)
    for w in (8, 4, 2, 1):
        b = s < pick(bits, w - 1, w)
        bits.append((b, w))
        rank = rank + jnp.where(b, float(w), 0.0)
    return jnp.where(s < tops[PEER_TOPK - 1], float(PEER_TOPK), rank)


def _peer_prologue(hn_ref, wq_ref, sk_ref, c1_ref, e1_ref, i2_ref, s_ref, v_ref, cand_ref):
    t = hn_ref.shape[0]
    hn = hn_ref[...]

    sub8 = lax.broadcasted_iota(jnp.int32, (8, t), 0)
    sub16 = lax.broadcasted_iota(jnp.int32, (PEER_TOPK, t), 0)
    zeros8 = jnp.zeros((8, t), F32)

    def head(h, carry):
        qh = _dot(hn, wq_ref[h]).astype(BF)
        r2 = None
        for p in range(2):
            st = _dot_nt(sk_ref[h, p], qh[:, p * PEER_HALF:(p + 1) * PEER_HALF])
            s_ref[p, :, 0:t] = st
            top = _top16_desc([st[8 * v:8 * v + 8, :] for v in range(PEER_N_KEYS // 8)])
            for k in range(PEER_TOPK):
                v_ref[p, k:k + 1, 0:t] = top[k][0:1, :]
            if p == 1:
                r2 = _rank_in_top16(st, [top[k][0:1, :] for k in range(PEER_TOPK)])
        v1 = v_ref[0, :, 0:t]
        v2 = v_ref[1, :, 0:t]

        groups = [
            v1[0:1, :] + v2,
            jnp.where(sub16 >= 1, v1 + v2[0:1, :], NEG_INF),
            jnp.where(sub8 >= 1, v1[1:2, :] + v2[0:8, :], NEG_INF),
            jnp.where(sub8 >= 2, v1[0:8, :] + v2[1:2, :], NEG_INF),
            jnp.where((sub8 >= 2) & (sub8 <= 4), v1[2:3, :] + v2[0:8, :], NEG_INF),
            jnp.where((sub8 >= 3) & (sub8 <= 4), v1[0:8, :] + v2[2:3, :], NEG_INF),
            jnp.where(sub8 == 3, v1[3:4, :] + v2[0:8, :], NEG_INF),
        ]
        cand_ref[:, 0:t] = jnp.concatenate(groups, axis=0)
        c00 = v1[0:1, :] + v2[0:1, :]

        def below_max(k, m):
            c = cand_ref[:, 0:t]
            return jnp.where(c < m, c, NEG_INF).max(axis=0, keepdims=True)

        tau = lax.fori_loop(1, PEER_TOPK, below_max, c00)
        c = cand_ref[:, 0:t]
        sel = c >= tau
        z = jnp.where(sel, jnp.exp(c - c00), 0.0).sum(axis=0, keepdims=True)
        ind = jnp.where(sel, 1.0, 0.0)

        def row_total(r0, rows, a):
            return jnp.where(sub16 == a, ind[r0:r0 + rows, :].sum(axis=0, keepdims=True), 0.0)

        cnt = (ind[16:32, :] + jnp.concatenate([ind[40:48, :] + ind[56:64, :], zeros8], axis=0)
               + row_total(0, 16, 0) + row_total(32, 8, 1) + row_total(48, 8, 2) + row_total(64, 8, 3))
        s1 = s_ref[0, :, 0:t]
        s2 = s_ref[1, :, 0:t]
        c1 = jnp.zeros((PEER_N_KEYS, t), F32)
        for a in range(PEER_TOPK):
            c1 = jnp.where(s1 == v1[a:a + 1, :], cnt[a:a + 1, :], c1)
        c1_ref[h, :, 0:t] = c1
        e1_ref[h, :, 0:t] = jnp.exp(s1 - v1[0:1, :]) * (1.0 / z)
        i2_ref[h, 0, 0:PEER_N_KEYS, 0:t] = r2.astype(BF)
        i2_ref[h, 1, 0:PEER_N_KEYS, 0:t] = jnp.exp(s2 - v2[0:1, :]).astype(BF)
        return carry

    lax.fori_loop(0, PEER_HEADS, head, 0)


def _peer_dense(at_ref, wt_ref, blk, t, c1_ref, e1_ref, i2_ref):
    i1_0 = pl.multiple_of(blk * PEER_I1_PER_BLK, PEER_I1_PER_BLK)
    zero = jnp.zeros((), BF)
    for lg in range(t // LANES):
        ls = slice(lg * LANES, (lg + 1) * LANES)
        c1_s = [c1_ref[h, pl.ds(i1_0, PEER_I1_PER_BLK), ls].astype(BF) for h in range(PEER_HEADS)]
        e1_s = [e1_ref[h, pl.ds(i1_0, PEER_I1_PER_BLK), ls].astype(BF) for h in range(PEER_HEADS)]
        for j in range(PEER_I1_PER_BLK):
            rs = slice(j * PEER_N_KEYS, (j + 1) * PEER_N_KEYS)
            g = jnp.zeros((PEER_N_KEYS, LANES), BF)
            for h in range(PEER_HEADS):
                c1 = jnp.broadcast_to(c1_s[h][j:j + 1, :], (PEER_N_KEYS, LANES))
                e1 = jnp.broadcast_to(e1_s[h][j:j + 1, :], (PEER_N_KEYS, LANES))
                gate1 = jnp.minimum(jnp.maximum(c1 - i2_ref[h, 0, 0:PEER_N_KEYS, ls], zero), e1)
                g = g + gate1 * i2_ref[h, 1, 0:PEER_N_KEYS, ls]
            wt_ref[rs, ls] = _gelu_tanh(at_ref[rs, ls].astype(BF)) * g


def _peer_kernel(hn_ref, x_ref, mod_ref, fg_ref, wq_ref, sk_ref, u_ref, vt_ref, o_ref,
                 acc_ref, at_ref, wt_ref, c1_ref, e1_ref, i2_ref, s_ref, v_ref, cand_ref, *, final):
    e = pl.program_id(1)
    t = hn_ref.shape[0]

    @pl.when(e == 0)
    def _():
        _peer_prologue(hn_ref, wq_ref, sk_ref, c1_ref, e1_ref, i2_ref, s_ref, v_ref, cand_ref)
        acc_ref[...] = jnp.zeros_like(acc_ref)

    at_ref[:, 0:t] = _dot_nt(u_ref[...], hn_ref[...])
    _peer_dense(at_ref, wt_ref, e, t, c1_ref, e1_ref, i2_ref)
    acc_ref[...] += _dot(vt_ref[...], wt_ref[:, 0:t])

    @pl.when(e == pl.num_programs(1) - 1)
    def _():
        y = x_ref[...] + mod_ref[5:6, :] * acc_ref[...].T
        if final:
            y = _rms(y) * fg_ref[...]
        o_ref[...] = y


def _peer(hn, x_mid, mod, mod_row0, tok_per_row, final_g, final, lw):
    ntok, d = hn.shape
    t = PEER_TOK
    n_blk = PEER_N_EXPERTS // PEER_EBLK
    if mod_row0 is None:
        assert tok_per_row % t == 0
        tiles_per_row = tok_per_row // t
        mod_spec = pl.BlockSpec((None, N_MOD, d), lambda i, e: (i // tiles_per_row, 0, 0))
    else:
        mod_spec = pl.BlockSpec((None, N_MOD, d), lambda i, e: (mod_row0, 0, 0))
    tw = t + LANES
    tab_f = pltpu.VMEM((PEER_HEADS, PEER_N_KEYS, tw), F32)
    tab_i2 = pltpu.VMEM((PEER_HEADS, 2, PEER_N_KEYS + 16, tw), BF)
    return pl.pallas_call(
        functools.partial(_peer_kernel, final=final),
        out_shape=jax.ShapeDtypeStruct((ntok, d), F32),
        grid=(ntok // t, n_blk),
        in_specs=[
            pl.BlockSpec((t, d), lambda i, e: (i, 0)),
            pl.BlockSpec((t, d), lambda i, e: (i, 0)),
            mod_spec,
            pl.BlockSpec((1, d), lambda i, e: (0, 0)),
            pl.BlockSpec(lw["peer_wq"].shape, lambda i, e: (0, 0, 0)),
            pl.BlockSpec(lw["peer_sk"].shape, lambda i, e: (0, 0, 0, 0)),
            pl.BlockSpec((PEER_EBLK, d), lambda i, e: (e, 0)),
            pl.BlockSpec((None, d, PEER_EBLK), lambda i, e: (e, 0, 0)),
        ],
        out_specs=pl.BlockSpec((t, d), lambda i, e: (i, 0)),
        scratch_shapes=[
            pltpu.VMEM((d, t), F32),
            pltpu.VMEM((PEER_EBLK, tw), F32),
            pltpu.VMEM((PEER_EBLK, tw), BF),
            tab_f, tab_f, tab_i2,
            pltpu.VMEM((2, PEER_N_KEYS, tw), F32),
            pltpu.VMEM((2, PEER_TOPK, tw), F32),
            pltpu.VMEM((PEER_CAND_ROWS, tw), F32),
        ],
        compiler_params=_cparams(("parallel", "arbitrary")),
        name="peer",
    )(hn, x_mid, mod, final_g, lw["peer_wq"], lw["peer_sk"], lw["peer_u"], lw["peer_vt"])


def _rope_tables(l):
    rows = l // GRID_W
    row = jnp.repeat(jnp.arange(rows), GRID_W).astype(F32)
    col = jnp.tile(jnp.arange(GRID_W), rows).astype(F32)

    def cs(rot_dim):
        n = rot_dim // 4
        inv = ROPE_BASE ** (-jnp.arange(n, dtype=F32) / n)
        ang = jnp.concatenate([row[:, None] * inv, col[:, None] * inv], axis=-1)
        return jnp.cos(ang), jnp.sin(ang)

    ones = lambda w: jnp.ones((l, w), F32)
    zeros = lambda w: jnp.zeros((l, w), F32)
    def block_tables(c, s):
        pad = ROLL_HALF - c.shape[1]
        return (jnp.concatenate([c, ones(pad), c, ones(pad)], axis=1),
                jnp.concatenate([-s, zeros(pad), s, zeros(pad)], axis=1))

    cos_a, sin_a = block_tables(*cs(HEAD_DIM))
    cos_c, sin_c = block_tables(*cs(MLA_ROPE_DIM))
    return cos_a, sin_a, cos_c, sin_c


def _peer_vt_blocks(v_tab):
    n, d = v_tab.shape
    return v_tab.astype(BF).reshape(n // PEER_EBLK, PEER_EBLK, d).transpose(0, 2, 1)


def _head_block(rot, nope):
    rows = (rot if rot is not None else nope).shape[0]
    n = 0 if rot is None else rot.shape[1] // 2
    m = 0 if nope is None else nope.shape[1]
    low = min(m, ROLL_HALF - n)
    parts = []
    if n:
        parts.append(rot[:, :n])
    parts.append(nope[:, :low] if m else jnp.zeros((rows, 0), F32))
    parts.append(jnp.zeros((rows, ROLL_HALF - n - low), F32))
    if n:
        parts.append(rot[:, n:])
    parts.append(nope[:, low:] if m else jnp.zeros((rows, 0), F32))
    parts.append(jnp.zeros((rows, ROLL_HALF - n - (m - low)), F32))
    blk = jnp.concatenate(parts, axis=1)
    assert blk.shape == (rows, LANES)
    return blk


def _layer_weights(layer, norm1_g, norm2_g, w_in, gqa_qn_g, gqa_kn_g, pool_w, pool_scale, mla_qn_g, mla_kvn_g,
                   mla_w_uq, mla_w_ukv, w_out, peer_wq, peer_subkeys, peer_u, peer_v):
    d = D_MODEL
    w = w_in[layer]
    o = 0
    aq = w[:, o:o + GQA_Q_HEADS * HEAD_DIM]; o += GQA_Q_HEADS * HEAD_DIM
    ak = w[:, o:o + GQA_KV_HEADS * HEAD_DIM]; o += GQA_KV_HEADS * HEAD_DIM
    av = w[:, o:o + GQA_KV_HEADS * HEAD_DIM]; o += GQA_KV_HEADS * HEAD_DIM
    wb = w[:, o:o + POOL_WIDTH]; o += POOL_WIDTH
    wcq = w[:, o:o + MLA_Q_RANK]; o += MLA_Q_RANK
    wckv = w[:, o:o + MLA_KV_RANK]; o += MLA_KV_RANK
    wkr = w[:, o:o + MLA_ROPE_DIM]
    cols = []
    for h in range(GQA_Q_HEADS):
        cols.append(_head_block(aq[:, h * HEAD_DIM:(h + 1) * HEAD_DIM], None))
    for g in range(GQA_KV_HEADS):
        cols.append(_head_block(ak[:, g * HEAD_DIM:(g + 1) * HEAD_DIM], None))
    for h in range(GQA_Q_HEADS):
        g = h // GQA_GROUP
        cols.append(av[:, g * HEAD_DIM:(g + 1) * HEAD_DIM])
    cols += [wb, wcq, wckv]
    kr_blk = _head_block(wkr, jnp.zeros((d, MLA_NOPE_DIM), F32))
    cols.append(kr_blk)
    w_wide = jnp.concatenate(cols, axis=1).astype(BF)
    assert w_wide.shape == (d, IN_WIDE)

    def head_gain(g):
        return _head_block(g[None, :], None)

    qd = MLA_NOPE_DIM + MLA_ROPE_DIM
    uq = mla_w_uq[layer]
    ukv = mla_w_ukv[layer]
    uq_cols, uk_cols, uv_cols = [], [], []
    zero_rot = jnp.zeros((MLA_KV_RANK, MLA_ROPE_DIM), F32)
    for h in range(MLA_HEADS):
        q_h = uq[:, h * qd:(h + 1) * qd]
        uq_cols.append(_head_block(q_h[:, MLA_NOPE_DIM:], q_h[:, :MLA_NOPE_DIM]))
        k0 = h * (MLA_NOPE_DIM + MLA_V_DIM)
        uk_cols.append(_head_block(zero_rot, ukv[:, k0:k0 + MLA_NOPE_DIM]))
        uv_cols += [ukv[:, k0 + MLA_NOPE_DIM:k0 + MLA_NOPE_DIM + MLA_V_DIM]]

    pw = pool_w[layer]
    w_bd = jnp.zeros((POOL_WIDTH, POOL_WIDTH), F32)
    for g in range(POOL_GROUPS):
        s = slice(g * POOL_GROUP_DIM, (g + 1) * POOL_GROUP_DIM)
        w_bd = w_bd.at[s, s].set(pw[g])

    wo = w_out[layer]
    na = GQA_Q_HEADS * HEAD_DIM
    w_o1 = jnp.concatenate([wo[0:na], wo[na + POOL_WIDTH:]], axis=0).astype(BF)
    w_o2 = wo[na:na + POOL_WIDTH].astype(BF)

    return {
        "n1_g": norm1_g[layer][None, :],
        "n2_g": norm2_g[layer][None, :],
        "w_in": w_wide,
        "gqa_qg": head_gain(gqa_qn_g[layer]),
        "gqa_kg": head_gain(gqa_kn_g[layer]),
        "mla_qg": mla_qn_g[layer][None, :],
        "mla_kvg": mla_kvn_g[layer][None, :],
        "w_uq": jnp.concatenate(uq_cols, axis=1).astype(BF),
        "w_uk": jnp.concatenate(uk_cols, axis=1).astype(BF),
        "w_uv": jnp.concatenate(uv_cols, axis=1).astype(BF),
        "pool_w": w_bd.astype(BF),
        "pool_scale": pool_scale[layer][None, :],
        "w_o1": w_o1,
        "w_o2": w_o2,
        "peer_wq": peer_wq[layer].reshape(d, PEER_HEADS, PEER_QUERY_DIM).transpose(1, 0, 2).astype(BF),
        "peer_sk": peer_subkeys[layer].astype(BF),
        "peer_u": peer_u[layer].astype(BF),
        "peer_vt": _peer_vt_blocks(peer_v[layer]),
    }


def _tile(l, pref):
    return pref if l % pref == 0 else l


def kernel(x, c, ctx, c_ctx, ada_w, ada_b, norm1_g, norm2_g, w_in, gqa_qn_g, gqa_kn_g, pool_w, pool_scale,
           mla_qn_g, mla_kvn_g, mla_w_uq, mla_w_ukv, w_out, peer_wq, peer_subkeys, peer_u, peer_v, final_g):
    b, l, d = x.shape
    lc = ctx.shape[1]
    depth = ada_w.shape[0]
    assert d == D_MODEL and l % GRID_W == 0
    assert (b * l) % PEER_TOK == 0 and (b * lc) % PEER_TOK == 0

    n_rows = -(-(b + 1) // 8) * 8
    cc = jnp.concatenate([c, c_ctx[None, :], jnp.zeros((n_rows - b - 1, d), F32)], axis=0)
    mod_all = _adaln(cc, ada_w, ada_b).reshape(depth, n_rows, N_MOD, d)

    rope_tabs = _rope_tables(l)
    tl_l, tl_c = _tile(l, 1024), _tile(lc, 512)
    ti_l, ti_c = _tile(l, 256), _tile(lc, 256)
    tq_l, tq_c = _tile(l, 512), _tile(lc, 256)
    fg = final_g[None, :]

    xl, xc = x, ctx
    for layer in range(depth):
        lw = _layer_weights(layer, norm1_g, norm2_g, w_in, gqa_qn_g, gqa_kn_g, pool_w, pool_scale, mla_qn_g,
                            mla_kvn_g, mla_w_uq, mla_w_ukv, w_out, peer_wq, peer_subkeys, peer_u, peer_v)
        mod = mod_all[layer]
        last = layer == depth - 1
        q_l, k_l, v_l, pb_l = _inproj(xl, mod, None, lw, rope_tabs, ti_l)
        q_c, k_c, v_c, pb_c = _inproj(xc, mod, b, lw, None, ti_c)

        att_l = _attn(q_l, [k_c, k_l], [v_c, v_l], tq_l)
        ob_l = _pool(pb_l, lw["pool_w"], lw["pool_scale"])
        xl_mid, hn_l = _outproj(xl, att_l, ob_l, mod, None, lw, tl_l)
        xl = _peer(hn_l.reshape(b * l, d), xl_mid.reshape(b * l, d), mod, None, l, fg, last, lw).reshape(b, l, d)

        if not last:
            att_c = _attn(q_c, [k_c], [v_c], tq_c)
            ob_c = _pool(pb_c, lw["pool_w"], lw["pool_scale"])
            xc_mid, hn_c = _outproj(xc, att_c, ob_c, mod, b, lw, tl_c)
            xc = _peer(hn_c.reshape(b * lc, d), xc_mid.reshape(b * lc, d), mod, b, lc, fg, False,
                       lw).reshape(b, lc, d)
    return xl
```

```python
import functools

import jax
import jax.numpy as jnp
from jax import lax
from jax.experimental import pallas as pl
from jax.experimental.pallas import tpu as pltpu

D_MODEL = 1024
GRID_W = 64
N_MOD = 6
EPS = 1e-6
ROPE_BASE = 10000.0
HEAD_DIM = 64
GQA_Q_HEADS = 6
GQA_KV_HEADS = 2
GQA_GROUP = GQA_Q_HEADS // GQA_KV_HEADS
POOL_GROUPS = 4
POOL_WINDOWS = (2, 4, 8, 16)
POOL_WIDTH = D_MODEL // 4
POOL_GROUP_DIM = POOL_WIDTH // POOL_GROUPS
MLA_HEADS = 6
MLA_NOPE_DIM = 64
MLA_ROPE_DIM = 32
MLA_V_DIM = 64
MLA_Q_RANK = 384
MLA_KV_RANK = 256
PEER_HEADS = 8
PEER_N_KEYS = 128
PEER_N_EXPERTS = PEER_N_KEYS * PEER_N_KEYS
PEER_TOPK = 16
PEER_QUERY_DIM = 256
PEER_HALF = PEER_QUERY_DIM // 2

LANES = 128
N_ATT_HEADS = GQA_Q_HEADS + MLA_HEADS
QK_WIDTH = N_ATT_HEADS * LANES
V_WIDTH = N_ATT_HEADS * HEAD_DIM
POOL_PAD = 16

_C_QA = 0
_C_KA = _C_QA + GQA_Q_HEADS * LANES
_C_VA = _C_KA + GQA_KV_HEADS * LANES
_C_B = _C_VA + GQA_Q_HEADS * HEAD_DIM
_C_CQ = _C_B + POOL_WIDTH
_C_CKV = _C_CQ + MLA_Q_RANK
_C_KR = _C_CKV + MLA_KV_RANK
IN_WIDE = _C_KR + LANES
ROLL_HALF = LANES // 2

PEER_TOK = 512
PEER_EBLK = 2048
PEER_I1_PER_BLK = PEER_EBLK // PEER_N_KEYS
PEER_CAND_ROWS = 2 * PEER_TOPK + 5 * 8
VMEM_LIMIT = 56 * 1024 * 1024
LARGE_VMEM_LIMIT = 60 * 1024 * 1024

BF = jnp.bfloat16
F32 = jnp.float32
LOG2_E = 1.4426950408889634
NEG_INF = float("-inf")


def _cparams(sem, vmem_limit=VMEM_LIMIT):
    return pltpu.CompilerParams(dimension_semantics=sem, vmem_limit_bytes=vmem_limit)


def _dot(a, b):
    return jnp.dot(a, b, preferred_element_type=F32)


def _dot_nt(a, b):
    return lax.dot_general(a, b, (((1,), (1,)), ((), ())), preferred_element_type=F32)


def _split_bf16(a):
    hi = a.astype(BF)
    lo = (a - hi.astype(F32)).astype(BF)
    return hi, lo


def _dot3(a, b):
    ah, al = _split_bf16(a)
    bh, bl = _split_bf16(b)
    return _dot(ah, bh) + _dot(ah, bl) + _dot(al, bh)


def _rms(x):
    return x * lax.rsqrt(jnp.mean(x * x, axis=-1, keepdims=True) + EPS)


def _adaln_kernel(c_ref, w_ref, b_ref, o_ref):
    c = c_ref[...]
    s = c * (1.0 / (1.0 + jnp.exp(-c)))
    o_ref[...] = _dot3(s, w_ref[...]) + b_ref[...]


def _adaln(cc, ada_w, ada_b):
    depth, d, nd = ada_w.shape
    r = cc.shape[0]
    nblk = nd // d
    return pl.pallas_call(
        _adaln_kernel,
        out_shape=jax.ShapeDtypeStruct((depth, r, nd), F32),
        grid=(depth, nblk),
        in_specs=[
            pl.BlockSpec((r, d), lambda l, j: (0, 0)),
            pl.BlockSpec((None, d, d), lambda l, j: (l, 0, j)),
            pl.BlockSpec((None, 1, d), lambda l, j: (l, 0, j)),
        ],
        out_specs=pl.BlockSpec((None, r, d), lambda l, j: (l, 0, j)),
        compiler_params=_cparams(("arbitrary", "arbitrary")),
        name="adaln",
    )(cc, ada_w, ada_b.reshape(depth, 1, nd))


def _rope_block(xb, cb, sb):
    return xb * cb + pltpu.roll(xb, ROLL_HALF, 1) * sb


def _inproj_kernel(*refs, rope):
    if rope:
        (x_ref, mod_ref, n1_ref, w_ref, qg_ref, kg_ref, cqg_ref, ckvg_ref, wuq_ref, wuk_ref, wuv_ref,
         ca_ref, sa_ref, cc_ref, sc_ref, q_ref, k_ref, v_ref, b_ref) = refs
    else:
        (x_ref, mod_ref, n1_ref, w_ref, qg_ref, kg_ref, cqg_ref, ckvg_ref, wuq_ref, wuk_ref, wuv_ref,
         q_ref, k_ref, v_ref, b_ref) = refs
    x = x_ref[...]
    shift = mod_ref[0:1, :]
    scale = mod_ref[1:2, :]
    h = _rms(x) * n1_ref[...] * (1.0 + scale) + shift
    p = _dot(h.astype(BF), w_ref[...])

    sa = HEAD_DIM ** -0.5 * LOG2_E
    sc = (MLA_NOPE_DIM + MLA_ROPE_DIM) ** -0.5 * LOG2_E

    def gqa_head(c0, g_ref):
        blk = p[:, c0:c0 + LANES]
        ms = jnp.sum(blk * blk, axis=-1, keepdims=True) * (1.0 / HEAD_DIM)
        y = blk * lax.rsqrt(ms + EPS) * g_ref[...]
        if rope:
            y = _rope_block(y, ca_ref[...], sa_ref[...])
        return y

    for hh in range(GQA_Q_HEADS):
        q_ref[:, hh * LANES:(hh + 1) * LANES] = (gqa_head(_C_QA + hh * LANES, qg_ref) * sa).astype(BF)
    for g in range(GQA_KV_HEADS):
        kb = gqa_head(_C_KA + g * LANES, kg_ref).astype(BF)
        for hh in range(g * GQA_GROUP, (g + 1) * GQA_GROUP):
            k_ref[:, hh * LANES:(hh + 1) * LANES] = kb
    nva = GQA_Q_HEADS * HEAD_DIM
    v_ref[:, 0:nva] = p[:, _C_VA:_C_VA + nva].astype(BF)
    b_ref[...] = p[:, _C_B:_C_B + POOL_WIDTH]

    cq = _rms(p[:, _C_CQ:_C_CQ + MLA_Q_RANK]) * cqg_ref[...]
    ckv = (_rms(p[:, _C_CKV:_C_CKV + MLA_KV_RANK]) * ckvg_ref[...]).astype(BF)
    qc = _dot(cq.astype(BF), wuq_ref[...])
    kc = _dot(ckv, wuk_ref[...])
    vc = _dot(ckv, wuv_ref[...])
    kr = p[:, _C_KR:_C_KR + LANES]
    if rope:
        kr = _rope_block(kr, cc_ref[...], sc_ref[...])
    base = GQA_Q_HEADS * LANES
    for hh in range(MLA_HEADS):
        qb = qc[:, hh * LANES:(hh + 1) * LANES]
        if rope:
            qb = _rope_block(qb, cc_ref[...], sc_ref[...])
        q_ref[:, base + hh * LANES:base + (hh + 1) * LANES] = (qb * sc).astype(BF)
        k_ref[:, base + hh * LANES:base + (hh + 1) * LANES] = (kc[:, hh * LANES:(hh + 1) * LANES] + kr).astype(BF)
    v_ref[:, nva:V_WIDTH] = vc.astype(BF)


def _inproj(x, mod, mod_row0, lw, rope_tabs, tl):
    b, l, d = x.shape
    rope = rope_tabs is not None
    grid = (b, l // tl)

    def full(a):
        nd = a.ndim
        return pl.BlockSpec(a.shape, lambda i, j: (0,) * nd)

    if mod_row0 is None:
        mod_spec = pl.BlockSpec((None, N_MOD, d), lambda i, j: (i, 0, 0))
    else:
        mod_spec = pl.BlockSpec((None, N_MOD, d), lambda i, j: (mod_row0, 0, 0))
    weights = [lw["n1_g"], lw["w_in"], lw["gqa_qg"], lw["gqa_kg"], lw["mla_qg"], lw["mla_kvg"],
               lw["w_uq"], lw["w_uk"], lw["w_uv"]]
    in_specs = [pl.BlockSpec((None, tl, d), lambda i, j: (i, j, 0)), mod_spec] + [full(a) for a in weights]
    args = [x, mod] + weights
    if rope:
        in_specs += [pl.BlockSpec((tl, LANES), lambda i, j: (j, 0)) for _ in range(4)]
        args += list(rope_tabs)
    def rows(w, dt):
        return jax.ShapeDtypeStruct((b, l, w), dt), pl.BlockSpec((None, tl, w), lambda i, j: (i, j, 0))

    outs = [rows(QK_WIDTH, BF), rows(QK_WIDTH, BF), rows(V_WIDTH, BF), rows(POOL_WIDTH, F32)]
    return pl.pallas_call(
        functools.partial(_inproj_kernel, rope=rope),
        out_shape=[o[0] for o in outs],
        grid=grid,
        in_specs=in_specs,
        out_specs=[o[1] for o in outs],
        compiler_params=_cparams(("parallel", "parallel")),
        name="inproj",
    )(*args)


def _attn_kernel(*refs, n_src):
    q_ref = refs[0]
    k_refs = refs[1:1 + n_src]
    v_refs = refs[1 + n_src:1 + 2 * n_src]
    o_ref = refs[1 + 2 * n_src]
    tq = q_ref.shape[0]
    lane = lax.broadcasted_iota(jnp.int32, (tq, LANES), 1)
    for jb in range(N_ATT_HEADS // 2):
        halves = []
        for n in (2 * jb, 2 * jb + 1):
            q = q_ref[:, n * LANES:(n + 1) * LANES]
            ss = [_dot_nt(q, k_ref[:, n * LANES:(n + 1) * LANES]) for k_ref in k_refs]
            m = ss[0].max(axis=-1, keepdims=True)
            for s in ss[1:]:
                m = jnp.maximum(m, s.max(axis=-1, keepdims=True))
            acc = None
            den = None
            for s, v_ref in zip(ss, v_refs):
                e = jnp.exp2(s - m)
                dsum = e.sum(axis=-1, keepdims=True)
                pv = _dot(e.astype(BF), v_ref[:, jb * LANES:(jb + 1) * LANES])
                acc = pv if acc is None else acc + pv
                den = dsum if den is None else den + dsum
            halves.append(acc * (1.0 / den))
        o = jnp.where(lane < HEAD_DIM, halves[0], halves[1])
        o_ref[:, jb * LANES:(jb + 1) * LANES] = o.astype(BF)


def _attn(q, ks, vs, tq):
    b, l, _ = q.shape
    n_src = len(ks)
    once = pl.Buffered(1)
    in_specs = [pl.BlockSpec((None, tq, QK_WIDTH), lambda i, j: (i, j, 0))]
    in_specs += [pl.BlockSpec((None, k.shape[1], QK_WIDTH), lambda i, j: (i, 0, 0), pipeline_mode=once) for k in ks]
    in_specs += [pl.BlockSpec((None, v.shape[1], V_WIDTH), lambda i, j: (i, 0, 0), pipeline_mode=once) for v in vs]
    return pl.pallas_call(
        functools.partial(_attn_kernel, n_src=n_src),
        out_shape=jax.ShapeDtypeStruct((b, l, V_WIDTH), BF),
        grid=(b, l // tq),
        in_specs=in_specs,
        out_specs=pl.BlockSpec((None, tq, V_WIDTH), lambda i, j: (i, j, 0)),
        compiler_params=_cparams(("parallel", "arbitrary"), LARGE_VMEM_LIMIT),
        name="attn",
    )(q, *ks, *vs)


def _pool_kernel(b_ref, w_ref, s_ref, o_ref, xp_ref):
    l = b_ref.shape[0]
    x = b_ref[...]
    zeros = jnp.zeros((POOL_PAD, POOL_WIDTH), F32)
    xp_ref[0:POOL_PAD, :] = zeros
    xp_ref[POOL_PAD + l:POOL_PAD + l + POOL_PAD, :] = zeros
    xp_ref[POOL_PAD:POOL_PAD + l, :] = x
    t = lax.broadcasted_iota(jnp.int32, (l, LANES), 0)
    lane = lax.broadcasted_iota(jnp.int32, (l, LANES), 1)
    outs = []
    for half in range(POOL_WIDTH // LANES):
        w_small = POOL_WINDOWS[2 * half]
        w_big = POOL_WINDOWS[2 * half + 1]
        lo, hi = half * LANES, (half + 1) * LANES

        def win(j):
            return xp_ref[pl.ds(POOL_PAD + j, l), lo:hi]

        s_small = None
        for j in range(-(w_small // 2), w_small // 2):
            s_small = win(j) if s_small is None else s_small + win(j)
        s_big = s_small
        for j in range(-(w_big // 2), w_big // 2):
            if not (-(w_small // 2) <= j < w_small // 2):
                s_big = s_big + win(j)

        def cnt(w):
            lo_i = jnp.maximum(t - w // 2, 0)
            hi_i = jnp.minimum(t - w // 2 + w, l)
            return (hi_i - lo_i).astype(F32)

        left = lane < POOL_GROUP_DIM
        s = jnp.where(left, s_small, s_big)
        c = jnp.where(left, cnt(w_small), cnt(w_big))
        outs.append(s / c - x[:, lo:hi])
    pooled = jnp.concatenate(outs, axis=1).astype(BF)
    o_ref[...] = (_dot(pooled, w_ref[...]) * s_ref[...]).astype(BF)


def _pool(pb, w_bd, pscale):
    b, l, w = pb.shape
    return pl.pallas_call(
        _pool_kernel,
        out_shape=jax.ShapeDtypeStruct((b, l, w), BF),
        grid=(b,),
        in_specs=[
            pl.BlockSpec((None, l, w), lambda i: (i, 0, 0)),
            pl.BlockSpec((w, w), lambda i: (0, 0)),
            pl.BlockSpec((1, w), lambda i: (0, 0)),
        ],
        out_specs=pl.BlockSpec((None, l, w), lambda i: (i, 0, 0)),
        scratch_shapes=[pltpu.VMEM((l + 2 * POOL_PAD, w), F32)],
        compiler_params=_cparams(("parallel",)),
        name="pool",
    )(pb, w_bd, pscale)


def _outproj_kernel(x_ref, att_ref, ob_ref, mod_ref, n2_ref, w1_ref, w2_ref, xo_ref, h_ref):
    y = _dot(att_ref[...], w1_ref[...]) + _dot(ob_ref[...], w2_ref[...])
    xn = x_ref[...] + mod_ref[2:3, :] * y
    xo_ref[...] = xn
    h = _rms(xn) * n2_ref[...] * (1.0 + mod_ref[4:5, :]) + mod_ref[3:4, :]
    h_ref[...] = h.astype(BF)


def _outproj(x, att, ob, mod, mod_row0, lw, tl):
    b, l, d = x.shape
    if mod_row0 is None:
        mod_spec = pl.BlockSpec((None, N_MOD, d), lambda i, j: (i, 0, 0))
    else:
        mod_spec = pl.BlockSpec((None, N_MOD, d), lambda i, j: (mod_row0, 0, 0))

    def tile(w):
        return pl.BlockSpec((None, tl, w), lambda i, j: (i, j, 0))

    def full(a):
        return pl.BlockSpec(a.shape, lambda i, j: (0, 0))

    return pl.pallas_call(
        _outproj_kernel,
        out_shape=[jax.ShapeDtypeStruct((b, l, d), F32), jax.ShapeDtypeStruct((b, l, d), BF)],
        grid=(b, l // tl),
        in_specs=[tile(d), tile(V_WIDTH), tile(POOL_WIDTH), mod_spec, full(lw["n2_g"]), full(lw["w_o1"]),
                  full(lw["w_o2"])],
        out_specs=[tile(d), tile(d)],
        compiler_params=_cparams(("parallel", "parallel")),
        name="outproj",
    )(x, att, ob, mod, lw["n2_g"], lw["w_o1"], lw["w_o2"])


def _gelu_tanh(x):
    return 0.5 * x * (1.0 + jnp.tanh(0.7978845608028654 * (x + 0.044715 * (x * x * x))))


def _sort16_network():
    def merge(lo, hi, r):
        step = r * 2
        if step < hi - lo:
            yield from merge(lo, hi, step)
            yield from merge(lo + r, hi, step)
            yield from [(i, i + r) for i in range(lo + r, hi - r, step)]
        else:
            yield (lo, lo + r)

    def sort(lo, hi):
        if hi - lo >= 1:
            mid = lo + (hi - lo) // 2
            yield from sort(lo, mid)
            yield from sort(mid + 1, hi)
            yield from merge(lo, hi, 1)

    return tuple(sort(0, PEER_TOPK - 1))


def _top16_desc(blocks):
    a = list(blocks)
    for i, j in _sort16_network():
        a[i], a[j] = jnp.maximum(a[i], a[j]), jnp.minimum(a[i], a[j])
    for shift in (4, 2, 1):
        c = [jnp.maximum(a[i], pltpu.roll(a[PEER_TOPK - 1 - i], shift, 0)) for i in range(PEER_TOPK)]
        stride = PEER_TOPK // 2
        while stride >= 1:
            for i in range(PEER_TOPK):
                if i & stride == 0:
                    c[i], c[i + stride] = jnp.maximum(c[i], c[i + stride]), jnp.minimum(c[i], c[i + stride])
            stride //= 2
        a = c
    return a


def _rank_in_top16(s, tops):
    def pick(bits, lo, step):
        if not bits:
            return tops[lo]
        (b, w), rest = bits[0], bits[1:]
        return jnp.where(b, pick(rest, lo + w, step), pick(rest, lo, step))

    bits = []
    rank = jnp.zeros(s.shape, F32)
    for w in (8, 4, 2, 1):
        b = s < pick(bits, w - 1, w)
        bits.append((b, w))
        rank = rank + jnp.where(b, float(w), 0.0)
    return jnp.where(s < tops[PEER_TOPK - 1], float(PEER_TOPK), rank)


def _peer_prologue(hn_ref, wq_ref, sk_ref, c1_ref, e1_ref, i2_ref, s_ref, v_ref, cand_ref, q_ref):
    t = hn_ref.shape[0]
    q_ref[...] = _dot(hn_ref[...], wq_ref[...]).astype(BF)
    for hh in range(PEER_HEADS):
        for p in range(2):
            c0 = hh * PEER_QUERY_DIM + p * PEER_HALF
            s_ref[hh, p, :, 0:t] = _dot_nt(sk_ref[hh, p], q_ref[:, c0:c0 + PEER_HALF])

    sub8 = lax.broadcasted_iota(jnp.int32, (8, t), 0)
    sub16 = lax.broadcasted_iota(jnp.int32, (PEER_TOPK, t), 0)
    zeros8 = jnp.zeros((8, t), F32)

    def head(h, carry):
        r2 = None
        for p in range(2):
            st = s_ref[h, p, :, 0:t]
            top = _top16_desc([st[8 * v:8 * v + 8, :] for v in range(PEER_N_KEYS // 8)])
            for k in range(PEER_TOPK):
                v_ref[p, k:k + 1, 0:t] = top[k][0:1, :]
            if p == 1:
                r2 = _rank_in_top16(st, [top[k][0:1, :] for k in range(PEER_TOPK)])
        v1 = v_ref[0, :, 0:t]
        v2 = v_ref[1, :, 0:t]

        groups = [
            v1[0:1, :] + v2,
            jnp.where(sub16 >= 1, v1 + v2[0:1, :], NEG_INF),
            jnp.where(sub8 >= 1, v1[1:2, :] + v2[0:8, :], NEG_INF),
            jnp.where(sub8 >= 2, v1[0:8, :] + v2[1:2, :], NEG_INF),
            jnp.where((sub8 >= 2) & (sub8 <= 4), v1[2:3, :] + v2[0:8, :], NEG_INF),
            jnp.where((sub8 >= 3) & (sub8 <= 4), v1[0:8, :] + v2[2:3, :], NEG_INF),
            jnp.where(sub8 == 3, v1[3:4, :] + v2[0:8, :], NEG_INF),
        ]
        cand_ref[:, 0:t] = jnp.concatenate(groups, axis=0)
        c00 = v1[0:1, :] + v2[0:1, :]

        def below_max(k, m):
            c = cand_ref[:, 0:t]
            return jnp.where(c < m, c, NEG_INF).max(axis=0, keepdims=True)

        tau = lax.fori_loop(1, PEER_TOPK, below_max, c00)
        c = cand_ref[:, 0:t]
        sel = c >= tau
        z = jnp.where(sel, jnp.exp(c - c00), 0.0).sum(axis=0, keepdims=True)
        ind = jnp.where(sel, 1.0, 0.0)

        def row_total(r0, rows, a):
            return jnp.where(sub16 == a, ind[r0:r0 + rows, :].sum(axis=0, keepdims=True), 0.0)

        cnt = (ind[16:32, :] + jnp.concatenate([ind[40:48, :] + ind[56:64, :], zeros8], axis=0)
               + row_total(0, 16, 0) + row_total(32, 8, 1) + row_total(48, 8, 2) + row_total(64, 8, 3))
        s1 = s_ref[h, 0, :, 0:t]
        s2 = s_ref[h, 1, :, 0:t]
        c1 = jnp.zeros((PEER_N_KEYS, t), F32)
        for a in range(PEER_TOPK):
            c1 = jnp.where(s1 == v1[a:a + 1, :], cnt[a:a + 1, :], c1)
        c1_ref[h, :, 0:t] = c1
        e1_ref[h, :, 0:t] = jnp.exp(s1 - v1[0:1, :]) * (1.0 / z)
        i2_ref[h, 0, 0:PEER_N_KEYS, 0:t] = r2.astype(BF)
        i2_ref[h, 1, 0:PEER_N_KEYS, 0:t] = jnp.exp(s2 - v2[0:1, :]).astype(BF)
        return carry

    lax.fori_loop(0, PEER_HEADS, head, 0)


def _peer_dense(at_ref, wt_ref, blk, t, c1_ref, e1_ref, i2_ref):
    i1_0 = pl.multiple_of(blk * PEER_I1_PER_BLK, PEER_I1_PER_BLK)
    zero = jnp.zeros((), BF)
    for lg in range(t // LANES):
        ls = slice(lg * LANES, (lg + 1) * LANES)
        c1_s = [c1_ref[h, pl.ds(i1_0, PEER_I1_PER_BLK), ls].astype(BF) for h in range(PEER_HEADS)]
        e1_s = [e1_ref[h, pl.ds(i1_0, PEER_I1_PER_BLK), ls].astype(BF) for h in range(PEER_HEADS)]
        for j in range(PEER_I1_PER_BLK):
            rs = slice(j * PEER_N_KEYS, (j + 1) * PEER_N_KEYS)
            g = jnp.zeros((PEER_N_KEYS, LANES), BF)
            for h in range(PEER_HEADS):
                c1 = jnp.broadcast_to(c1_s[h][j:j + 1, :], (PEER_N_KEYS, LANES))
                e1 = jnp.broadcast_to(e1_s[h][j:j + 1, :], (PEER_N_KEYS, LANES))
                gate1 = jnp.minimum(jnp.maximum(c1 - i2_ref[h, 0, 0:PEER_N_KEYS, ls], zero), e1)
                g = g + gate1 * i2_ref[h, 1, 0:PEER_N_KEYS, ls]
            wt_ref[rs, ls] = _gelu_tanh(at_ref[rs, ls].astype(BF)) * g


def _peer_kernel(hn_ref, x_ref, mod_ref, fg_ref, wq_ref, sk_ref, u_ref, vt_ref, o_ref,
                 acc_ref, at_ref, wt_ref, c1_ref, e1_ref, i2_ref, s_ref, v_ref, cand_ref, q_ref, *, final):
    e = pl.program_id(1)
    t = hn_ref.shape[0]

    @pl.when(e == 0)
    def _():
        _peer_prologue(hn_ref, wq_ref, sk_ref, c1_ref, e1_ref, i2_ref, s_ref, v_ref, cand_ref, q_ref)
        acc_ref[...] = jnp.zeros_like(acc_ref)

    at_ref[:, 0:t] = _dot_nt(u_ref[...], hn_ref[...])
    _peer_dense(at_ref, wt_ref, e, t, c1_ref, e1_ref, i2_ref)
    acc_ref[...] += _dot(vt_ref[...], wt_ref[:, 0:t])

    @pl.when(e == pl.num_programs(1) - 1)
    def _():
        y = x_ref[...] + mod_ref[5:6, :] * acc_ref[...].T
        if final:
            y = _rms(y) * fg_ref[...]
        o_ref[...] = y


def _peer(hn, x_mid, mod, mod_row0, tok_per_row, final_g, final, lw):
    ntok, d = hn.shape
    t = PEER_TOK
    n_blk = PEER_N_EXPERTS // PEER_EBLK
    if mod_row0 is None:
        assert tok_per_row % t == 0
        tiles_per_row = tok_per_row // t
        mod_spec = pl.BlockSpec((None, N_MOD, d), lambda i, e: (i // tiles_per_row, 0, 0))
    else:
        mod_spec = pl.BlockSpec((None, N_MOD, d), lambda i, e: (mod_row0, 0, 0))
    tw = t + LANES
    tab_f = pltpu.VMEM((PEER_HEADS, PEER_N_KEYS, tw), F32)
    tab_i2 = pltpu.VMEM((PEER_HEADS, 2, PEER_N_KEYS + 16, tw), BF)
    return pl.pallas_call(
        functools.partial(_peer_kernel, final=final),
        out_shape=jax.ShapeDtypeStruct((ntok, d), F32),
        grid=(ntok // t, n_blk),
        in_specs=[
            pl.BlockSpec((t, d), lambda i, e: (i, 0)),
            pl.BlockSpec((t, d), lambda i, e: (i, 0)),
            mod_spec,
            pl.BlockSpec((1, d), lambda i, e: (0, 0)),
            pl.BlockSpec(lw["peer_wq"].shape, lambda i, e: (0, 0), pipeline_mode=pl.Buffered(1)),
            pl.BlockSpec(lw["peer_sk"].shape, lambda i, e: (0, 0, 0, 0)),
            pl.BlockSpec((PEER_EBLK, d), lambda i, e: (e, 0)),
            pl.BlockSpec((None, d, PEER_EBLK), lambda i, e: (e, 0, 0)),
        ],
        out_specs=pl.BlockSpec((t, d), lambda i, e: (i, 0)),
        scratch_shapes=[
            pltpu.VMEM((d, t), F32),
            pltpu.VMEM((PEER_EBLK, tw), F32),
            pltpu.VMEM((PEER_EBLK, tw), BF),
            tab_f, tab_f, tab_i2,
            pltpu.VMEM((PEER_HEADS, 2, PEER_N_KEYS, tw), F32),
            pltpu.VMEM((2, PEER_TOPK, tw), F32),
            pltpu.VMEM((PEER_CAND_ROWS, tw), F32),
            pltpu.VMEM((t, PEER_HEADS * PEER_QUERY_DIM), BF),
        ],
        compiler_params=_cparams(("parallel", "arbitrary"), LARGE_VMEM_LIMIT),
        name="peer",
    )(hn, x_mid, mod, final_g, lw["peer_wq"], lw["peer_sk"], lw["peer_u"], lw["peer_vt"])


def _rope_tables(l):
    rows = l // GRID_W
    row = jnp.repeat(jnp.arange(rows), GRID_W).astype(F32)
    col = jnp.tile(jnp.arange(GRID_W), rows).astype(F32)

    def cs(rot_dim):
        n = rot_dim // 4
        inv = ROPE_BASE ** (-jnp.arange(n, dtype=F32) / n)
        ang = jnp.concatenate([row[:, None] * inv, col[:, None] * inv], axis=-1)
        return jnp.cos(ang), jnp.sin(ang)

    ones = lambda w: jnp.ones((l, w), F32)
    zeros = lambda w: jnp.zeros((l, w), F32)
    def block_tables(c, s):
        pad = ROLL_HALF - c.shape[1]
        return (jnp.concatenate([c, ones(pad), c, ones(pad)], axis=1),
                jnp.concatenate([-s, zeros(pad), s, zeros(pad)], axis=1))

    cos_a, sin_a = block_tables(*cs(HEAD_DIM))
    cos_c, sin_c = block_tables(*cs(MLA_ROPE_DIM))
    return cos_a, sin_a, cos_c, sin_c


def _peer_vt_blocks(v_tab):
    n, d = v_tab.shape
    return v_tab.astype(BF).reshape(n // PEER_EBLK, PEER_EBLK, d).transpose(0, 2, 1)


def _head_block(rot, nope):
    rows = (rot if rot is not None else nope).shape[0]
    n = 0 if rot is None else rot.shape[1] // 2
    m = 0 if nope is None else nope.shape[1]
    low = min(m, ROLL_HALF - n)
    parts = []
    if n:
        parts.append(rot[:, :n])
    parts.append(nope[:, :low] if m else jnp.zeros((rows, 0), F32))
    parts.append(jnp.zeros((rows, ROLL_HALF - n - low), F32))
    if n:
        parts.append(rot[:, n:])
    parts.append(nope[:, low:] if m else jnp.zeros((rows, 0), F32))
    parts.append(jnp.zeros((rows, ROLL_HALF - n - (m - low)), F32))
    blk = jnp.concatenate(parts, axis=1)
    assert blk.shape == (rows, LANES)
    return blk


def _layer_weights(layer, norm1_g, norm2_g, w_in, gqa_qn_g, gqa_kn_g, pool_w, pool_scale, mla_qn_g, mla_kvn_g,
                   mla_w_uq, mla_w_ukv, w_out, peer_wq, peer_subkeys, peer_u, peer_v):
    d = D_MODEL
    w = w_in[layer]
    o = 0
    aq = w[:, o:o + GQA_Q_HEADS * HEAD_DIM]; o += GQA_Q_HEADS * HEAD_DIM
    ak = w[:, o:o + GQA_KV_HEADS * HEAD_DIM]; o += GQA_KV_HEADS * HEAD_DIM
    av = w[:, o:o + GQA_KV_HEADS * HEAD_DIM]; o += GQA_KV_HEADS * HEAD_DIM
    wb = w[:, o:o + POOL_WIDTH]; o += POOL_WIDTH
    wcq = w[:, o:o + MLA_Q_RANK]; o += MLA_Q_RANK
    wckv = w[:, o:o + MLA_KV_RANK]; o += MLA_KV_RANK
    wkr = w[:, o:o + MLA_ROPE_DIM]
    cols = []
    for h in range(GQA_Q_HEADS):
        cols.append(_head_block(aq[:, h * HEAD_DIM:(h + 1) * HEAD_DIM], None))
    for g in range(GQA_KV_HEADS):
        cols.append(_head_block(ak[:, g * HEAD_DIM:(g + 1) * HEAD_DIM], None))
    for h in range(GQA_Q_HEADS):
        g = h // GQA_GROUP
        cols.append(av[:, g * HEAD_DIM:(g + 1) * HEAD_DIM])
    cols += [wb, wcq, wckv]
    kr_blk = _head_block(wkr, jnp.zeros((d, MLA_NOPE_DIM), F32))
    cols.append(kr_blk)
    w_wide = jnp.concatenate(cols, axis=1).astype(BF)
    assert w_wide.shape == (d, IN_WIDE)

    def head_gain(g):
        return _head_block(g[None, :], None)

    qd = MLA_NOPE_DIM + MLA_ROPE_DIM
    uq = mla_w_uq[layer]
    ukv = mla_w_ukv[layer]
    uq_cols, uk_cols, uv_cols = [], [], []
    zero_rot = jnp.zeros((MLA_KV_RANK, MLA_ROPE_DIM), F32)
    for h in range(MLA_HEADS):
        q_h = uq[:, h * qd:(h + 1) * qd]
        uq_cols.append(_head_block(q_h[:, MLA_NOPE_DIM:], q_h[:, :MLA_NOPE_DIM]))
        k0 = h * (MLA_NOPE_DIM + MLA_V_DIM)
        uk_cols.append(_head_block(zero_rot, ukv[:, k0:k0 + MLA_NOPE_DIM]))
        uv_cols += [ukv[:, k0 + MLA_NOPE_DIM:k0 + MLA_NOPE_DIM + MLA_V_DIM]]

    pw = pool_w[layer]
    w_bd = jnp.zeros((POOL_WIDTH, POOL_WIDTH), F32)
    for g in range(POOL_GROUPS):
        s = slice(g * POOL_GROUP_DIM, (g + 1) * POOL_GROUP_DIM)
        w_bd = w_bd.at[s, s].set(pw[g])

    wo = w_out[layer]
    na = GQA_Q_HEADS * HEAD_DIM
    w_o1 = jnp.concatenate([wo[0:na], wo[na + POOL_WIDTH:]], axis=0).astype(BF)
    w_o2 = wo[na:na + POOL_WIDTH].astype(BF)

    return {
        "n1_g": norm1_g[layer][None, :],
        "n2_g": norm2_g[layer][None, :],
        "w_in": w_wide,
        "gqa_qg": head_gain(gqa_qn_g[layer]),
        "gqa_kg": head_gain(gqa_kn_g[layer]),
        "mla_qg": mla_qn_g[layer][None, :],
        "mla_kvg": mla_kvn_g[layer][None, :],
        "w_uq": jnp.concatenate(uq_cols, axis=1).astype(BF),
        "w_uk": jnp.concatenate(uk_cols, axis=1).astype(BF),
        "w_uv": jnp.concatenate(uv_cols, axis=1).astype(BF),
        "pool_w": w_bd.astype(BF),
        "pool_scale": pool_scale[layer][None, :],
        "w_o1": w_o1,
        "w_o2": w_o2,
        "peer_wq": peer_wq[layer].astype(BF),
        "peer_sk": peer_subkeys[layer].astype(BF),
        "peer_u": peer_u[layer].astype(BF),
        "peer_vt": _peer_vt_blocks(peer_v[layer]),
    }


def _tile(l, pref):
    return pref if l % pref == 0 else l


def kernel(x, c, ctx, c_ctx, ada_w, ada_b, norm1_g, norm2_g, w_in, gqa_qn_g, gqa_kn_g, pool_w, pool_scale,
           mla_qn_g, mla_kvn_g, mla_w_uq, mla_w_ukv, w_out, peer_wq, peer_subkeys, peer_u, peer_v, final_g):
    b, l, d = x.shape
    lc = ctx.shape[1]
    depth = ada_w.shape[0]
    assert d == D_MODEL and l % GRID_W == 0
    assert (b * l) % PEER_TOK == 0 and (b * lc) % PEER_TOK == 0

    n_rows = -(-(b + 1) // 8) * 8
    cc = jnp.concatenate([c, c_ctx[None, :], jnp.zeros((n_rows - b - 1, d), F32)], axis=0)
    mod_all = _adaln(cc, ada_w, ada_b).reshape(depth, n_rows, N_MOD, d)

    rope_tabs = _rope_tables(l)
    tl_l, tl_c = _tile(l, 1024), _tile(lc, 512)
    ti_l, ti_c = _tile(l, 256), _tile(lc, 256)
    tq_l, tq_c = _tile(l, 512), _tile(lc, 256)
    fg = final_g[None, :]

    xl, xc = x, ctx
    for layer in range(depth):
        lw = _layer_weights(layer, norm1_g, norm2_g, w_in, gqa_qn_g, gqa_kn_g, pool_w, pool_scale, mla_qn_g,
                            mla_kvn_g, mla_w_uq, mla_w_ukv, w_out, peer_wq, peer_subkeys, peer_u, peer_v)
        mod = mod_all[layer]
        last = layer == depth - 1
        q_l, k_l, v_l, pb_l = _inproj(xl, mod, None, lw, rope_tabs, ti_l)
        q_c, k_c, v_c, pb_c = _inproj(xc, mod, b, lw, None, ti_c)

        att_l = _attn(q_l, [k_c, k_l], [v_c, v_l], tq_l)
        ob_l = _pool(pb_l, lw["pool_w"], lw["pool_scale"])
        xl_mid, hn_l = _outproj(xl, att_l, ob_l, mod, None, lw, tl_l)
        xl = _peer(hn_l.reshape(b * l, d), xl_mid.reshape(b * l, d), mod, None, l, fg, last, lw).reshape(b, l, d)

        if not last:
            att_c = _attn(q_c, [k_c], [v_c], tq_c)
            ob_c = _pool(pb_c, lw["pool_w"], lw["pool_scale"])
            xc_mid, hn_c = _outproj(xc, att_c, ob_c, mod, b, lw, tl_c)
            xc = _peer(hn_c.reshape(b * lc, d), xc_mid.reshape(b * lc, d), mod, b, lc, fg, False,
                       lw).reshape(b, lc, d)
    return xl
```

```python
import functools

import jax
import jax.numpy as jnp
from jax import lax
from jax.experimental import pallas as pl
from jax.experimental.pallas import tpu as pltpu

D_MODEL = 1024
GRID_W = 64
N_MOD = 6
EPS = 1e-6
ROPE_BASE = 10000.0
HEAD_DIM = 64
GQA_Q_HEADS = 6
GQA_KV_HEADS = 2
GQA_GROUP = GQA_Q_HEADS // GQA_KV_HEADS
POOL_GROUPS = 4
POOL_WINDOWS = (2, 4, 8, 16)
POOL_WIDTH = D_MODEL // 4
POOL_GROUP_DIM = POOL_WIDTH // POOL_GROUPS
MLA_HEADS = 6
MLA_NOPE_DIM = 64
MLA_ROPE_DIM = 32
MLA_V_DIM = 64
MLA_Q_RANK = 384
MLA_KV_RANK = 256
PEER_HEADS = 8
PEER_N_KEYS = 128
PEER_N_EXPERTS = PEER_N_KEYS * PEER_N_KEYS
PEER_TOPK = 16
PEER_QUERY_DIM = 256
PEER_HALF = PEER_QUERY_DIM // 2

LANES = 128
N_ATT_HEADS = GQA_Q_HEADS + MLA_HEADS
QK_WIDTH = N_ATT_HEADS * LANES
V_WIDTH = N_ATT_HEADS * HEAD_DIM
POOL_PAD = 16

_C_QA = 0
_C_KA = _C_QA + GQA_Q_HEADS * LANES
_C_VA = _C_KA + GQA_KV_HEADS * LANES
_C_B = _C_VA + GQA_Q_HEADS * HEAD_DIM
_C_CQ = _C_B + POOL_WIDTH
_C_CKV = _C_CQ + MLA_Q_RANK
_C_KR = _C_CKV + MLA_KV_RANK
IN_WIDE = _C_KR + LANES
ROLL_HALF = LANES // 2

PEER_TOK = 512
PEER_EBLK = 2048
PEER_I1_PER_BLK = PEER_EBLK // PEER_N_KEYS
PEER_CAND_ROWS = 2 * PEER_TOPK + 5 * 8
VMEM_LIMIT = 56 * 1024 * 1024
LARGE_VMEM_LIMIT = 60 * 1024 * 1024

BF = jnp.bfloat16
F32 = jnp.float32
LOG2_E = 1.4426950408889634
NEG_INF = float("-inf")


def _cparams(sem, vmem_limit=VMEM_LIMIT):
    return pltpu.CompilerParams(dimension_semantics=sem, vmem_limit_bytes=vmem_limit)


def _dot(a, b):
    return jnp.dot(a, b, preferred_element_type=F32)


def _dot_nt(a, b):
    return lax.dot_general(a, b, (((1,), (1,)), ((), ())), preferred_element_type=F32)


def _split_bf16(a):
    hi = a.astype(BF)
    lo = (a - hi.astype(F32)).astype(BF)
    return hi, lo


def _dot3(a, b):
    ah, al = _split_bf16(a)
    bh, bl = _split_bf16(b)
    return _dot(ah, bh) + _dot(ah, bl) + _dot(al, bh)


def _rms(x):
    return x * lax.rsqrt(jnp.mean(x * x, axis=-1, keepdims=True) + EPS)


def _adaln_kernel(c_ref, w_ref, b_ref, o_ref):
    c = c_ref[...]
    s = c * (1.0 / (1.0 + jnp.exp(-c)))
    o_ref[...] = _dot3(s, w_ref[...]) + b_ref[...]


def _adaln(cc, ada_w, ada_b):
    depth, d, nd = ada_w.shape
    r = cc.shape[0]
    nblk = nd // d
    return pl.pallas_call(
        _adaln_kernel,
        out_shape=jax.ShapeDtypeStruct((depth, r, nd), F32),
        grid=(depth, nblk),
        in_specs=[
            pl.BlockSpec((r, d), lambda l, j: (0, 0)),
            pl.BlockSpec((None, d, d), lambda l, j: (l, 0, j)),
            pl.BlockSpec((None, 1, d), lambda l, j: (l, 0, j)),
        ],
        out_specs=pl.BlockSpec((None, r, d), lambda l, j: (l, 0, j)),
        compiler_params=_cparams(("arbitrary", "arbitrary")),
        name="adaln",
    )(cc, ada_w, ada_b.reshape(depth, 1, nd))


def _rope_block(xb, cb, sb):
    return xb * cb + pltpu.roll(xb, ROLL_HALF, 1) * sb


def _inproj_kernel(*refs, rope):
    if rope:
        (x_ref, mod_ref, n1_ref, w_ref, qg_ref, kg_ref, cqg_ref, ckvg_ref, wuq_ref, wuk_ref, wuv_ref,
         ca_ref, sa_ref, cc_ref, sc_ref, q_ref, k_ref, v_ref, b_ref) = refs
    else:
        (x_ref, mod_ref, n1_ref, w_ref, qg_ref, kg_ref, cqg_ref, ckvg_ref, wuq_ref, wuk_ref, wuv_ref,
         q_ref, k_ref, v_ref, b_ref) = refs
    x = x_ref[...]
    shift = mod_ref[0:1, :]
    scale = mod_ref[1:2, :]
    h = _rms(x) * n1_ref[...] * (1.0 + scale) + shift
    p = _dot(h.astype(BF), w_ref[...])

    sa = HEAD_DIM ** -0.5 * LOG2_E
    sc = (MLA_NOPE_DIM + MLA_ROPE_DIM) ** -0.5 * LOG2_E

    def gqa_head(c0, g_ref):
        blk = p[:, c0:c0 + LANES]
        ms = jnp.sum(blk * blk, axis=-1, keepdims=True) * (1.0 / HEAD_DIM)
        y = blk * lax.rsqrt(ms + EPS) * g_ref[...]
        if rope:
            y = _rope_block(y, ca_ref[...], sa_ref[...])
        return y

    for hh in range(GQA_Q_HEADS):
        q_ref[:, hh * LANES:(hh + 1) * LANES] = (gqa_head(_C_QA + hh * LANES, qg_ref) * sa).astype(BF)
    for g in range(GQA_KV_HEADS):
        kb = gqa_head(_C_KA + g * LANES, kg_ref).astype(BF)
        for hh in range(g * GQA_GROUP, (g + 1) * GQA_GROUP):
            k_ref[:, hh * LANES:(hh + 1) * LANES] = kb
    nva = GQA_Q_HEADS * HEAD_DIM
    v_ref[:, 0:nva] = p[:, _C_VA:_C_VA + nva].astype(BF)
    b_ref[...] = p[:, _C_B:_C_B + POOL_WIDTH]

    cq = _rms(p[:, _C_CQ:_C_CQ + MLA_Q_RANK]) * cqg_ref[...]
    ckv = (_rms(p[:, _C_CKV:_C_CKV + MLA_KV_RANK]) * ckvg_ref[...]).astype(BF)
    qc = _dot(cq.astype(BF), wuq_ref[...])
    kc = _dot(ckv, wuk_ref[...])
    vc = _dot(ckv, wuv_ref[...])
    kr = p[:, _C_KR:_C_KR + LANES]
    if rope:
        kr = _rope_block(kr, cc_ref[...], sc_ref[...])
    base = GQA_Q_HEADS * LANES
    for hh in range(MLA_HEADS):
        qb = qc[:, hh * LANES:(hh + 1) * LANES]
        if rope:
            qb = _rope_block(qb, cc_ref[...], sc_ref[...])
        q_ref[:, base + hh * LANES:base + (hh + 1) * LANES] = (qb * sc).astype(BF)
        k_ref[:, base + hh * LANES:base + (hh + 1) * LANES] = (kc[:, hh * LANES:(hh + 1) * LANES] + kr).astype(BF)
    v_ref[:, nva:V_WIDTH] = vc.astype(BF)


def _inproj(x, mod, mod_row0, lw, rope_tabs, tl):
    b, l, d = x.shape
    rope = rope_tabs is not None
    grid = (b, l // tl)

    def full(a):
        nd = a.ndim
        return pl.BlockSpec(a.shape, lambda i, j: (0,) * nd)

    if mod_row0 is None:
        mod_spec = pl.BlockSpec((None, N_MOD, d), lambda i, j: (i, 0, 0))
    else:
        mod_spec = pl.BlockSpec((None, N_MOD, d), lambda i, j: (mod_row0, 0, 0))
    weights = [lw["n1_g"], lw["w_in"], lw["gqa_qg"], lw["gqa_kg"], lw["mla_qg"], lw["mla_kvg"],
               lw["w_uq"], lw["w_uk"], lw["w_uv"]]
    in_specs = [pl.BlockSpec((None, tl, d), lambda i, j: (i, j, 0)), mod_spec] + [full(a) for a in weights]
    args = [x, mod] + weights
    if rope:
        in_specs += [pl.BlockSpec((tl, LANES), lambda i, j: (j, 0)) for _ in range(4)]
        args += list(rope_tabs)
    def rows(w, dt):
        return jax.ShapeDtypeStruct((b, l, w), dt), pl.BlockSpec((None, tl, w), lambda i, j: (i, j, 0))

    outs = [rows(QK_WIDTH, BF), rows(QK_WIDTH, BF), rows(V_WIDTH, BF), rows(POOL_WIDTH, F32)]
    return pl.pallas_call(
        functools.partial(_inproj_kernel, rope=rope),
        out_shape=[o[0] for o in outs],
        grid=grid,
        in_specs=in_specs,
        out_specs=[o[1] for o in outs],
        compiler_params=_cparams(("parallel", "parallel")),
        name="inproj",
    )(*args)


def _attn_kernel(*refs, n_src):
    q_ref = refs[0]
    k_refs = refs[1:1 + n_src]
    v_refs = refs[1 + n_src:1 + 2 * n_src]
    o_ref = refs[1 + 2 * n_src]
    tq = q_ref.shape[0]
    lane = lax.broadcasted_iota(jnp.int32, (tq, LANES), 1)
    for jb in range(N_ATT_HEADS // 2):
        halves = []
        for n in (2 * jb, 2 * jb + 1):
            q = q_ref[:, n * LANES:(n + 1) * LANES]
            ss = [_dot_nt(q, k_ref[:, n * LANES:(n + 1) * LANES]) for k_ref in k_refs]
            m = ss[0].max(axis=-1, keepdims=True)
            for s in ss[1:]:
                m = jnp.maximum(m, s.max(axis=-1, keepdims=True))
            acc = None
            den = None
            for s, v_ref in zip(ss, v_refs):
                e = jnp.exp2(s - m)
                dsum = e.sum(axis=-1, keepdims=True)
                pv = _dot(e.astype(BF), v_ref[:, jb * LANES:(jb + 1) * LANES])
                acc = pv if acc is None else acc + pv
                den = dsum if den is None else den + dsum
            halves.append(acc * (1.0 / den))
        o = jnp.where(lane < HEAD_DIM, halves[0], halves[1])
        o_ref[:, jb * LANES:(jb + 1) * LANES] = o.astype(BF)


def _attn(q, ks, vs, tq):
    b, l, _ = q.shape
    n_src = len(ks)
    once = pl.Buffered(1)
    in_specs = [pl.BlockSpec((None, tq, QK_WIDTH), lambda i, j: (i, j, 0))]
    in_specs += [pl.BlockSpec((None, k.shape[1], QK_WIDTH), lambda i, j: (i, 0, 0), pipeline_mode=once) for k in ks]
    in_specs += [pl.BlockSpec((None, v.shape[1], V_WIDTH), lambda i, j: (i, 0, 0), pipeline_mode=once) for v in vs]
    return pl.pallas_call(
        functools.partial(_attn_kernel, n_src=n_src),
        out_shape=jax.ShapeDtypeStruct((b, l, V_WIDTH), BF),
        grid=(b, l // tq),
        in_specs=in_specs,
        out_specs=pl.BlockSpec((None, tq, V_WIDTH), lambda i, j: (i, j, 0)),
        compiler_params=_cparams(("parallel", "arbitrary"), LARGE_VMEM_LIMIT),
        name="attn",
    )(q, *ks, *vs)


def _pool_kernel(b_ref, w_ref, s_ref, o_ref, xp_ref):
    l = b_ref.shape[0]
    x = b_ref[...]
    zeros = jnp.zeros((POOL_PAD, POOL_WIDTH), F32)
    xp_ref[0:POOL_PAD, :] = zeros
    xp_ref[POOL_PAD + l:POOL_PAD + l + POOL_PAD, :] = zeros
    xp_ref[POOL_PAD:POOL_PAD + l, :] = x
    t = lax.broadcasted_iota(jnp.int32, (l, LANES), 0)
    lane = lax.broadcasted_iota(jnp.int32, (l, LANES), 1)
    outs = []
    for half in range(POOL_WIDTH // LANES):
        w_small = POOL_WINDOWS[2 * half]
        w_big = POOL_WINDOWS[2 * half + 1]
        lo, hi = half * LANES, (half + 1) * LANES

        def win(j):
            return xp_ref[pl.ds(POOL_PAD + j, l), lo:hi]

        s_small = None
        for j in range(-(w_small // 2), w_small // 2):
            s_small = win(j) if s_small is None else s_small + win(j)
        s_big = s_small
        for j in range(-(w_big // 2), w_big // 2):
            if not (-(w_small // 2) <= j < w_small // 2):
                s_big = s_big + win(j)

        def cnt(w):
            lo_i = jnp.maximum(t - w // 2, 0)
            hi_i = jnp.minimum(t - w // 2 + w, l)
            return (hi_i - lo_i).astype(F32)

        left = lane < POOL_GROUP_DIM
        s = jnp.where(left, s_small, s_big)
        c = jnp.where(left, cnt(w_small), cnt(w_big))
        outs.append(s / c - x[:, lo:hi])
    pooled = jnp.concatenate(outs, axis=1).astype(BF)
    o_ref[...] = (_dot(pooled, w_ref[...]) * s_ref[...]).astype(BF)


def _pool(pb, w_bd, pscale):
    b, l, w = pb.shape
    return pl.pallas_call(
        _pool_kernel,
        out_shape=jax.ShapeDtypeStruct((b, l, w), BF),
        grid=(b,),
        in_specs=[
            pl.BlockSpec((None, l, w), lambda i: (i, 0, 0)),
            pl.BlockSpec((w, w), lambda i: (0, 0)),
            pl.BlockSpec((1, w), lambda i: (0, 0)),
        ],
        out_specs=pl.BlockSpec((None, l, w), lambda i: (i, 0, 0)),
        scratch_shapes=[pltpu.VMEM((l + 2 * POOL_PAD, w), F32)],
        compiler_params=_cparams(("parallel",)),
        name="pool",
    )(pb, w_bd, pscale)


def _outproj_kernel(x_ref, att_ref, ob_ref, mod_ref, n2_ref, w1_ref, w2_ref, xo_ref, h_ref):
    y = _dot(att_ref[...], w1_ref[...]) + _dot(ob_ref[...], w2_ref[...])
    xn = x_ref[...] + mod_ref[2:3, :] * y
    xo_ref[...] = xn
    h = _rms(xn) * n2_ref[...] * (1.0 + mod_ref[4:5, :]) + mod_ref[3:4, :]
    h_ref[...] = h.astype(BF)


def _outproj(x, att, ob, mod, mod_row0, lw, tl):
    b, l, d = x.shape
    if mod_row0 is None:
        mod_spec = pl.BlockSpec((None, N_MOD, d), lambda i, j: (i, 0, 0))
    else:
        mod_spec = pl.BlockSpec((None, N_MOD, d), lambda i, j: (mod_row0, 0, 0))

    def tile(w):
        return pl.BlockSpec((None, tl, w), lambda i, j: (i, j, 0))

    def full(a):
        return pl.BlockSpec(a.shape, lambda i, j: (0, 0))

    return pl.pallas_call(
        _outproj_kernel,
        out_shape=[jax.ShapeDtypeStruct((b, l, d), F32), jax.ShapeDtypeStruct((b, l, d), BF)],
        grid=(b, l // tl),
        in_specs=[tile(d), tile(V_WIDTH), tile(POOL_WIDTH), mod_spec, full(lw["n2_g"]), full(lw["w_o1"]),
                  full(lw["w_o2"])],
        out_specs=[tile(d), tile(d)],
        compiler_params=_cparams(("parallel", "parallel")),
        name="outproj",
    )(x, att, ob, mod, lw["n2_g"], lw["w_o1"], lw["w_o2"])


def _gelu_tanh(x):
    return 0.5 * x * (1.0 + jnp.tanh(0.7978845608028654 * (x + 0.044715 * (x * x * x))))


def _sort16_network():
    def merge(lo, hi, r):
        step = r * 2
        if step < hi - lo:
            yield from merge(lo, hi, step)
            yield from merge(lo + r, hi, step)
            yield from [(i, i + r) for i in range(lo + r, hi - r, step)]
        else:
            yield (lo, lo + r)

    def sort(lo, hi):
        if hi - lo >= 1:
            mid = lo + (hi - lo) // 2
            yield from sort(lo, mid)
            yield from sort(mid + 1, hi)
            yield from merge(lo, hi, 1)

    return tuple(sort(0, PEER_TOPK - 1))


def _top16_desc(blocks):
    a = list(blocks)
    for i, j in _sort16_network():
        a[i], a[j] = jnp.maximum(a[i], a[j]), jnp.minimum(a[i], a[j])
    for shift in (4, 2, 1):
        c = [jnp.maximum(a[i], pltpu.roll(a[PEER_TOPK - 1 - i], shift, 0)) for i in range(PEER_TOPK)]
        stride = PEER_TOPK // 2
        while stride >= 1:
            for i in range(PEER_TOPK):
                if i & stride == 0:
                    c[i], c[i + stride] = jnp.maximum(c[i], c[i + stride]), jnp.minimum(c[i], c[i + stride])
            stride //= 2
        a = c
    return a


def _rank_in_top16(s, tops):
    def pick(bits, lo, step):
        if not bits:
            return tops[lo]
        (b, w), rest = bits[0], bits[1:]
        return jnp.where(b, pick(rest, lo + w, step), pick(rest, lo, step))

    bits = []
    rank = jnp.zeros(s.shape, F32)
    for w in (8, 4, 2, 1):
        b = s < pick(bits, w - 1, w)
        bits.append((b, w))
        rank = rank + jnp.where(b, float(w), 0.0)
    return jnp.where(s < tops[PEER_TOPK - 1], float(PEER_TOPK), rank)


def _peer_prologue(hn_ref, wq_ref, sk_ref, c1_ref, e1_ref, i2_ref, s_ref, v_ref, cand_ref, q_ref):
    t = hn_ref.shape[0]
    q_ref[...] = _dot(hn_ref[...], wq_ref[...]).astype(BF)
    for hh in range(PEER_HEADS):
        for p in range(2):
            c0 = hh * PEER_QUERY_DIM + p * PEER_HALF
            s_ref[hh, p, :, 0:t] = _dot_nt(sk_ref[hh, p], q_ref[:, c0:c0 + PEER_HALF])

    sub8 = lax.broadcasted_iota(jnp.int32, (8, t), 0)
    sub16 = lax.broadcasted_iota(jnp.int32, (PEER_TOPK, t), 0)
    zeros8 = jnp.zeros((8, t), F32)

    def head(h, carry):
        r2 = None
        for p in range(2):
            st = s_ref[h, p, :, 0:t]
            top = _top16_desc([st[8 * v:8 * v + 8, :] for v in range(PEER_N_KEYS // 8)])
            for k in range(PEER_TOPK):
                v_ref[p, k:k + 1, 0:t] = top[k][0:1, :]
            if p == 1:
                r2 = _rank_in_top16(st, [top[k][0:1, :] for k in range(PEER_TOPK)])
        v1 = v_ref[0, :, 0:t]
        v2 = v_ref[1, :, 0:t]

        groups = [
            v1[0:1, :] + v2,
            jnp.where(sub16 >= 1, v1 + v2[0:1, :], NEG_INF),
            jnp.where(sub8 >= 1, v1[1:2, :] + v2[0:8, :], NEG_INF),
            jnp.where(sub8 >= 2, v1[0:8, :] + v2[1:2, :], NEG_INF),
            jnp.where((sub8 >= 2) & (sub8 <= 4), v1[2:3, :] + v2[0:8, :], NEG_INF),
            jnp.where((sub8 >= 3) & (sub8 <= 4), v1[0:8, :] + v2[2:3, :], NEG_INF),
            jnp.where(sub8 == 3, v1[3:4, :] + v2[0:8, :], NEG_INF),
        ]
        cand_ref[:, 0:t] = jnp.concatenate(groups, axis=0)
        c00 = v1[0:1, :] + v2[0:1, :]

        c = cand_ref[:, 0:t]
        neg = jnp.full((8, t), NEG_INF, F32)
        cblocks = [c[8 * v:8 * v + 8, :] for v in range(PEER_CAND_ROWS // 8)]
        tau = _top16_desc(cblocks + [neg] * (PEER_TOPK - len(cblocks)))[PEER_TOPK - 1][0:1, :]
        sel = c >= tau
        z = jnp.where(sel, jnp.exp(c - c00), 0.0).sum(axis=0, keepdims=True)
        ind = jnp.where(sel, 1.0, 0.0)

        def row_total(r0, rows, a):
            return jnp.where(sub16 == a, ind[r0:r0 + rows, :].sum(axis=0, keepdims=True), 0.0)

        cnt = (ind[16:32, :] + jnp.concatenate([ind[40:48, :] + ind[56:64, :], zeros8], axis=0)
               + row_total(0, 16, 0) + row_total(32, 8, 1) + row_total(48, 8, 2) + row_total(64, 8, 3))
        s1 = s_ref[h, 0, :, 0:t]
        s2 = s_ref[h, 1, :, 0:t]
        c1 = jnp.zeros((PEER_N_KEYS, t), F32)
        for a in range(PEER_TOPK):
            c1 = jnp.where(s1 == v1[a:a + 1, :], cnt[a:a + 1, :], c1)
        c1_ref[h, :, 0:t] = c1
        e1_ref[h, :, 0:t] = jnp.exp(s1 - v1[0:1, :]) * (1.0 / z)
        i2_ref[h, 0, 0:PEER_N_KEYS, 0:t] = r2.astype(BF)
        i2_ref[h, 1, 0:PEER_N_KEYS, 0:t] = jnp.exp(s2 - v2[0:1, :]).astype(BF)
        return carry

    lax.fori_loop(0, PEER_HEADS, head, 0)


def _peer_dense(at_ref, wt_ref, blk, t, c1_ref, e1_ref, i2_ref):
    i1_0 = pl.multiple_of(blk * PEER_I1_PER_BLK, PEER_I1_PER_BLK)
    zero = jnp.zeros((), BF)
    for lg in range(t // LANES):
        ls = slice(lg * LANES, (lg + 1) * LANES)
        c1_s = [c1_ref[h, pl.ds(i1_0, PEER_I1_PER_BLK), ls].astype(BF) for h in range(PEER_HEADS)]
        e1_s = [e1_ref[h, pl.ds(i1_0, PEER_I1_PER_BLK), ls].astype(BF) for h in range(PEER_HEADS)]
        for j in range(PEER_I1_PER_BLK):
            rs = slice(j * PEER_N_KEYS, (j + 1) * PEER_N_KEYS)
            g = jnp.zeros((PEER_N_KEYS, LANES), BF)
            for h in range(PEER_HEADS):
                c1 = jnp.broadcast_to(c1_s[h][j:j + 1, :], (PEER_N_KEYS, LANES))
                e1 = jnp.broadcast_to(e1_s[h][j:j + 1, :], (PEER_N_KEYS, LANES))
                gate1 = jnp.minimum(jnp.maximum(c1 - i2_ref[h, 0, 0:PEER_N_KEYS, ls], zero), e1)
                g = g + gate1 * i2_ref[h, 1, 0:PEER_N_KEYS, ls]
            wt_ref[rs, ls] = _gelu_tanh(at_ref[rs, ls].astype(BF)) * g


def _peer_kernel(hn_ref, x_ref, mod_ref, fg_ref, wq_ref, sk_ref, u_ref, vt_ref, o_ref,
                 acc_ref, at_ref, wt_ref, c1_ref, e1_ref, i2_ref, s_ref, v_ref, cand_ref, q_ref, *, final):
    e = pl.program_id(1)
    t = hn_ref.shape[0]

    @pl.when(e == 0)
    def _():
        _peer_prologue(hn_ref, wq_ref, sk_ref, c1_ref, e1_ref, i2_ref, s_ref, v_ref, cand_ref, q_ref)
        acc_ref[...] = jnp.zeros_like(acc_ref)

    at_ref[:, 0:t] = _dot_nt(u_ref[...], hn_ref[...])
    _peer_dense(at_ref, wt_ref, e, t, c1_ref, e1_ref, i2_ref)
    acc_ref[...] += _dot(vt_ref[...], wt_ref[:, 0:t])

    @pl.when(e == pl.num_programs(1) - 1)
    def _():
        y = x_ref[...] + mod_ref[5:6, :] * acc_ref[...].T
        if final:
            y = _rms(y) * fg_ref[...]
        o_ref[...] = y


def _peer(hn, x_mid, mod, mod_row0, tok_per_row, final_g, final, lw):
    ntok, d = hn.shape
    t = PEER_TOK
    n_blk = PEER_N_EXPERTS // PEER_EBLK
    if mod_row0 is None:
        assert tok_per_row % t == 0
        tiles_per_row = tok_per_row // t
        mod_spec = pl.BlockSpec((None, N_MOD, d), lambda i, e: (i // tiles_per_row, 0, 0))
    else:
        mod_spec = pl.BlockSpec((None, N_MOD, d), lambda i, e: (mod_row0, 0, 0))
    tw = t + LANES
    tab_f = pltpu.VMEM((PEER_HEADS, PEER_N_KEYS, tw), F32)
    tab_i2 = pltpu.VMEM((PEER_HEADS, 2, PEER_N_KEYS + 16, tw), BF)
    return pl.pallas_call(
        functools.partial(_peer_kernel, final=final),
        out_shape=jax.ShapeDtypeStruct((ntok, d), F32),
        grid=(ntok // t, n_blk),
        in_specs=[
            pl.BlockSpec((t, d), lambda i, e: (i, 0)),
            pl.BlockSpec((t, d), lambda i, e: (i, 0)),
            mod_spec,
            pl.BlockSpec((1, d), lambda i, e: (0, 0)),
            pl.BlockSpec(lw["peer_wq"].shape, lambda i, e: (0, 0), pipeline_mode=pl.Buffered(1)),
            pl.BlockSpec(lw["peer_sk"].shape, lambda i, e: (0, 0, 0, 0)),
            pl.BlockSpec((PEER_EBLK, d), lambda i, e: (e, 0)),
            pl.BlockSpec((None, d, PEER_EBLK), lambda i, e: (e, 0, 0)),
        ],
        out_specs=pl.BlockSpec((t, d), lambda i, e: (i, 0)),
        scratch_shapes=[
            pltpu.VMEM((d, t), F32),
            pltpu.VMEM((PEER_EBLK, tw), F32),
            pltpu.VMEM((PEER_EBLK, tw), BF),
            tab_f, tab_f, tab_i2,
            pltpu.VMEM((PEER_HEADS, 2, PEER_N_KEYS, tw), F32),
            pltpu.VMEM((2, PEER_TOPK, tw), F32),
            pltpu.VMEM((PEER_CAND_ROWS, tw), F32),
            pltpu.VMEM((t, PEER_HEADS * PEER_QUERY_DIM), BF),
        ],
        compiler_params=_cparams(("parallel", "arbitrary"), LARGE_VMEM_LIMIT),
        name="peer",
    )(hn, x_mid, mod, final_g, lw["peer_wq"], lw["peer_sk"], lw["peer_u"], lw["peer_vt"])


def _rope_tables(l):
    rows = l // GRID_W
    row = jnp.repeat(jnp.arange(rows), GRID_W).astype(F32)
    col = jnp.tile(jnp.arange(GRID_W), rows).astype(F32)

    def cs(rot_dim):
        n = rot_dim // 4
        inv = ROPE_BASE ** (-jnp.arange(n, dtype=F32) / n)
        ang = jnp.concatenate([row[:, None] * inv, col[:, None] * inv], axis=-1)
        return jnp.cos(ang), jnp.sin(ang)

    ones = lambda w: jnp.ones((l, w), F32)
    zeros = lambda w: jnp.zeros((l, w), F32)
    def block_tables(c, s):
        pad = ROLL_HALF - c.shape[1]
        return (jnp.concatenate([c, ones(pad), c, ones(pad)], axis=1),
                jnp.concatenate([-s, zeros(pad), s, zeros(pad)], axis=1))

    cos_a, sin_a = block_tables(*cs(HEAD_DIM))
    cos_c, sin_c = block_tables(*cs(MLA_ROPE_DIM))
    return cos_a, sin_a, cos_c, sin_c


def _peer_vt_blocks(v_tab):
    n, d = v_tab.shape
    return v_tab.astype(BF).reshape(n // PEER_EBLK, PEER_EBLK, d).transpose(0, 2, 1)


def _head_block(rot, nope):
    rows = (rot if rot is not None else nope).shape[0]
    n = 0 if rot is None else rot.shape[1] // 2
    m = 0 if nope is None else nope.shape[1]
    low = min(m, ROLL_HALF - n)
    parts = []
    if n:
        parts.append(rot[:, :n])
    parts.append(nope[:, :low] if m else jnp.zeros((rows, 0), F32))
    parts.append(jnp.zeros((rows, ROLL_HALF - n - low), F32))
    if n:
        parts.append(rot[:, n:])
    parts.append(nope[:, low:] if m else jnp.zeros((rows, 0), F32))
    parts.append(jnp.zeros((rows, ROLL_HALF - n - (m - low)), F32))
    blk = jnp.concatenate(parts, axis=1)
    assert blk.shape == (rows, LANES)
    return blk


def _layer_weights(layer, norm1_g, norm2_g, w_in, gqa_qn_g, gqa_kn_g, pool_w, pool_scale, mla_qn_g, mla_kvn_g,
                   mla_w_uq, mla_w_ukv, w_out, peer_wq, peer_subkeys, peer_u, peer_v):
    d = D_MODEL
    w = w_in[layer]
    o = 0
    aq = w[:, o:o + GQA_Q_HEADS * HEAD_DIM]; o += GQA_Q_HEADS * HEAD_DIM
    ak = w[:, o:o + GQA_KV_HEADS * HEAD_DIM]; o += GQA_KV_HEADS * HEAD_DIM
    av = w[:, o:o + GQA_KV_HEADS * HEAD_DIM]; o += GQA_KV_HEADS * HEAD_DIM
    wb = w[:, o:o + POOL_WIDTH]; o += POOL_WIDTH
    wcq = w[:, o:o + MLA_Q_RANK]; o += MLA_Q_RANK
    wckv = w[:, o:o + MLA_KV_RANK]; o += MLA_KV_RANK
    wkr = w[:, o:o + MLA_ROPE_DIM]
    cols = []
    for h in range(GQA_Q_HEADS):
        cols.append(_head_block(aq[:, h * HEAD_DIM:(h + 1) * HEAD_DIM], None))
    for g in range(GQA_KV_HEADS):
        cols.append(_head_block(ak[:, g * HEAD_DIM:(g + 1) * HEAD_DIM], None))
    for h in range(GQA_Q_HEADS):
        g = h // GQA_GROUP
        cols.append(av[:, g * HEAD_DIM:(g + 1) * HEAD_DIM])
    cols += [wb, wcq, wckv]
    kr_blk = _head_block(wkr, jnp.zeros((d, MLA_NOPE_DIM), F32))
    cols.append(kr_blk)
    w_wide = jnp.concatenate(cols, axis=1).astype(BF)
    assert w_wide.shape == (d, IN_WIDE)

    def head_gain(g):
        return _head_block(g[None, :], None)

    qd = MLA_NOPE_DIM + MLA_ROPE_DIM
    uq = mla_w_uq[layer]
    ukv = mla_w_ukv[layer]
    uq_cols, uk_cols, uv_cols = [], [], []
    zero_rot = jnp.zeros((MLA_KV_RANK, MLA_ROPE_DIM), F32)
    for h in range(MLA_HEADS):
        q_h = uq[:, h * qd:(h + 1) * qd]
        uq_cols.append(_head_block(q_h[:, MLA_NOPE_DIM:], q_h[:, :MLA_NOPE_DIM]))
        k0 = h * (MLA_NOPE_DIM + MLA_V_DIM)
        uk_cols.append(_head_block(zero_rot, ukv[:, k0:k0 + MLA_NOPE_DIM]))
        uv_cols += [ukv[:, k0 + MLA_NOPE_DIM:k0 + MLA_NOPE_DIM + MLA_V_DIM]]

    pw = pool_w[layer]
    w_bd = jnp.zeros((POOL_WIDTH, POOL_WIDTH), F32)
    for g in range(POOL_GROUPS):
        s = slice(g * POOL_GROUP_DIM, (g + 1) * POOL_GROUP_DIM)
        w_bd = w_bd.at[s, s].set(pw[g])

    wo = w_out[layer]
    na = GQA_Q_HEADS * HEAD_DIM
    w_o1 = jnp.concatenate([wo[0:na], wo[na + POOL_WIDTH:]], axis=0).astype(BF)
    w_o2 = wo[na:na + POOL_WIDTH].astype(BF)

    return {
        "n1_g": norm1_g[layer][None, :],
        "n2_g": norm2_g[layer][None, :],
        "w_in": w_wide,
        "gqa_qg": head_gain(gqa_qn_g[layer]),
        "gqa_kg": head_gain(gqa_kn_g[layer]),
        "mla_qg": mla_qn_g[layer][None, :],
        "mla_kvg": mla_kvn_g[layer][None, :],
        "w_uq": jnp.concatenate(uq_cols, axis=1).astype(BF),
        "w_uk": jnp.concatenate(uk_cols, axis=1).astype(BF),
        "w_uv": jnp.concatenate(uv_cols, axis=1).astype(BF),
        "pool_w": w_bd.astype(BF),
        "pool_scale": pool_scale[layer][None, :],
        "w_o1": w_o1,
        "w_o2": w_o2,
        "peer_wq": peer_wq[layer].astype(BF),
        "peer_sk": peer_subkeys[layer].astype(BF),
        "peer_u": peer_u[layer].astype(BF),
        "peer_vt": _peer_vt_blocks(peer_v[layer]),
    }


def _tile(l, pref):
    return pref if l % pref == 0 else l


def kernel(x, c, ctx, c_ctx, ada_w, ada_b, norm1_g, norm2_g, w_in, gqa_qn_g, gqa_kn_g, pool_w, pool_scale,
           mla_qn_g, mla_kvn_g, mla_w_uq, mla_w_ukv, w_out, peer_wq, peer_subkeys, peer_u, peer_v, final_g):
    b, l, d = x.shape
    lc = ctx.shape[1]
    depth = ada_w.shape[0]
    assert d == D_MODEL and l % GRID_W == 0
    assert (b * l) % PEER_TOK == 0 and (b * lc) % PEER_TOK == 0

    n_rows = -(-(b + 1) // 8) * 8
    cc = jnp.concatenate([c, c_ctx[None, :], jnp.zeros((n_rows - b - 1, d), F32)], axis=0)
    mod_all = _adaln(cc, ada_w, ada_b).reshape(depth, n_rows, N_MOD, d)

    rope_tabs = _rope_tables(l)
    tl_l, tl_c = _tile(l, 1024), _tile(lc, 512)
    ti_l, ti_c = _tile(l, 256), _tile(lc, 256)
    tq_l, tq_c = _tile(l, 512), _tile(lc, 256)
    fg = final_g[None, :]

    xl, xc = x, ctx
    for layer in range(depth):
        lw = _layer_weights(layer, norm1_g, norm2_g, w_in, gqa_qn_g, gqa_kn_g, pool_w, pool_scale, mla_qn_g,
                            mla_kvn_g, mla_w_uq, mla_w_ukv, w_out, peer_wq, peer_subkeys, peer_u, peer_v)
        mod = mod_all[layer]
        last = layer == depth - 1
        q_l, k_l, v_l, pb_l = _inproj(xl, mod, None, lw, rope_tabs, ti_l)
        q_c, k_c, v_c, pb_c = _inproj(xc, mod, b, lw, None, ti_c)

        att_l = _attn(q_l, [k_c, k_l], [v_c, v_l], tq_l)
        ob_l = _pool(pb_l, lw["pool_w"], lw["pool_scale"])
        xl_mid, hn_l = _outproj(xl, att_l, ob_l, mod, None, lw, tl_l)
        xl = _peer(hn_l.reshape(b * l, d), xl_mid.reshape(b * l, d), mod, None, l, fg, last, lw).reshape(b, l, d)

        if not last:
            att_c = _attn(q_c, [k_c], [v_c], tq_c)
            ob_c = _pool(pb_c, lw["pool_w"], lw["pool_scale"])
            xc_mid, hn_c = _outproj(xc, att_c, ob_c, mod, b, lw, tl_c)
            xc = _peer(hn_c.reshape(b * lc, d), xc_mid.reshape(b * lc, d), mod, b, lc, fg, False,
                       lw).reshape(b, lc, d)
    return xl
```

```python
import functools

import jax
import jax.numpy as jnp
from jax import lax
from jax.experimental import pallas as pl
from jax.experimental.pallas import tpu as pltpu

D_MODEL = 1024
GRID_W = 64
N_MOD = 6
EPS = 1e-6
ROPE_BASE = 10000.0
HEAD_DIM = 64
GQA_Q_HEADS = 6
GQA_KV_HEADS = 2
GQA_GROUP = GQA_Q_HEADS // GQA_KV_HEADS
POOL_GROUPS = 4
POOL_WINDOWS = (2, 4, 8, 16)
POOL_WIDTH = D_MODEL // 4
POOL_GROUP_DIM = POOL_WIDTH // POOL_GROUPS
MLA_HEADS = 6
MLA_NOPE_DIM = 64
MLA_ROPE_DIM = 32
MLA_V_DIM = 64
MLA_Q_RANK = 384
MLA_KV_RANK = 256
PEER_HEADS = 8
PEER_N_KEYS = 128
PEER_N_EXPERTS = PEER_N_KEYS * PEER_N_KEYS
PEER_TOPK = 16
PEER_QUERY_DIM = 256
PEER_HALF = PEER_QUERY_DIM // 2

LANES = 128
N_ATT_HEADS = GQA_Q_HEADS + MLA_HEADS
QK_WIDTH = N_ATT_HEADS * LANES
V_WIDTH = N_ATT_HEADS * HEAD_DIM
POOL_PAD = 16

_C_QA = 0
_C_KA = _C_QA + GQA_Q_HEADS * LANES
_C_VA = _C_KA + GQA_KV_HEADS * LANES
_C_B = _C_VA + GQA_Q_HEADS * HEAD_DIM
_C_CQ = _C_B + POOL_WIDTH
_C_CKV = _C_CQ + MLA_Q_RANK
_C_KR = _C_CKV + MLA_KV_RANK
IN_WIDE = _C_KR + LANES
ROLL_HALF = LANES // 2

PEER_TOK = 512
PEER_EBLK = 2048
PEER_I1_PER_BLK = PEER_EBLK // PEER_N_KEYS
PEER_CAND_ROWS = 2 * PEER_TOPK + 5 * 8
VMEM_LIMIT = 56 * 1024 * 1024
LARGE_VMEM_LIMIT = 60 * 1024 * 1024

BF = jnp.bfloat16
F32 = jnp.float32
LOG2_E = 1.4426950408889634
NEG_INF = float("-inf")


def _cparams(sem, vmem_limit=VMEM_LIMIT):
    return pltpu.CompilerParams(dimension_semantics=sem, vmem_limit_bytes=vmem_limit)


def _dot(a, b):
    return jnp.dot(a, b, preferred_element_type=F32)


def _dot_nt(a, b):
    return lax.dot_general(a, b, (((1,), (1,)), ((), ())), preferred_element_type=F32)


def _split_bf16(a):
    hi = a.astype(BF)
    lo = (a - hi.astype(F32)).astype(BF)
    return hi, lo


def _dot3(a, b):
    ah, al = _split_bf16(a)
    bh, bl = _split_bf16(b)
    return _dot(ah, bh) + _dot(ah, bl) + _dot(al, bh)


def _rms(x):
    return x * lax.rsqrt(jnp.mean(x * x, axis=-1, keepdims=True) + EPS)


def _adaln_kernel(c_ref, w_ref, b_ref, o_ref):
    c = c_ref[...]
    s = c * (1.0 / (1.0 + jnp.exp(-c)))
    o_ref[...] = _dot3(s, w_ref[...]) + b_ref[...]


def _adaln(cc, ada_w, ada_b):
    depth, d, nd = ada_w.shape
    r = cc.shape[0]
    nblk = nd // d
    return pl.pallas_call(
        _adaln_kernel,
        out_shape=jax.ShapeDtypeStruct((depth, r, nd), F32),
        grid=(depth, nblk),
        in_specs=[
            pl.BlockSpec((r, d), lambda l, j: (0, 0)),
            pl.BlockSpec((None, d, d), lambda l, j: (l, 0, j)),
            pl.BlockSpec((None, 1, d), lambda l, j: (l, 0, j)),
        ],
        out_specs=pl.BlockSpec((None, r, d), lambda l, j: (l, 0, j)),
        compiler_params=_cparams(("arbitrary", "arbitrary")),
        name="adaln",
    )(cc, ada_w, ada_b.reshape(depth, 1, nd))


def _rope_block(xb, cb, sb):
    return xb * cb + pltpu.roll(xb, ROLL_HALF, 1) * sb


def _inproj_kernel(*refs, rope):
    if rope:
        (x_ref, mod_ref, n1_ref, w_ref, qg_ref, kg_ref, cqg_ref, ckvg_ref, wuq_ref, wuk_ref, wuv_ref,
         ca_ref, sa_ref, cc_ref, sc_ref, q_ref, k_ref, v_ref, b_ref) = refs
    else:
        (x_ref, mod_ref, n1_ref, w_ref, qg_ref, kg_ref, cqg_ref, ckvg_ref, wuq_ref, wuk_ref, wuv_ref,
         q_ref, k_ref, v_ref, b_ref) = refs
    x = x_ref[...]
    shift = mod_ref[0:1, :]
    scale = mod_ref[1:2, :]
    h = _rms(x) * n1_ref[...] * (1.0 + scale) + shift
    p = _dot(h.astype(BF), w_ref[...])

    sa = HEAD_DIM ** -0.5 * LOG2_E
    sc = (MLA_NOPE_DIM + MLA_ROPE_DIM) ** -0.5 * LOG2_E

    def gqa_head(c0, g_ref):
        blk = p[:, c0:c0 + LANES]
        ms = jnp.sum(blk * blk, axis=-1, keepdims=True) * (1.0 / HEAD_DIM)
        y = blk * lax.rsqrt(ms + EPS) * g_ref[...]
        if rope:
            y = _rope_block(y, ca_ref[...], sa_ref[...])
        return y

    for hh in range(GQA_Q_HEADS):
        q_ref[:, hh * LANES:(hh + 1) * LANES] = (gqa_head(_C_QA + hh * LANES, qg_ref) * sa).astype(BF)
    for g in range(GQA_KV_HEADS):
        kb = gqa_head(_C_KA + g * LANES, kg_ref).astype(BF)
        for hh in range(g * GQA_GROUP, (g + 1) * GQA_GROUP):
            k_ref[:, hh * LANES:(hh + 1) * LANES] = kb
    nva = GQA_Q_HEADS * HEAD_DIM
    v_ref[:, 0:nva] = p[:, _C_VA:_C_VA + nva].astype(BF)
    b_ref[...] = p[:, _C_B:_C_B + POOL_WIDTH]

    cq = _rms(p[:, _C_CQ:_C_CQ + MLA_Q_RANK]) * cqg_ref[...]
    ckv = (_rms(p[:, _C_CKV:_C_CKV + MLA_KV_RANK]) * ckvg_ref[...]).astype(BF)
    qc = _dot(cq.astype(BF), wuq_ref[...])
    kc = _dot(ckv, wuk_ref[...])
    vc = _dot(ckv, wuv_ref[...])
    kr = p[:, _C_KR:_C_KR + LANES]
    if rope:
        kr = _rope_block(kr, cc_ref[...], sc_ref[...])
    base = GQA_Q_HEADS * LANES
    for hh in range(MLA_HEADS):
        qb = qc[:, hh * LANES:(hh + 1) * LANES]
        if rope:
            qb = _rope_block(qb, cc_ref[...], sc_ref[...])
        q_ref[:, base + hh * LANES:base + (hh + 1) * LANES] = (qb * sc).astype(BF)
        k_ref[:, base + hh * LANES:base + (hh + 1) * LANES] = (kc[:, hh * LANES:(hh + 1) * LANES] + kr).astype(BF)
    v_ref[:, nva:V_WIDTH] = vc.astype(BF)


def _inproj(x, mod, mod_row0, lw, rope_tabs, tl):
    b, l, d = x.shape
    rope = rope_tabs is not None
    grid = (b, l // tl)

    def full(a):
        nd = a.ndim
        return pl.BlockSpec(a.shape, lambda i, j: (0,) * nd)

    if mod_row0 is None:
        mod_spec = pl.BlockSpec((None, N_MOD, d), lambda i, j: (i, 0, 0))
    else:
        mod_spec = pl.BlockSpec((None, N_MOD, d), lambda i, j: (mod_row0, 0, 0))
    weights = [lw["n1_g"], lw["w_in"], lw["gqa_qg"], lw["gqa_kg"], lw["mla_qg"], lw["mla_kvg"],
               lw["w_uq"], lw["w_uk"], lw["w_uv"]]
    in_specs = [pl.BlockSpec((None, tl, d), lambda i, j: (i, j, 0)), mod_spec] + [full(a) for a in weights]
    args = [x, mod] + weights
    if rope:
        in_specs += [pl.BlockSpec((tl, LANES), lambda i, j: (j, 0)) for _ in range(4)]
        args += list(rope_tabs)
    def rows(w, dt):
        return jax.ShapeDtypeStruct((b, l, w), dt), pl.BlockSpec((None, tl, w), lambda i, j: (i, j, 0))

    outs = [rows(QK_WIDTH, BF), rows(QK_WIDTH, BF), rows(V_WIDTH, BF), rows(POOL_WIDTH, F32)]
    return pl.pallas_call(
        functools.partial(_inproj_kernel, rope=rope),
        out_shape=[o[0] for o in outs],
        grid=grid,
        in_specs=in_specs,
        out_specs=[o[1] for o in outs],
        compiler_params=_cparams(("parallel", "parallel")),
        name="inproj",
    )(*args)


def _attn_kernel(*refs, n_src):
    q_ref = refs[0]
    k_refs = refs[1:1 + n_src]
    v_refs = refs[1 + n_src:1 + 2 * n_src]
    o_ref = refs[1 + 2 * n_src]
    tq = q_ref.shape[0]
    lane = lax.broadcasted_iota(jnp.int32, (tq, LANES), 1)
    for jb in range(N_ATT_HEADS // 2):
        halves = []
        for n in (2 * jb, 2 * jb + 1):
            q = q_ref[:, n * LANES:(n + 1) * LANES]
            ss = [_dot_nt(q, k_ref[:, n * LANES:(n + 1) * LANES]) for k_ref in k_refs]
            m = ss[0].max(axis=-1, keepdims=True)
            for s in ss[1:]:
                m = jnp.maximum(m, s.max(axis=-1, keepdims=True))
            acc = None
            den = None
            for s, v_ref in zip(ss, v_refs):
                e = jnp.exp2(s - m)
                dsum = e.sum(axis=-1, keepdims=True)
                pv = _dot(e.astype(BF), v_ref[:, jb * LANES:(jb + 1) * LANES])
                acc = pv if acc is None else acc + pv
                den = dsum if den is None else den + dsum
            halves.append(acc * (1.0 / den))
        o = jnp.where(lane < HEAD_DIM, halves[0], halves[1])
        o_ref[:, jb * LANES:(jb + 1) * LANES] = o.astype(BF)


def _attn(q, ks, vs, tq):
    b, l, _ = q.shape
    n_src = len(ks)
    once = pl.Buffered(1)
    in_specs = [pl.BlockSpec((None, tq, QK_WIDTH), lambda i, j: (i, j, 0))]
    in_specs += [pl.BlockSpec((None, k.shape[1], QK_WIDTH), lambda i, j: (i, 0, 0), pipeline_mode=once) for k in ks]
    in_specs += [pl.BlockSpec((None, v.shape[1], V_WIDTH), lambda i, j: (i, 0, 0), pipeline_mode=once) for v in vs]
    return pl.pallas_call(
        functools.partial(_attn_kernel, n_src=n_src),
        out_shape=jax.ShapeDtypeStruct((b, l, V_WIDTH), BF),
        grid=(b, l // tq),
        in_specs=in_specs,
        out_specs=pl.BlockSpec((None, tq, V_WIDTH), lambda i, j: (i, j, 0)),
        compiler_params=_cparams(("parallel", "arbitrary"), LARGE_VMEM_LIMIT),
        name="attn",
    )(q, *ks, *vs)


def _pool_kernel(b_ref, w_ref, s_ref, o_ref, xp_ref):
    l = b_ref.shape[0]
    x = b_ref[...]
    zeros = jnp.zeros((POOL_PAD, POOL_WIDTH), F32)
    xp_ref[0:POOL_PAD, :] = zeros
    xp_ref[POOL_PAD + l:POOL_PAD + l + POOL_PAD, :] = zeros
    xp_ref[POOL_PAD:POOL_PAD + l, :] = x
    t = lax.broadcasted_iota(jnp.int32, (l, LANES), 0)
    lane = lax.broadcasted_iota(jnp.int32, (l, LANES), 1)
    outs = []
    for half in range(POOL_WIDTH // LANES):
        w_small = POOL_WINDOWS[2 * half]
        w_big = POOL_WINDOWS[2 * half + 1]
        lo, hi = half * LANES, (half + 1) * LANES

        def win(j):
            return xp_ref[pl.ds(POOL_PAD + j, l), lo:hi]

        s_small = None
        for j in range(-(w_small // 2), w_small // 2):
            s_small = win(j) if s_small is None else s_small + win(j)
        s_big = s_small
        for j in range(-(w_big // 2), w_big // 2):
            if not (-(w_small // 2) <= j < w_small // 2):
                s_big = s_big + win(j)

        def cnt(w):
            lo_i = jnp.maximum(t - w // 2, 0)
            hi_i = jnp.minimum(t - w // 2 + w, l)
            return (hi_i - lo_i).astype(F32)

        left = lane < POOL_GROUP_DIM
        s = jnp.where(left, s_small, s_big)
        c = jnp.where(left, cnt(w_small), cnt(w_big))
        outs.append(s / c - x[:, lo:hi])
    pooled = jnp.concatenate(outs, axis=1).astype(BF)
    o_ref[...] = (_dot(pooled, w_ref[...]) * s_ref[...]).astype(BF)


def _pool(pb, w_bd, pscale):
    b, l, w = pb.shape
    return pl.pallas_call(
        _pool_kernel,
        out_shape=jax.ShapeDtypeStruct((b, l, w), BF),
        grid=(b,),
        in_specs=[
            pl.BlockSpec((None, l, w), lambda i: (i, 0, 0)),
            pl.BlockSpec((w, w), lambda i: (0, 0)),
            pl.BlockSpec((1, w), lambda i: (0, 0)),
        ],
        out_specs=pl.BlockSpec((None, l, w), lambda i: (i, 0, 0)),
        scratch_shapes=[pltpu.VMEM((l + 2 * POOL_PAD, w), F32)],
        compiler_params=_cparams(("parallel",)),
        name="pool",
    )(pb, w_bd, pscale)


def _outproj_kernel(x_ref, att_ref, ob_ref, mod_ref, n2_ref, w1_ref, w2_ref, xo_ref, h_ref):
    y = _dot(att_ref[...], w1_ref[...]) + _dot(ob_ref[...], w2_ref[...])
    xn = x_ref[...] + mod_ref[2:3, :] * y
    xo_ref[...] = xn
    h = _rms(xn) * n2_ref[...] * (1.0 + mod_ref[4:5, :]) + mod_ref[3:4, :]
    h_ref[...] = h.astype(BF)


def _outproj(x, att, ob, mod, mod_row0, lw, tl):
    b, l, d = x.shape
    if mod_row0 is None:
        mod_spec = pl.BlockSpec((None, N_MOD, d), lambda i, j: (i, 0, 0))
    else:
        mod_spec = pl.BlockSpec((None, N_MOD, d), lambda i, j: (mod_row0, 0, 0))

    def tile(w):
        return pl.BlockSpec((None, tl, w), lambda i, j: (i, j, 0))

    def full(a):
        return pl.BlockSpec(a.shape, lambda i, j: (0, 0))

    return pl.pallas_call(
        _outproj_kernel,
        out_shape=[jax.ShapeDtypeStruct((b, l, d), F32), jax.ShapeDtypeStruct((b, l, d), BF)],
        grid=(b, l // tl),
        in_specs=[tile(d), tile(V_WIDTH), tile(POOL_WIDTH), mod_spec, full(lw["n2_g"]), full(lw["w_o1"]),
                  full(lw["w_o2"])],
        out_specs=[tile(d), tile(d)],
        compiler_params=_cparams(("parallel", "parallel")),
        name="outproj",
    )(x, att, ob, mod, lw["n2_g"], lw["w_o1"], lw["w_o2"])


def _gelu_tanh(x):
    return 0.5 * x * (1.0 + jnp.tanh(0.7978845608028654 * (x + 0.044715 * (x * x * x))))


def _sort16_network():
    def merge(lo, hi, r):
        step = r * 2
        if step < hi - lo:
            yield from merge(lo, hi, step)
            yield from merge(lo + r, hi, step)
            yield from [(i, i + r) for i in range(lo + r, hi - r, step)]
        else:
            yield (lo, lo + r)

    def sort(lo, hi):
        if hi - lo >= 1:
            mid = lo + (hi - lo) // 2
            yield from sort(lo, mid)
            yield from sort(mid + 1, hi)
            yield from merge(lo, hi, 1)

    return tuple(sort(0, PEER_TOPK - 1))


def _top16_desc(blocks):
    a = list(blocks)
    for i, j in _sort16_network():
        a[i], a[j] = jnp.maximum(a[i], a[j]), jnp.minimum(a[i], a[j])
    for shift in (4, 2, 1):
        c = [jnp.maximum(a[i], pltpu.roll(a[PEER_TOPK - 1 - i], shift, 0)) for i in range(PEER_TOPK)]
        stride = PEER_TOPK // 2
        while stride >= 1:
            for i in range(PEER_TOPK):
                if i & stride == 0:
                    c[i], c[i + stride] = jnp.maximum(c[i], c[i + stride]), jnp.minimum(c[i], c[i + stride])
            stride //= 2
        a = c
    return a


def _rank_in_top16(s, tops):
    def pick(bits, lo, step):
        if not bits:
            return tops[lo]
        (b, w), rest = bits[0], bits[1:]
        return jnp.where(b, pick(rest, lo + w, step), pick(rest, lo, step))

    bits = []
    rank = jnp.zeros(s.shape, F32)
    for w in (8, 4, 2, 1):
        b = s < pick(bits, w - 1, w)
        bits.append((b, w))
        rank = rank + jnp.where(b, float(w), 0.0)
    return jnp.where(s < tops[PEER_TOPK - 1], float(PEER_TOPK), rank)


def _peer_prologue(hn_ref, wq_ref, sk_ref, c1_ref, e1_ref, i2_ref, s_ref, q_ref):
    t = hn_ref.shape[0]
    q_ref[...] = _dot(hn_ref[...], wq_ref[...]).astype(BF)
    for hh in range(PEER_HEADS):
        for p in range(2):
            c0 = hh * PEER_QUERY_DIM + p * PEER_HALF
            s_ref[hh, p, :, 0:t] = _dot_nt(sk_ref[hh, p], q_ref[:, c0:c0 + PEER_HALF])

    sub8 = lax.broadcasted_iota(jnp.int32, (8, t), 0)
    sub16 = lax.broadcasted_iota(jnp.int32, (PEER_TOPK, t), 0)
    zeros8 = jnp.zeros((8, t), F32)

    def head(h, carry):
        r2 = None
        vs = []
        for p in range(2):
            st = s_ref[h, p, :, 0:t]
            top = _top16_desc([st[8 * v:8 * v + 8, :] for v in range(PEER_N_KEYS // 8)])
            rows = [top[k][0:1, :] for k in range(PEER_TOPK)]
            vs.append(jnp.concatenate(rows, axis=0))
            if p == 1:
                r2 = _rank_in_top16(st, rows)
        v1, v2 = vs

        groups = [
            v1[0:1, :] + v2,
            jnp.where(sub16 >= 1, v1 + v2[0:1, :], NEG_INF),
            jnp.where(sub8 >= 1, v1[1:2, :] + v2[0:8, :], NEG_INF),
            jnp.where(sub8 >= 2, v1[0:8, :] + v2[1:2, :], NEG_INF),
            jnp.where((sub8 >= 2) & (sub8 <= 4), v1[2:3, :] + v2[0:8, :], NEG_INF),
            jnp.where((sub8 >= 3) & (sub8 <= 4), v1[0:8, :] + v2[2:3, :], NEG_INF),
            jnp.where(sub8 == 3, v1[3:4, :] + v2[0:8, :], NEG_INF),
        ]
        c = jnp.concatenate(groups, axis=0)
        c00 = v1[0:1, :] + v2[0:1, :]

        neg = jnp.full((8, t), NEG_INF, F32)
        cblocks = [c[8 * v:8 * v + 8, :] for v in range(PEER_CAND_ROWS // 8)]
        tau = _top16_desc(cblocks + [neg] * (PEER_TOPK - len(cblocks)))[PEER_TOPK - 1][0:1, :]
        sel = c >= tau
        z = jnp.where(sel, jnp.exp(c - c00), 0.0).sum(axis=0, keepdims=True)
        ind = jnp.where(sel, 1.0, 0.0)

        def row_total(r0, rows, a):
            return jnp.where(sub16 == a, ind[r0:r0 + rows, :].sum(axis=0, keepdims=True), 0.0)

        cnt = (ind[16:32, :] + jnp.concatenate([ind[40:48, :] + ind[56:64, :], zeros8], axis=0)
               + row_total(0, 16, 0) + row_total(32, 8, 1) + row_total(48, 8, 2) + row_total(64, 8, 3))
        s1 = s_ref[h, 0, :, 0:t]
        s2 = s_ref[h, 1, :, 0:t]
        c1 = jnp.zeros((PEER_N_KEYS, t), F32)
        for a in range(PEER_TOPK):
            c1 = jnp.where(s1 == v1[a:a + 1, :], cnt[a:a + 1, :], c1)
        c1_ref[h, :, 0:t] = c1
        e1_ref[h, :, 0:t] = jnp.exp(s1 - v1[0:1, :]) * (1.0 / z)
        i2_ref[h, 0, 0:PEER_N_KEYS, 0:t] = r2.astype(BF)
        i2_ref[h, 1, 0:PEER_N_KEYS, 0:t] = jnp.exp(s2 - v2[0:1, :]).astype(BF)
        return carry

    lax.fori_loop(0, PEER_HEADS, head, 0)


def _peer_dense(at_ref, wt_ref, blk, t, c1_ref, e1_ref, i2_ref):
    i1_0 = pl.multiple_of(blk * PEER_I1_PER_BLK, PEER_I1_PER_BLK)
    zero = jnp.zeros((), BF)
    for lg in range(t // LANES):
        ls = slice(lg * LANES, (lg + 1) * LANES)
        c1_s = [c1_ref[h, pl.ds(i1_0, PEER_I1_PER_BLK), ls].astype(BF) for h in range(PEER_HEADS)]
        e1_s = [e1_ref[h, pl.ds(i1_0, PEER_I1_PER_BLK), ls].astype(BF) for h in range(PEER_HEADS)]
        for j in range(PEER_I1_PER_BLK):
            rs = slice(j * PEER_N_KEYS, (j + 1) * PEER_N_KEYS)
            g = jnp.zeros((PEER_N_KEYS, LANES), BF)
            for h in range(PEER_HEADS):
                c1 = jnp.broadcast_to(c1_s[h][j:j + 1, :], (PEER_N_KEYS, LANES))
                e1 = jnp.broadcast_to(e1_s[h][j:j + 1, :], (PEER_N_KEYS, LANES))
                gate1 = jnp.minimum(jnp.maximum(c1 - i2_ref[h, 0, 0:PEER_N_KEYS, ls], zero), e1)
                g = g + gate1 * i2_ref[h, 1, 0:PEER_N_KEYS, ls]
            wt_ref[rs, ls] = _gelu_tanh(at_ref[rs, ls].astype(BF)) * g


def _peer_kernel(hn_ref, x_ref, mod_ref, fg_ref, wq_ref, sk_ref, u_ref, vt_ref, o_ref,
                 acc_ref, at_ref, wt_ref, c1_ref, e1_ref, i2_ref, s_ref, q_ref, *, final):
    e = pl.program_id(1)
    t = hn_ref.shape[0]

    @pl.when(e == 0)
    def _():
        _peer_prologue(hn_ref, wq_ref, sk_ref, c1_ref, e1_ref, i2_ref, s_ref, q_ref)
        acc_ref[...] = jnp.zeros_like(acc_ref)

    at_ref[:, 0:t] = _dot_nt(u_ref[...], hn_ref[...])
    _peer_dense(at_ref, wt_ref, e, t, c1_ref, e1_ref, i2_ref)
    acc_ref[...] += _dot(vt_ref[...], wt_ref[:, 0:t])

    @pl.when(e == pl.num_programs(1) - 1)
    def _():
        y = x_ref[...] + mod_ref[5:6, :] * acc_ref[...].T
        if final:
            y = _rms(y) * fg_ref[...]
        o_ref[...] = y


def _peer(hn, x_mid, mod, mod_row0, tok_per_row, final_g, final, lw):
    ntok, d = hn.shape
    t = PEER_TOK
    n_blk = PEER_N_EXPERTS // PEER_EBLK
    if mod_row0 is None:
        assert tok_per_row % t == 0
        tiles_per_row = tok_per_row // t
        mod_spec = pl.BlockSpec((None, N_MOD, d), lambda i, e: (i // tiles_per_row, 0, 0))
    else:
        mod_spec = pl.BlockSpec((None, N_MOD, d), lambda i, e: (mod_row0, 0, 0))
    tw = t + LANES
    tab_f = pltpu.VMEM((PEER_HEADS, PEER_N_KEYS, tw), F32)
    tab_i2 = pltpu.VMEM((PEER_HEADS, 2, PEER_N_KEYS + 16, tw), BF)
    return pl.pallas_call(
        functools.partial(_peer_kernel, final=final),
        out_shape=jax.ShapeDtypeStruct((ntok, d), F32),
        grid=(ntok // t, n_blk),
        in_specs=[
            pl.BlockSpec((t, d), lambda i, e: (i, 0)),
            pl.BlockSpec((t, d), lambda i, e: (i, 0)),
            mod_spec,
            pl.BlockSpec((1, d), lambda i, e: (0, 0)),
            pl.BlockSpec(lw["peer_wq"].shape, lambda i, e: (0, 0), pipeline_mode=pl.Buffered(1)),
            pl.BlockSpec(lw["peer_sk"].shape, lambda i, e: (0, 0, 0, 0)),
            pl.BlockSpec((PEER_EBLK, d), lambda i, e: (e, 0)),
            pl.BlockSpec((None, d, PEER_EBLK), lambda i, e: (e, 0, 0)),
        ],
        out_specs=pl.BlockSpec((t, d), lambda i, e: (i, 0)),
        scratch_shapes=[
            pltpu.VMEM((d, t), F32),
            pltpu.VMEM((PEER_EBLK, tw), F32),
            pltpu.VMEM((PEER_EBLK, tw), BF),
            tab_f, tab_f, tab_i2,
            pltpu.VMEM((PEER_HEADS, 2, PEER_N_KEYS, tw), F32),
            pltpu.VMEM((t, PEER_HEADS * PEER_QUERY_DIM), BF),
        ],
        compiler_params=_cparams(("parallel", "arbitrary"), LARGE_VMEM_LIMIT),
        name="peer",
    )(hn, x_mid, mod, final_g, lw["peer_wq"], lw["peer_sk"], lw["peer_u"], lw["peer_vt"])


def _rope_tables(l):
    rows = l // GRID_W
    row = jnp.repeat(jnp.arange(rows), GRID_W).astype(F32)
    col = jnp.tile(jnp.arange(GRID_W), rows).astype(F32)

    def cs(rot_dim):
        n = rot_dim // 4
        inv = ROPE_BASE ** (-jnp.arange(n, dtype=F32) / n)
        ang = jnp.concatenate([row[:, None] * inv, col[:, None] * inv], axis=-1)
        return jnp.cos(ang), jnp.sin(ang)

    ones = lambda w: jnp.ones((l, w), F32)
    zeros = lambda w: jnp.zeros((l, w), F32)
    def block_tables(c, s):
        pad = ROLL_HALF - c.shape[1]
        return (jnp.concatenate([c, ones(pad), c, ones(pad)], axis=1),
                jnp.concatenate([-s, zeros(pad), s, zeros(pad)], axis=1))

    cos_a, sin_a = block_tables(*cs(HEAD_DIM))
    cos_c, sin_c = block_tables(*cs(MLA_ROPE_DIM))
    return cos_a, sin_a, cos_c, sin_c


def _peer_vt_blocks(v_tab):
    n, d = v_tab.shape
    return v_tab.astype(BF).reshape(n // PEER_EBLK, PEER_EBLK, d).transpose(0, 2, 1)


def _head_block(rot, nope):
    rows = (rot if rot is not None else nope).shape[0]
    n = 0 if rot is None else rot.shape[1] // 2
    m = 0 if nope is None else nope.shape[1]
    low = min(m, ROLL_HALF - n)
    parts = []
    if n:
        parts.append(rot[:, :n])
    parts.append(nope[:, :low] if m else jnp.zeros((rows, 0), F32))
    parts.append(jnp.zeros((rows, ROLL_HALF - n - low), F32))
    if n:
        parts.append(rot[:, n:])
    parts.append(nope[:, low:] if m else jnp.zeros((rows, 0), F32))
    parts.append(jnp.zeros((rows, ROLL_HALF - n - (m - low)), F32))
    blk = jnp.concatenate(parts, axis=1)
    assert blk.shape == (rows, LANES)
    return blk


def _layer_weights(layer, norm1_g, norm2_g, w_in, gqa_qn_g, gqa_kn_g, pool_w, pool_scale, mla_qn_g, mla_kvn_g,
                   mla_w_uq, mla_w_ukv, w_out, peer_wq, peer_subkeys, peer_u, peer_v):
    d = D_MODEL
    w = w_in[layer]
    o = 0
    aq = w[:, o:o + GQA_Q_HEADS * HEAD_DIM]; o += GQA_Q_HEADS * HEAD_DIM
    ak = w[:, o:o + GQA_KV_HEADS * HEAD_DIM]; o += GQA_KV_HEADS * HEAD_DIM
    av = w[:, o:o + GQA_KV_HEADS * HEAD_DIM]; o += GQA_KV_HEADS * HEAD_DIM
    wb = w[:, o:o + POOL_WIDTH]; o += POOL_WIDTH
    wcq = w[:, o:o + MLA_Q_RANK]; o += MLA_Q_RANK
    wckv = w[:, o:o + MLA_KV_RANK]; o += MLA_KV_RANK
    wkr = w[:, o:o + MLA_ROPE_DIM]
    cols = []
    for h in range(GQA_Q_HEADS):
        cols.append(_head_block(aq[:, h * HEAD_DIM:(h + 1) * HEAD_DIM], None))
    for g in range(GQA_KV_HEADS):
        cols.append(_head_block(ak[:, g * HEAD_DIM:(g + 1) * HEAD_DIM], None))
    for h in range(GQA_Q_HEADS):
        g = h // GQA_GROUP
        cols.append(av[:, g * HEAD_DIM:(g + 1) * HEAD_DIM])
    cols += [wb, wcq, wckv]
    kr_blk = _head_block(wkr, jnp.zeros((d, MLA_NOPE_DIM), F32))
    cols.append(kr_blk)
    w_wide = jnp.concatenate(cols, axis=1).astype(BF)
    assert w_wide.shape == (d, IN_WIDE)

    def head_gain(g):
        return _head_block(g[None, :], None)

    qd = MLA_NOPE_DIM + MLA_ROPE_DIM
    uq = mla_w_uq[layer]
    ukv = mla_w_ukv[layer]
    uq_cols, uk_cols, uv_cols = [], [], []
    zero_rot = jnp.zeros((MLA_KV_RANK, MLA_ROPE_DIM), F32)
    for h in range(MLA_HEADS):
        q_h = uq[:, h * qd:(h + 1) * qd]
        uq_cols.append(_head_block(q_h[:, MLA_NOPE_DIM:], q_h[:, :MLA_NOPE_DIM]))
        k0 = h * (MLA_NOPE_DIM + MLA_V_DIM)
        uk_cols.append(_head_block(zero_rot, ukv[:, k0:k0 + MLA_NOPE_DIM]))
        uv_cols += [ukv[:, k0 + MLA_NOPE_DIM:k0 + MLA_NOPE_DIM + MLA_V_DIM]]

    pw = pool_w[layer]
    w_bd = jnp.zeros((POOL_WIDTH, POOL_WIDTH), F32)
    for g in range(POOL_GROUPS):
        s = slice(g * POOL_GROUP_DIM, (g + 1) * POOL_GROUP_DIM)
        w_bd = w_bd.at[s, s].set(pw[g])

    wo = w_out[layer]
    na = GQA_Q_HEADS * HEAD_DIM
    w_o1 = jnp.concatenate([wo[0:na], wo[na + POOL_WIDTH:]], axis=0).astype(BF)
    w_o2 = wo[na:na + POOL_WIDTH].astype(BF)

    return {
        "n1_g": norm1_g[layer][None, :],
        "n2_g": norm2_g[layer][None, :],
        "w_in": w_wide,
        "gqa_qg": head_gain(gqa_qn_g[layer]),
        "gqa_kg": head_gain(gqa_kn_g[layer]),
        "mla_qg": mla_qn_g[layer][None, :],
        "mla_kvg": mla_kvn_g[layer][None, :],
        "w_uq": jnp.concatenate(uq_cols, axis=1).astype(BF),
        "w_uk": jnp.concatenate(uk_cols, axis=1).astype(BF),
        "w_uv": jnp.concatenate(uv_cols, axis=1).astype(BF),
        "pool_w": w_bd.astype(BF),
        "pool_scale": pool_scale[layer][None, :],
        "w_o1": w_o1,
        "w_o2": w_o2,
        "peer_wq": peer_wq[layer].astype(BF),
        "peer_sk": peer_subkeys[layer].astype(BF),
        "peer_u": peer_u[layer].astype(BF),
        "peer_vt": _peer_vt_blocks(peer_v[layer]),
    }


def _tile(l, pref):
    return pref if l % pref == 0 else l


def kernel(x, c, ctx, c_ctx, ada_w, ada_b, norm1_g, norm2_g, w_in, gqa_qn_g, gqa_kn_g, pool_w, pool_scale,
           mla_qn_g, mla_kvn_g, mla_w_uq, mla_w_ukv, w_out, peer_wq, peer_subkeys, peer_u, peer_v, final_g):
    b, l, d = x.shape
    lc = ctx.shape[1]
    depth = ada_w.shape[0]
    assert d == D_MODEL and l % GRID_W == 0
    assert (b * l) % PEER_TOK == 0 and (b * lc) % PEER_TOK == 0

    n_rows = -(-(b + 1) // 8) * 8
    cc = jnp.concatenate([c, c_ctx[None, :], jnp.zeros((n_rows - b - 1, d), F32)], axis=0)
    mod_all = _adaln(cc, ada_w, ada_b).reshape(depth, n_rows, N_MOD, d)

    rope_tabs = _rope_tables(l)
    tl_l, tl_c = _tile(l, 1024), _tile(lc, 512)
    ti_l, ti_c = _tile(l, 256), _tile(lc, 256)
    tq_l, tq_c = _tile(l, 512), _tile(lc, 256)
    fg = final_g[None, :]

    xl, xc = x, ctx
    for layer in range(depth):
        lw = _layer_weights(layer, norm1_g, norm2_g, w_in, gqa_qn_g, gqa_kn_g, pool_w, pool_scale, mla_qn_g,
                            mla_kvn_g, mla_w_uq, mla_w_ukv, w_out, peer_wq, peer_subkeys, peer_u, peer_v)
        mod = mod_all[layer]
        last = layer == depth - 1
        q_l, k_l, v_l, pb_l = _inproj(xl, mod, None, lw, rope_tabs, ti_l)
        q_c, k_c, v_c, pb_c = _inproj(xc, mod, b, lw, None, ti_c)

        att_l = _attn(q_l, [k_c, k_l], [v_c, v_l], tq_l)
        ob_l = _pool(pb_l, lw["pool_w"], lw["pool_scale"])
        xl_mid, hn_l = _outproj(xl, att_l, ob_l, mod, None, lw, tl_l)
        xl = _peer(hn_l.reshape(b * l, d), xl_mid.reshape(b * l, d), mod, None, l, fg, last, lw).reshape(b, l, d)

        if not last:
            att_c = _attn(q_c, [k_c], [v_c], tq_c)
            ob_c = _pool(pb_c, lw["pool_w"], lw["pool_scale"])
            xc_mid, hn_c = _outproj(xc, att_c, ob_c, mod, b, lw, tl_c)
            xc = _peer(hn_c.reshape(b * lc, d), xc_mid.reshape(b * lc, d), mod, b, lc, fg, False,
                       lw).reshape(b, lc, d)
    return xl
```
